```python
import jax, jax.numpy as jnp
from jax import lax
import numpy as np

D_MODEL = 1024
BATCH = 8
SEQ = 2048
DEPTH = 1
DEC_BATCH = 128
DEC_SEQ = 8
PAST_LEN = 16384
PAGE_SIZE = 128

D_MIX = D_MODEL
D_POOL = D_MIX // 2
D_LRU = D_MIX - D_POOL
POOL_WINDOWS = (2, 4, 8, 16)
N_POOL_GROUPS = len(POOL_WINDOWS)
POOL_GROUP = D_POOL // N_POOL_GROUPS
POOL_BUF = max(POOL_WINDOWS) - 1
N_LRU_HEADS = 8
LRU_HEAD = D_LRU // N_LRU_HEADS
CONV_W = 4
LRU_C = 8.0
D_IN = D_POOL + 2 * D_LRU
N_EXPERT_GROUPS = 4
EXPERTS_PER_GROUP = 4
N_EXPERTS = N_EXPERT_GROUPS * EXPERTS_PER_GROUP
TOP_K = 2
D_EXPERT = D_MODEL // 4
ALPHA = (2.0 * DEPTH) ** 0.25
BETA = (8.0 * DEPTH) ** -0.25
LN_EPS = 1e-5

kernel_name = 'hymba_pool_rglru_hmoe_deepnorm_step'


def _layer_norm(x, g, b):
    x32 = x.astype(jnp.float32)
    mu = jnp.mean(x32, -1, keepdims=True)
    var = jnp.mean(jnp.square(x32 - mu), -1, keepdims=True)
    y = (x32 - mu) * lax.rsqrt(var + LN_EPS) * g.astype(jnp.float32) + b.astype(jnp.float32)
    return y.astype(x.dtype)


def _pool_mixer(p, buf, n_prev, w_pool, pool_scale):
    bsz, t_len, _ = p.shape
    z = jnp.concatenate([buf.astype(p.dtype), p], axis=1)
    z32 = z.astype(jnp.float32)
    cs = jnp.concatenate([jnp.zeros_like(z32[:, :1]), jnp.cumsum(z32, axis=1)], axis=1)
    pos = jnp.arange(t_len, dtype=jnp.float32)
    pooled = []
    for g, w in enumerate(POOL_WINDOWS):
        ch = slice(g * POOL_GROUP, (g + 1) * POOL_GROUP)
        hi = cs[:, POOL_BUF + 1:POOL_BUF + 1 + t_len, ch]
        lo = cs[:, POOL_BUF + 1 - w:POOL_BUF + 1 - w + t_len, ch]
        cnt = jnp.minimum(jnp.float32(w), pos + jnp.float32(1 + n_prev))
        pooled.append((hi - lo) / cnt[None, :, None])
    d = jnp.concatenate(pooled, -1) - p.astype(jnp.float32)
    d = d.astype(p.dtype).reshape(bsz, t_len, N_POOL_GROUPS, POOL_GROUP)
    y = jnp.einsum('btgc,gcd->btgd', d, w_pool).reshape(bsz, t_len, D_POOL) * pool_scale
    return y, z[:, -POOL_BUF:]


def _causal_conv(xr, buf, w_conv, b_conv):
    t_len = xr.shape[1]
    z = jnp.concatenate([buf.astype(xr.dtype), xr], axis=1)
    y = b_conv + z[:, 0:t_len] * w_conv[0]
    for k in range(1, CONV_W):
        y = y + z[:, k:k + t_len] * w_conv[k]
    return y, z[:, -(CONV_W - 1):]


def _lin_combine(left, right):
    a_l, b_l = left
    a_r, b_r = right
    return a_l * a_r, a_r * b_l + b_r


def _rg_lru(xc, h0, w_a, b_a, w_x, b_x, lam):
    bsz, t_len, _ = xc.shape
    xh = xc.reshape(bsz, t_len, N_LRU_HEADS, LRU_HEAD)
    r = jax.nn.sigmoid((jnp.einsum('bthi,hij->bthj', xh, w_a).reshape(bsz, t_len, D_LRU) + b_a).astype(jnp.float32))
    i = jax.nn.sigmoid((jnp.einsum('bthi,hij->bthj', xh, w_x).reshape(bsz, t_len, D_LRU) + b_x).astype(jnp.float32))
    log_a = -LRU_C * r * jax.nn.softplus(-lam.astype(jnp.float32))
    a = jnp.exp(log_a)
    mult = jnp.sqrt(jnp.maximum(-jnp.expm1(2.0 * log_a), 0.0))
    b = mult * i * xc.astype(jnp.float32)
    a_cum, h_zero = lax.associative_scan(_lin_combine, (a, b), axis=1)
    h = a_cum * h0.astype(jnp.float32)[:, None, :] + h_zero
    return h.astype(xc.dtype), h[:, -1].astype(h0.dtype)


def _hier_moe(u, w_group, b_group, w_route, b_route, w_gate, w_up, w_down):
    shp = u.shape
    un = u.reshape(-1, D_MODEL)
    n_tok = un.shape[0]
    g_logits = (un @ w_group + b_group).astype(jnp.float32)
    p_group = jax.nn.softmax(g_logits, axis=-1)
    g_sel = jnp.argmax(g_logits, axis=-1)
    e_logits = (un @ w_route + b_route).astype(jnp.float32).reshape(n_tok, N_EXPERT_GROUPS, EXPERTS_PER_GROUP)
    e_sel = jnp.take_along_axis(e_logits, g_sel[:, None, None], axis=1)[:, 0]
    top_v, top_i = lax.top_k(e_sel, TOP_K)
    top_w = jax.nn.softmax(top_v, axis=-1) * jnp.take_along_axis(p_group, g_sel[:, None], axis=1)
    w_in_group = jnp.einsum('nk,nke->ne', top_w, jax.nn.one_hot(top_i, EXPERTS_PER_GROUP, dtype=jnp.float32))
    comb = (jax.nn.one_hot(g_sel, N_EXPERT_GROUPS, dtype=jnp.float32)[:, :, None] * w_in_group[:, None, :])
    comb = comb.reshape(n_tok, N_EXPERTS).astype(u.dtype)
    hg = jnp.einsum('nd,edf->nef', un, w_gate)
    hu = jnp.einsum('nd,edf->nef', un, w_up)
    h = jax.nn.silu(hg) * hu * comb[:, :, None]
    y = jnp.einsum('nef,efd->nd', h, w_down)
    return y.reshape(shp)


def _layer(x, c, pool_buf, n_prev, conv_buf, h0,
           w_ada, b_ada, w_in, w_pool, pool_scale, w_conv, b_conv, w_a, b_a, w_x, b_x, lam,
           w_out, ln1_g, ln1_b, w_group, b_group, w_route, b_route, w_gate, w_up, w_down, ln2_g, ln2_b):
    mod = c @ w_ada + b_ada
    sh1, sc1, g1, sh2, sc2, g2 = jnp.split(mod, 6, axis=-1)
    u = x * (1.0 + sc1[:, None]) + sh1[:, None]
    proj = u @ w_in
    p = proj[..., :D_POOL]
    xr = proj[..., D_POOL:D_POOL + D_LRU]
    gr = proj[..., D_POOL + D_LRU:]
    y_pool, new_pool = _pool_mixer(p, pool_buf, n_prev, w_pool, pool_scale)
    xc, new_conv = _causal_conv(xr, conv_buf, w_conv, b_conv)
    h_seq, h_last = _rg_lru(xc, h0, w_a, b_a, w_x, b_x, lam)
    y_lru = h_seq * jax.nn.gelu(gr)
    mix = jnp.concatenate([y_pool, y_lru], axis=-1) @ w_out
    x = _layer_norm(ALPHA * x + (1.0 + g1[:, None]) * mix, ln1_g, ln1_b)
    u2 = x * (1.0 + sc2[:, None]) + sh2[:, None]
    y_moe = _hier_moe(u2, w_group, b_group, w_route, b_route, w_gate, w_up, w_down)
    x = _layer_norm(ALPHA * x + (1.0 + g2[:, None]) * y_moe, ln2_g, ln2_b)
    return x, new_pool, new_conv, h_last


def setup_inputs(seed: int = 0) -> dict:
    key = jax.random.key(seed)
    ks = jax.random.split(key, 32)

    def nrm(k, shape, scale):
        return jax.random.normal(k, shape, jnp.float32) * scale

    a0 = jax.random.uniform(ks[19], (DEPTH, D_LRU), jnp.float32, 0.9, 0.999)
    return {
        'x_prompt': nrm(ks[0], (BATCH, SEQ, D_MODEL), 1.0),
        'x_sample': nrm(ks[1], (DEC_BATCH, DEC_SEQ, D_MODEL), 1.0),
        'c_prompt': nrm(ks[2], (BATCH, D_MODEL), 1.0),
        'c_sample': nrm(ks[3], (DEC_BATCH, D_MODEL), 1.0),
        'state_pool': nrm(ks[4], (DEPTH, DEC_BATCH, POOL_BUF, D_POOL), 1.0),
        'state_conv': nrm(ks[5], (DEPTH, DEC_BATCH, CONV_W - 1, D_LRU), 1.0),
        'state_lru': nrm(ks[6], (DEPTH, DEC_BATCH, D_LRU), 0.5),
        'w_ada': nrm(ks[7], (DEPTH, D_MODEL, 6 * D_MODEL), 0.1 * D_MODEL ** -0.5),
        'b_ada': nrm(ks[8], (DEPTH, 6 * D_MODEL), 0.01),
        'w_in': nrm(ks[9], (DEPTH, D_MODEL, D_IN), D_MODEL ** -0.5),
        'w_pool': nrm(ks[10], (DEPTH, N_POOL_GROUPS, POOL_GROUP, POOL_GROUP), POOL_GROUP ** -0.5),
        'pool_scale': 1.0 + nrm(ks[11], (DEPTH, D_POOL), 0.1),
        'w_conv': nrm(ks[12], (DEPTH, CONV_W, D_LRU), CONV_W ** -0.5),
        'b_conv': nrm(ks[13], (DEPTH, D_LRU), 0.01),
        'w_a': nrm(ks[14], (DEPTH, N_LRU_HEADS, LRU_HEAD, LRU_HEAD), LRU_HEAD ** -0.5),
        'b_a': nrm(ks[15], (DEPTH, D_LRU), 0.01),
        'w_x': nrm(ks[16], (DEPTH, N_LRU_HEADS, LRU_HEAD, LRU_HEAD), LRU_HEAD ** -0.5),
        'b_x': nrm(ks[17], (DEPTH, D_LRU), 0.01),
        'lru_lambda': jnp.log(a0) - jnp.log1p(-a0),
        'w_out': nrm(ks[18], (DEPTH, D_MIX, D_MODEL), BETA * D_MIX ** -0.5),
        'ln1_g': 1.0 + nrm(ks[20], (DEPTH, D_MODEL), 0.01),
        'ln1_b': nrm(ks[21], (DEPTH, D_MODEL), 0.01),
        'w_group': nrm(ks[22], (DEPTH, D_MODEL, N_EXPERT_GROUPS), D_MODEL ** -0.5),
        'b_group': nrm(ks[23], (DEPTH, N_EXPERT_GROUPS), 0.01),
        'w_route': nrm(ks[24], (DEPTH, D_MODEL, N_EXPERTS), D_MODEL ** -0.5),
        'b_route': nrm(ks[25], (DEPTH, N_EXPERTS), 0.01),
        'w_gate': nrm(ks[26], (DEPTH, N_EXPERTS, D_MODEL, D_EXPERT), D_MODEL ** -0.5),
        'w_up': nrm(ks[27], (DEPTH, N_EXPERTS, D_MODEL, D_EXPERT), D_MODEL ** -0.5),
        'w_down': nrm(ks[28], (DEPTH, N_EXPERTS, D_EXPERT, D_MODEL), BETA * D_EXPERT ** -0.5),
        'ln2_g': 1.0 + nrm(ks[29], (DEPTH, D_MODEL), 0.01),
        'ln2_b': nrm(ks[30], (DEPTH, D_MODEL), 0.01),
    }


def reference(x_prompt, x_sample, c_prompt, c_sample, state_pool, state_conv, state_lru,
              w_ada, b_ada, w_in, w_pool, pool_scale, w_conv, b_conv, w_a, b_a, w_x, b_x, lru_lambda,
              w_out, ln1_g, ln1_b, w_group, b_group, w_route, b_route, w_gate, w_up, w_down, ln2_g, ln2_b):
    n_prev_sample = min(PAST_LEN, POOL_BUF)
    bp = x_prompt.shape[0]
    xp, xs = x_prompt, x_sample
    pool_p, conv_p, lru_p, pool_s, conv_s, lru_s = [], [], [], [], [], []
    for l in range(DEPTH):
        wl = (w_ada[l], b_ada[l], w_in[l], w_pool[l], pool_scale[l], w_conv[l], b_conv[l],
              w_a[l], b_a[l], w_x[l], b_x[l], lru_lambda[l], w_out[l], ln1_g[l], ln1_b[l],
              w_group[l], b_group[l], w_route[l], b_route[l], w_gate[l], w_up[l], w_down[l],
              ln2_g[l], ln2_b[l])
        zero_pool = jnp.zeros((bp, POOL_BUF, D_POOL), x_prompt.dtype)
        zero_conv = jnp.zeros((bp, CONV_W - 1, D_LRU), x_prompt.dtype)
        zero_h = jnp.zeros((bp, D_LRU), state_lru.dtype)
        xp, npl, ncv, nh = _layer(xp, c_prompt, zero_pool, 0, zero_conv, zero_h, *wl)
        pool_p.append(npl); conv_p.append(ncv); lru_p.append(nh)
        xs, npl, ncv, nh = _layer(xs, c_sample, state_pool[l], n_prev_sample, state_conv[l], state_lru[l], *wl)
        pool_s.append(npl); conv_s.append(ncv); lru_s.append(nh)
    return (xp, xs, jnp.stack(pool_p), jnp.stack(conv_p), jnp.stack(lru_p),
            jnp.stack(pool_s), jnp.stack(conv_s), jnp.stack(lru_s))
```

```python
import functools
import math

import jax
import jax.numpy as jnp
from jax import lax
from jax.experimental import pallas as pl
from jax.experimental.pallas import tpu as pltpu

D = 1024
D_POOL = 512
D_LRU = 512
D_IN = D_POOL + 2 * D_LRU
POOL_WINDOWS = (2, 4, 8, 16)
POOL_GROUP = 128
POOL_BUF = 15
N_HEADS = 8
CONV_W = 4
LRU_C = 8.0
N_GROUPS = 4
EPG = 4
N_EXPERTS = 16
D_EXPERT = 256
DEPTH = 1
ALPHA = (2.0 * DEPTH) ** 0.25
LN_EPS = 1e-5
PAST_LEN = 16384

LANES = 128
SUBLANES = 8
N_SLABS = D // LANES
ROUTE_ROWS = 48
VMEM_LIMIT = 60 * 1024 * 1024

f32 = jnp.float32
bf16 = jnp.bfloat16


def _dot(a, b):
    return jnp.dot(a, b, preferred_element_type=f32)


def _sigmoid(x):
    return 1.0 / (1.0 + jnp.exp(-x))


def _gelu_tanh(x):
    c = math.sqrt(2.0 / math.pi)
    return 0.5 * x * (1.0 + jnp.tanh(c * (x + 0.044715 * (x * x * x))))


def _layer_norm(v, g, b):
    mu = jnp.mean(v, axis=-1, keepdims=True)
    c = v - mu
    var = jnp.mean(c * c, axis=-1, keepdims=True)
    return c * lax.rsqrt(var + LN_EPS) * g + b


def _ada_body(c_ref, w_ref, b_ref, o_ref):
    o_ref[...] = _dot(c_ref[...].astype(bf16), w_ref[...].astype(bf16)) + b_ref[...]


def _ada_mod(c_all, w_ada, b_ada):
    n = c_all.shape[0]
    bn = 512
    return pl.pallas_call(
        _ada_body,
        grid=(6 * D // bn,),
        in_specs=[
            pl.BlockSpec((n, D), lambda j: (0, 0)),
            pl.BlockSpec((D, bn), lambda j: (0, j)),
            pl.BlockSpec((1, bn), lambda j: (0, j)),
        ],
        out_specs=pl.BlockSpec((n, bn), lambda j: (0, j)),
        out_shape=jax.ShapeDtypeStruct((n, 6 * D), f32),
        name="ada_mod",
    )(c_all, w_ada, b_ada.reshape(1, 6 * D))


def _layer_body(nb, tb, pitch, n_prev,
                x_ref, mod_ref, pool0_ref, conv0_ref, h0_ref,
                w_in_ref, w_pool_ref, pscale_ref, w_conv_ref, b_conv_ref,
                w_ax_ref, b_ax_ref, lam_ref, w_out_ref, ln1g_ref, ln1b_ref,
                w_rt_ref, b_rt_ref, wg_ref, wu_ref, wd_ref, ln2g_ref, ln2b_ref,
                y_ref, npool_ref, nconv_ref, nh_ref,
                slab, xt, x1, u2, ycat, yacc, comb, zpool, zconv, a_s, b_s, gl_s, h_s):
    tm = nb * tb
    ti = pl.program_id(1)
    g = pl.program_id(2)

    def mod_part(k):
        return mod_ref[:, pl.ds(k * D, D)]

    @pl.when(g == 0)
    def _mix_and_route():
        for b in range(nb):
            for j in range(N_SLABS):
                slab[j, pl.ds(b * pitch, tb), :] = x_ref[b, :, pl.ds(j * LANES, LANES)]
        for t in range(tb):
            for j in range(N_SLABS):
                xt[t, :, pl.ds(j * LANES, LANES)] = slab[j, pl.ds(t, nb, stride=pitch), :]

        @pl.when(ti == 0)
        def _init_state():
            zpool[pl.ds(0, POOL_BUF * nb), :] = pool0_ref[...].reshape(POOL_BUF * nb, D_POOL)
            zconv[pl.ds(0, (CONV_W - 1) * nb), :] = conv0_ref[...].reshape((CONV_W - 1) * nb, D_LRU)
            h_s[...] = h0_ref[...]

        sh1 = mod_part(0)
        sc1 = mod_part(1)
        u = (xt[...] * (1.0 + sc1)[None] + sh1[None]).reshape(tm, D)
        proj = _dot(u.astype(bf16), w_in_ref[...])
        zpool[pl.ds(POOL_BUF * nb, tm), :] = proj[:, :D_POOL]
        zconv[pl.ds((CONV_W - 1) * nb, tm), :] = proj[:, D_POOL:D_POOL + D_LRU]
        gl_s[...] = _gelu_tanh(proj[:, D_POOL + D_LRU:])

        row = lax.broadcasted_iota(jnp.int32, (tm, LANES), 0)
        t_loc = lax.shift_right_logical(row, int(math.log2(nb)))
        t_glob = (ti * tb + t_loc + (1 + n_prev)).astype(f32)
        for c, w in enumerate(POOL_WINDOWS):
            lanes = pl.ds(c * LANES, LANES)
            s = zpool[pl.ds((POOL_BUF + 1 - w) * nb, tm + (w - 1) * nb), lanes]
            step = 1
            while step < w:
                s = s[step * nb:] + s[:-step * nb]
                step *= 2
            cnt = jnp.minimum(f32(w), t_glob)
            dlt = s / cnt - zpool[pl.ds(POOL_BUF * nb, tm), lanes]
            yp = _dot(dlt.astype(bf16), w_pool_ref[c]) * pscale_ref[:, lanes]
            ycat[:, lanes] = yp.astype(bf16)

        xc = b_conv_ref[...] + zconv[pl.ds(0, tm), :] * w_conv_ref[0:1, :]
        for k in range(1, CONV_W):
            xc = xc + zconv[pl.ds(k * nb, tm), :] * w_conv_ref[k:k + 1, :]
        gates = _dot(xc.astype(bf16), w_ax_ref[...]) + b_ax_ref[...]
        r = _sigmoid(gates[:, :D_LRU])
        ig = _sigmoid(gates[:, D_LRU:])
        nl = -lam_ref[...]
        softplus = jnp.maximum(nl, 0.0) + jnp.log(1.0 + jnp.exp(-jnp.abs(nl)))
        log_a = (-LRU_C) * r * softplus
        a = jnp.exp(log_a)
        mult = jnp.sqrt(jnp.maximum(1.0 - jnp.exp(2.0 * log_a), 0.0))
        a_s[...] = a
        b_s[...] = mult * ig * xc

        def scan_step(t, h):
            rows = pl.ds(pl.multiple_of(t * nb, nb), nb)
            h = a_s[rows, :] * h + b_s[rows, :]
            b_s[rows, :] = h
            return h

        h_last = lax.fori_loop(0, tb, scan_step, h_s[...], unroll=min(tb, 8))
        h_s[...] = h_last
        ycat[:, pl.ds(D_POOL, D_LRU)] = (b_s[...] * gl_s[...]).astype(bf16)

        new_pool = zpool[pl.ds(tm, POOL_BUF * nb), :]
        new_conv = zconv[pl.ds(tm, (CONV_W - 1) * nb), :]
        zpool[pl.ds(0, POOL_BUF * nb), :] = new_pool
        zconv[pl.ds(0, (CONV_W - 1) * nb), :] = new_conv

        @pl.when(ti == pl.num_programs(1) - 1)
        def _emit_state():
            npool_ref[...] = new_pool.reshape(POOL_BUF, nb, D_POOL)
            nconv_ref[...] = new_conv.reshape(CONV_W - 1, nb, D_LRU)
            nh_ref[...] = h_last

        mix = _dot(ycat[...], w_out_ref[...]).reshape(tb, nb, D)
        g1 = mod_part(2)
        v = ALPHA * xt[...] + (1.0 + g1)[None] * mix
        xn = _layer_norm(v, ln1g_ref[...][None], ln1b_ref[...][None])
        x1[...] = xn
        sh2 = mod_part(3)
        sc2 = mod_part(4)
        u2v = (xn * (1.0 + sc2)[None] + sh2[None]).reshape(tm, D).astype(bf16)
        u2[...] = u2v

        lt = lax.dot_general(w_rt_ref[...], u2v, (((1,), (1,)), ((), ())),
                             preferred_element_type=f32) + b_rt_ref[...]
        gl = [lt[k:k + 1, :] for k in range(N_GROUPS)]
        best = gl[0]
        gsel = jnp.zeros_like(best, dtype=jnp.int32)
        for k in range(1, N_GROUPS):
            better = gl[k] > best
            best = jnp.where(better, gl[k], best)
            gsel = jnp.where(better, k, gsel)
        denom = jnp.exp(gl[0] - best)
        for k in range(1, N_GROUPS):
            denom = denom + jnp.exp(gl[k] - best)
        p_sel = 1.0 / denom
        es = []
        for j in range(EPG):
            v_j = lt[SUBLANES + j:SUBLANES + j + 1, :]
            for k in range(1, N_GROUPS):
                r0 = SUBLANES * (k + 1) + j
                v_j = jnp.where(gsel == k, lt[r0:r0 + 1, :], v_j)
            es.append(v_j)
        v1 = es[0]
        i1 = jnp.zeros_like(gsel)
        for j in range(1, EPG):
            better = es[j] > v1
            v1 = jnp.where(better, es[j], v1)
            i1 = jnp.where(better, j, i1)
        neg = f32(-jnp.inf)
        v2 = jnp.full_like(v1, neg)
        i2 = jnp.full_like(gsel, -1)
        for j in range(EPG):
            cand = jnp.logical_and(i1 != j, jnp.logical_or(i2 < 0, es[j] > v2))
            v2 = jnp.where(cand, es[j], v2)
            i2 = jnp.where(cand, j, i2)
        e21 = jnp.exp(v2 - v1)
        w1 = p_sel / (1.0 + e21)
        w2 = p_sel * e21 / (1.0 + e21)
        rid = lax.broadcasted_iota(jnp.int32, (LANES, tm), 0)
        for k in range(N_GROUPS):
            in_grp = gsel == k
            ct = jnp.zeros((LANES, tm), f32)
            for j in range(EPG):
                wj = jnp.where(jnp.logical_and(in_grp, i1 == j), w1,
                               jnp.where(jnp.logical_and(in_grp, i2 == j), w2, 0.0))
                ct = jnp.where(rid == j, wj, ct)
            comb[k] = ct.T

    u2v = u2[...]
    hg = _dot(u2v, wg_ref[0])
    hu = _dot(u2v, wu_ref[0])
    act = hg * _sigmoid(hg) * hu
    cw = comb[g]
    parts = [act[:, e * D_EXPERT:(e + 1) * D_EXPERT] * cw[:, e:e + 1] for e in range(EPG)]
    hb = jnp.concatenate(parts, axis=1).astype(bf16)
    contrib = _dot(hb, wd_ref[0])

    @pl.when(g == 0)
    def _first():
        yacc[...] = contrib

    @pl.when(g > 0)
    def _rest():
        yacc[...] += contrib

    @pl.when(g == N_GROUPS - 1)
    def _finish():
        g2 = mod_part(5)
        v = ALPHA * x1[...] + (1.0 + g2)[None] * yacc[...].reshape(tb, nb, D)
        xt[...] = _layer_norm(v, ln2g_ref[...][None], ln2b_ref[...][None])
        for t in range(tb):
            for j in range(N_SLABS):
                slab[j, pl.ds(t, nb, stride=pitch), :] = xt[t, :, pl.ds(j * LANES, LANES)]
        for b in range(nb):
            for j in range(N_SLABS):
                y_ref[b, :, pl.ds(j * LANES, LANES)] = slab[j, pl.ds(b * pitch, tb), :]


def _run_layer(x, mod, pool0, conv0, h0, n_prev, nb, tb, wts):
    bsz, t_len, _ = x.shape
    n_sb = bsz // nb
    n_tt = t_len // tb
    tm = nb * tb
    pitch = tb + SUBLANES if (tb // SUBLANES) % 2 == 0 else tb
    const2 = lambda s, t, g: (0, 0)
    const3 = lambda s, t, g: (0, 0, 0)
    grp3 = lambda s, t, g: (g, 0, 0)
    in_specs = [
        pl.BlockSpec((nb, tb, D), lambda s, t, g: (s, t, 0)),
        pl.BlockSpec((nb, 6 * D), lambda s, t, g: (s, 0)),
        pl.BlockSpec((POOL_BUF, nb, D_POOL), lambda s, t, g: (0, s, 0)),
        pl.BlockSpec((CONV_W - 1, nb, D_LRU), lambda s, t, g: (0, s, 0)),
        pl.BlockSpec((nb, D_LRU), lambda s, t, g: (s, 0)),
        pl.BlockSpec((D, D_IN), const2, pipeline_mode=pl.Buffered(1)),
        pl.BlockSpec((4, POOL_GROUP, POOL_GROUP), const3),
        pl.BlockSpec((1, D_POOL), const2),
        pl.BlockSpec((CONV_W, D_LRU), const2),
        pl.BlockSpec((1, D_LRU), const2),
        pl.BlockSpec((D_LRU, 2 * D_LRU), const2, pipeline_mode=pl.Buffered(1)),
        pl.BlockSpec((1, 2 * D_LRU), const2),
        pl.BlockSpec((1, D_LRU), const2),
        pl.BlockSpec((D, D), const2, pipeline_mode=pl.Buffered(1)),
        pl.BlockSpec((1, D), const2),
        pl.BlockSpec((1, D), const2),
        pl.BlockSpec((ROUTE_ROWS, D), const2),
        pl.BlockSpec((ROUTE_ROWS, 1), const2),
        pl.BlockSpec((1, D, D), grp3),
        pl.BlockSpec((1, D, D), grp3),
        pl.BlockSpec((1, D, D), grp3),
        pl.BlockSpec((1, D), const2),
        pl.BlockSpec((1, D), const2),
    ]
    out_specs = [
        pl.BlockSpec((nb, tb, D), lambda s, t, g: (s, t, 0)),
        pl.BlockSpec((POOL_BUF, nb, D_POOL), lambda s, t, g: (0, s, 0)),
        pl.BlockSpec((CONV_W - 1, nb, D_LRU), lambda s, t, g: (0, s, 0)),
        pl.BlockSpec((nb, D_LRU), lambda s, t, g: (s, 0)),
    ]
    out_shape = [
        jax.ShapeDtypeStruct((bsz, t_len, D), f32),
        jax.ShapeDtypeStruct((POOL_BUF, bsz, D_POOL), f32),
        jax.ShapeDtypeStruct((CONV_W - 1, bsz, D_LRU), f32),
        jax.ShapeDtypeStruct((bsz, D_LRU), f32),
    ]
    scratch = [
        pltpu.VMEM((N_SLABS, nb * pitch, LANES), f32),
        pltpu.VMEM((tb, nb, D), f32),
        pltpu.VMEM((tb, nb, D), f32),
        pltpu.VMEM((tm, D), bf16),
        pltpu.VMEM((tm, D), bf16),
        pltpu.VMEM((tm, D), f32),
        pltpu.VMEM((N_GROUPS, tm, LANES), f32),
        pltpu.VMEM(((tb + POOL_BUF) * nb, D_POOL), f32),
        pltpu.VMEM(((tb + CONV_W - 1) * nb, D_LRU), f32),
        pltpu.VMEM((tm, D_LRU), f32),
        pltpu.VMEM((tm, D_LRU), f32),
        pltpu.VMEM((tm, D_LRU), f32),
        pltpu.VMEM((nb, D_LRU), f32),
    ]
    body = functools.partial(_layer_body, nb, tb, pitch, n_prev)
    return pl.pallas_call(
        body,
        grid=(n_sb, n_tt, N_GROUPS),
        in_specs=in_specs,
        out_specs=out_specs,
        out_shape=out_shape,
        scratch_shapes=scratch,
        compiler_params=pltpu.CompilerParams(
            dimension_semantics=("arbitrary", "arbitrary", "arbitrary"),
            vmem_limit_bytes=VMEM_LIMIT),
        name=f"layer_nb{nb}_tb{tb}",
    )(x, mod, pool0, conv0, h0, *wts)


def _block_diag(w):
    n, k, _ = w.shape
    eye = jnp.eye(n, dtype=w.dtype)
    return jnp.einsum('hij,hg->higj', w, eye).reshape(n * k, n * k)


def _group_cols(w):
    return w.reshape(N_GROUPS, EPG, D, D_EXPERT).transpose(0, 2, 1, 3).reshape(N_GROUPS, D, EPG * D_EXPERT)


def kernel(x_prompt, x_sample, c_prompt, c_sample, state_pool, state_conv, state_lru, w_ada, b_ada, w_in, w_pool, pool_scale, w_conv, b_conv, w_a, b_a, w_x, b_x, lru_lambda, w_out, ln1_g, ln1_b, w_group, b_group, w_route, b_route, w_gate, w_up, w_down, ln2_g, ln2_b):
    l = 0
    bp = x_prompt.shape[0]
    bs = x_sample.shape[0]
    mod = _ada_mod(jnp.concatenate([c_prompt, c_sample], axis=0), w_ada[l], b_ada[l])

    w_rt = jnp.zeros((ROUTE_ROWS, D), f32).at[0:N_GROUPS].set(w_group[l].T)
    b_rt = jnp.zeros((ROUTE_ROWS,), f32).at[0:N_GROUPS].set(b_group[l])
    for k in range(N_GROUPS):
        r0 = SUBLANES * (k + 1)
        w_rt = w_rt.at[r0:r0 + EPG].set(w_route[l][:, k * EPG:(k + 1) * EPG].T)
        b_rt = b_rt.at[r0:r0 + EPG].set(b_route[l][k * EPG:(k + 1) * EPG])

    wts = (
        w_in[l].astype(bf16),
        w_pool[l].astype(bf16),
        pool_scale[l].reshape(1, D_POOL),
        w_conv[l],
        b_conv[l].reshape(1, D_LRU),
        jnp.concatenate([_block_diag(w_a[l]), _block_diag(w_x[l])], axis=1).astype(bf16),
        jnp.concatenate([b_a[l], b_x[l]]).reshape(1, 2 * D_LRU),
        lru_lambda[l].reshape(1, D_LRU),
        w_out[l].astype(bf16),
        ln1_g[l].reshape(1, D),
        ln1_b[l].reshape(1, D),
        w_rt.astype(bf16),
        b_rt.reshape(ROUTE_ROWS, 1),
        _group_cols(w_gate[l]).astype(bf16),
        _group_cols(w_up[l]).astype(bf16),
        w_down[l].reshape(N_GROUPS, EPG * D_EXPERT, D).astype(bf16),
        ln2_g[l].reshape(1, D),
        ln2_b[l].reshape(1, D),
    )

    zp = jnp.zeros((POOL_BUF, bp, D_POOL), f32)
    zc = jnp.zeros((CONV_W - 1, bp, D_LRU), f32)
    zh = jnp.zeros((bp, D_LRU), f32)
    yp, pool_p, conv_p, lru_p = _run_layer(x_prompt, mod[:bp], zp, zc, zh, 0, bp, 64, wts)

    n_prev_s = min(PAST_LEN, POOL_BUF)
    ys, pool_s, conv_s, lru_s = _run_layer(
        x_sample, mod[bp:], state_pool[l].transpose(1, 0, 2), state_conv[l].transpose(1, 0, 2),
        state_lru[l], n_prev_s, 64, x_sample.shape[1], wts)

    tr = lambda a: a.transpose(1, 0, 2)[None]
    return (yp, ys, tr(pool_p), tr(conv_p), lru_p[None], tr(pool_s), tr(conv_s), lru_s[None])
```

```python
import functools
import math

import jax
import jax.numpy as jnp
from jax import lax
from jax.experimental import pallas as pl
from jax.experimental.pallas import tpu as pltpu

D = 1024
D_POOL = 512
D_LRU = 512
D_IN = D_POOL + 2 * D_LRU
POOL_WINDOWS = (2, 4, 8, 16)
POOL_GROUP = 128
POOL_BUF = 15
N_HEADS = 8
CONV_W = 4
LRU_C = 8.0
N_GROUPS = 4
EPG = 4
N_EXPERTS = 16
D_EXPERT = 256
DEPTH = 1
ALPHA = (2.0 * DEPTH) ** 0.25
LN_EPS = 1e-5
PAST_LEN = 16384

LANES = 128
SUBLANES = 8
N_SLABS = D // LANES
ROUTE_ROWS = 48
VMEM_LIMIT = 60 * 1024 * 1024

f32 = jnp.float32
bf16 = jnp.bfloat16


def _dot(a, b):
    return jnp.dot(a, b, preferred_element_type=f32)


def _sigmoid(x):
    return 1.0 / (1.0 + jnp.exp(-x))


def _gelu_tanh(x):
    c = math.sqrt(2.0 / math.pi)
    return 0.5 * x * (1.0 + jnp.tanh(c * (x + 0.044715 * (x * x * x))))


def _layer_norm(v, g, b):
    mu = jnp.mean(v, axis=-1, keepdims=True)
    c = v - mu
    var = jnp.mean(c * c, axis=-1, keepdims=True)
    return c * lax.rsqrt(var + LN_EPS) * g + b


def _ada_body(c_ref, w_ref, b_ref, o_ref):
    o_ref[...] = _dot(c_ref[...].astype(bf16), w_ref[...].astype(bf16)) + b_ref[...]


def _ada_mod(c_all, w_ada, b_ada):
    n = c_all.shape[0]
    bn = 512
    return pl.pallas_call(
        _ada_body,
        grid=(6 * D // bn,),
        in_specs=[
            pl.BlockSpec((n, D), lambda j: (0, 0)),
            pl.BlockSpec((D, bn), lambda j: (0, j)),
            pl.BlockSpec((1, bn), lambda j: (0, j)),
        ],
        out_specs=pl.BlockSpec((n, bn), lambda j: (0, j)),
        out_shape=jax.ShapeDtypeStruct((n, 6 * D), f32),
        name="ada_mod",
    )(c_all, w_ada, b_ada.reshape(1, 6 * D))


def _layer_body(nb, tb, pitch, n_prev,
                x_ref, mod_ref, pool0_ref, conv0_ref, h0_ref,
                w_in_ref, w_pool_ref, pscale_ref, w_conv_ref, b_conv_ref,
                w_ax_ref, b_ax_ref, lam_ref, w_out_ref, ln1g_ref, ln1b_ref,
                w_rt_ref, b_rt_ref, wg_ref, wu_ref, wd_ref, ln2g_ref, ln2b_ref,
                y_ref, npool_ref, nconv_ref, nh_ref,
                slab, xt, x1, u2, ycat, yacc, comb, zpool, zconv, a_s, b_s, gl_s, h_s):
    tm = nb * tb
    ti = pl.program_id(1)
    g = pl.program_id(2)

    def mod_part(k):
        return mod_ref[:, pl.ds(k * D, D)]

    @pl.when(g == 0)
    def _mix_and_route():
        for b in range(nb):
            for j in range(N_SLABS):
                slab[j, pl.ds(b * pitch, tb), :] = x_ref[b, :, pl.ds(j * LANES, LANES)]
        for t in range(tb):
            for j in range(N_SLABS):
                xt[t, :, pl.ds(j * LANES, LANES)] = slab[j, pl.ds(t, nb, stride=pitch), :]

        @pl.when(ti == 0)
        def _init_state():
            zpool[pl.ds(0, POOL_BUF * nb), :] = pool0_ref[...].reshape(POOL_BUF * nb, D_POOL)
            zconv[pl.ds(0, (CONV_W - 1) * nb), :] = conv0_ref[...].reshape((CONV_W - 1) * nb, D_LRU)
            h_s[...] = h0_ref[...]

        sh1 = mod_part(0)
        sc1 = mod_part(1)
        u = (xt[...] * (1.0 + sc1)[None] + sh1[None]).reshape(tm, D)
        proj = _dot(u.astype(bf16), w_in_ref[...])
        zpool[pl.ds(POOL_BUF * nb, tm), :] = proj[:, :D_POOL]
        zconv[pl.ds((CONV_W - 1) * nb, tm), :] = proj[:, D_POOL:D_POOL + D_LRU]
        gl_s[...] = _gelu_tanh(proj[:, D_POOL + D_LRU:])

        row = lax.broadcasted_iota(jnp.int32, (tm, LANES), 0)
        t_loc = lax.shift_right_logical(row, int(math.log2(nb)))
        t_glob = (ti * tb + t_loc + (1 + n_prev)).astype(f32)
        for c, w in enumerate(POOL_WINDOWS):
            lanes = pl.ds(c * LANES, LANES)
            s = zpool[pl.ds((POOL_BUF + 1 - w) * nb, tm + (w - 1) * nb), lanes]
            step = 1
            while step < w:
                s = s[step * nb:] + s[:-step * nb]
                step *= 2
            cnt = jnp.minimum(f32(w), t_glob)
            dlt = s / cnt - zpool[pl.ds(POOL_BUF * nb, tm), lanes]
            yp = _dot(dlt.astype(bf16), w_pool_ref[c]) * pscale_ref[:, lanes]
            ycat[:, lanes] = yp.astype(bf16)

        xc = b_conv_ref[...] + zconv[pl.ds(0, tm), :] * w_conv_ref[0:1, :]
        for k in range(1, CONV_W):
            xc = xc + zconv[pl.ds(k * nb, tm), :] * w_conv_ref[k:k + 1, :]
        gates = _dot(xc.astype(bf16), w_ax_ref[...]) + b_ax_ref[...]
        r = _sigmoid(gates[:, :D_LRU])
        ig = _sigmoid(gates[:, D_LRU:])
        nl = -lam_ref[...]
        softplus = jnp.maximum(nl, 0.0) + jnp.log(1.0 + jnp.exp(-jnp.abs(nl)))
        log_a = (-LRU_C) * r * softplus
        a = jnp.exp(log_a)
        mult = jnp.sqrt(jnp.maximum(1.0 - jnp.exp(2.0 * log_a), 0.0))
        a_s[...] = a
        b_s[...] = mult * ig * xc

        def scan_step(t, h):
            rows = pl.ds(pl.multiple_of(t * nb, nb), nb)
            h = a_s[rows, :] * h + b_s[rows, :]
            b_s[rows, :] = h
            return h

        h_last = lax.fori_loop(0, tb, scan_step, h_s[...], unroll=min(tb, 8))
        h_s[...] = h_last
        ycat[:, pl.ds(D_POOL, D_LRU)] = (b_s[...] * gl_s[...]).astype(bf16)

        new_pool = zpool[pl.ds(tm, POOL_BUF * nb), :]
        new_conv = zconv[pl.ds(tm, (CONV_W - 1) * nb), :]
        zpool[pl.ds(0, POOL_BUF * nb), :] = new_pool
        zconv[pl.ds(0, (CONV_W - 1) * nb), :] = new_conv

        @pl.when(ti == pl.num_programs(1) - 1)
        def _emit_state():
            npool_ref[...] = new_pool.reshape(POOL_BUF, nb, D_POOL)
            nconv_ref[...] = new_conv.reshape(CONV_W - 1, nb, D_LRU)
            nh_ref[...] = h_last

        mix = _dot(ycat[...], w_out_ref[...]).reshape(tb, nb, D)
        g1 = mod_part(2)
        v = ALPHA * xt[...] + (1.0 + g1)[None] * mix
        xn = _layer_norm(v, ln1g_ref[...][None], ln1b_ref[...][None])
        x1[...] = xn
        sh2 = mod_part(3)
        sc2 = mod_part(4)
        u2v = (xn * (1.0 + sc2)[None] + sh2[None]).reshape(tm, D).astype(bf16)
        u2[...] = u2v

        lt = lax.dot_general(w_rt_ref[...], u2v, (((1,), (1,)), ((), ())),
                             preferred_element_type=f32) + b_rt_ref[...]
        gl = [lt[k:k + 1, :] for k in range(N_GROUPS)]
        best = gl[0]
        gsel = jnp.zeros_like(best, dtype=jnp.int32)
        for k in range(1, N_GROUPS):
            better = gl[k] > best
            best = jnp.where(better, gl[k], best)
            gsel = jnp.where(better, k, gsel)
        denom = jnp.exp(gl[0] - best)
        for k in range(1, N_GROUPS):
            denom = denom + jnp.exp(gl[k] - best)
        p_sel = 1.0 / denom
        es = []
        for j in range(EPG):
            v_j = lt[SUBLANES + j:SUBLANES + j + 1, :]
            for k in range(1, N_GROUPS):
                r0 = SUBLANES * (k + 1) + j
                v_j = jnp.where(gsel == k, lt[r0:r0 + 1, :], v_j)
            es.append(v_j)
        v1 = es[0]
        i1 = jnp.zeros_like(gsel)
        for j in range(1, EPG):
            better = es[j] > v1
            v1 = jnp.where(better, es[j], v1)
            i1 = jnp.where(better, j, i1)
        neg = f32(-jnp.inf)
        v2 = jnp.full_like(v1, neg)
        i2 = jnp.full_like(gsel, -1)
        for j in range(EPG):
            cand = jnp.logical_and(i1 != j, jnp.logical_or(i2 < 0, es[j] > v2))
            v2 = jnp.where(cand, es[j], v2)
            i2 = jnp.where(cand, j, i2)
        e21 = jnp.exp(v2 - v1)
        w1 = p_sel / (1.0 + e21)
        w2 = p_sel * e21 / (1.0 + e21)
        rid = lax.broadcasted_iota(jnp.int32, (LANES, tm), 0)
        for k in range(N_GROUPS):
            in_grp = gsel == k
            ct = jnp.zeros((LANES, tm), f32)
            for j in range(EPG):
                wj = jnp.where(jnp.logical_and(in_grp, i1 == j), w1,
                               jnp.where(jnp.logical_and(in_grp, i2 == j), w2, 0.0))
                ct = jnp.where(rid == j, wj, ct)
            comb[k] = ct.T

    u2v = u2[...]
    cw = comb[g]
    contrib = None
    for e in range(EPG):
        hg = _dot(u2v, wg_ref[e])
        hu = _dot(u2v, wu_ref[e])
        act = hg * _sigmoid(hg) * hu * cw[:, e:e + 1]
        part = _dot(act.astype(bf16), wd_ref[e])
        contrib = part if contrib is None else contrib + part

    @pl.when(g == 0)
    def _first():
        yacc[...] = contrib

    @pl.when(g > 0)
    def _rest():
        yacc[...] += contrib

    @pl.when(g == N_GROUPS - 1)
    def _finish():
        g2 = mod_part(5)
        v = ALPHA * x1[...] + (1.0 + g2)[None] * yacc[...].reshape(tb, nb, D)
        xt[...] = _layer_norm(v, ln2g_ref[...][None], ln2b_ref[...][None])
        for t in range(tb):
            for j in range(N_SLABS):
                slab[j, pl.ds(t, nb, stride=pitch), :] = xt[t, :, pl.ds(j * LANES, LANES)]
        for b in range(nb):
            for j in range(N_SLABS):
                y_ref[b, :, pl.ds(j * LANES, LANES)] = slab[j, pl.ds(b * pitch, tb), :]


def _run_layer(x, mod, mod_blk0, pool0, conv0, h0, n_prev, nb, tb, wts):
    bsz, t_len, _ = x.shape
    n_sb = bsz // nb
    n_tt = t_len // tb
    tm = nb * tb
    pitch = tb + SUBLANES if (tb // SUBLANES) % 2 == 0 else tb
    const2 = lambda s, t, g: (0, 0)
    const3 = lambda s, t, g: (0, 0, 0)
    grp3 = lambda s, t, g: (g, 0, 0)
    in_specs = [
        pl.BlockSpec((nb, tb, D), lambda s, t, g: (s, t, 0)),
        pl.BlockSpec((nb, 6 * D), lambda s, t, g: (mod_blk0 + s, 0)),
        pl.BlockSpec((POOL_BUF, nb, D_POOL), lambda s, t, g: (0, s, 0)),
        pl.BlockSpec((CONV_W - 1, nb, D_LRU), lambda s, t, g: (0, s, 0)),
        pl.BlockSpec((nb, D_LRU), lambda s, t, g: (s, 0)),
        pl.BlockSpec((D, D_IN), const2, pipeline_mode=pl.Buffered(1)),
        pl.BlockSpec((4, POOL_GROUP, POOL_GROUP), const3),
        pl.BlockSpec((1, D_POOL), const2),
        pl.BlockSpec((CONV_W, D_LRU), const2),
        pl.BlockSpec((1, D_LRU), const2),
        pl.BlockSpec((D_LRU, 2 * D_LRU), const2, pipeline_mode=pl.Buffered(1)),
        pl.BlockSpec((1, 2 * D_LRU), const2),
        pl.BlockSpec((1, D_LRU), const2),
        pl.BlockSpec((D, D), const2, pipeline_mode=pl.Buffered(1)),
        pl.BlockSpec((1, D), const2),
        pl.BlockSpec((1, D), const2),
        pl.BlockSpec((ROUTE_ROWS, D), const2),
        pl.BlockSpec((ROUTE_ROWS, 1), const2),
        pl.BlockSpec((EPG, D, D_EXPERT), grp3),
        pl.BlockSpec((EPG, D, D_EXPERT), grp3),
        pl.BlockSpec((EPG, D_EXPERT, D), grp3),
        pl.BlockSpec((1, D), const2),
        pl.BlockSpec((1, D), const2),
    ]
    out_specs = [
        pl.BlockSpec((nb, tb, D), lambda s, t, g: (s, t, 0)),
        pl.BlockSpec((POOL_BUF, nb, D_POOL), lambda s, t, g: (0, s, 0)),
        pl.BlockSpec((CONV_W - 1, nb, D_LRU), lambda s, t, g: (0, s, 0)),
        pl.BlockSpec((nb, D_LRU), lambda s, t, g: (s, 0)),
    ]
    out_shape = [
        jax.ShapeDtypeStruct((bsz, t_len, D), f32),
        jax.ShapeDtypeStruct((POOL_BUF, bsz, D_POOL), f32),
        jax.ShapeDtypeStruct((CONV_W - 1, bsz, D_LRU), f32),
        jax.ShapeDtypeStruct((bsz, D_LRU), f32),
    ]
    scratch = [
        pltpu.VMEM((N_SLABS, nb * pitch, LANES), f32),
        pltpu.VMEM((tb, nb, D), f32),
        pltpu.VMEM((tb, nb, D), f32),
        pltpu.VMEM((tm, D), bf16),
        pltpu.VMEM((tm, D), bf16),
        pltpu.VMEM((tm, D), f32),
        pltpu.VMEM((N_GROUPS, tm, LANES), f32),
        pltpu.VMEM(((tb + POOL_BUF) * nb, D_POOL), f32),
        pltpu.VMEM(((tb + CONV_W - 1) * nb, D_LRU), f32),
        pltpu.VMEM((tm, D_LRU), f32),
        pltpu.VMEM((tm, D_LRU), f32),
        pltpu.VMEM((tm, D_LRU), f32),
        pltpu.VMEM((nb, D_LRU), f32),
    ]
    body = functools.partial(_layer_body, nb, tb, pitch, n_prev)
    return pl.pallas_call(
        body,
        grid=(n_sb, n_tt, N_GROUPS),
        in_specs=in_specs,
        out_specs=out_specs,
        out_shape=out_shape,
        scratch_shapes=scratch,
        compiler_params=pltpu.CompilerParams(
            dimension_semantics=("arbitrary", "arbitrary", "arbitrary"),
            vmem_limit_bytes=VMEM_LIMIT),
        name=f"layer_nb{nb}_tb{tb}",
    )(x, mod, pool0, conv0, h0, *wts)


def _block_diag(w):
    n, k, _ = w.shape
    eye = jnp.eye(n, dtype=w.dtype)
    return jnp.einsum('hij,hg->higj', w, eye).reshape(n * k, n * k)


def kernel(x_prompt, x_sample, c_prompt, c_sample, state_pool, state_conv, state_lru, w_ada, b_ada, w_in, w_pool, pool_scale, w_conv, b_conv, w_a, b_a, w_x, b_x, lru_lambda, w_out, ln1_g, ln1_b, w_group, b_group, w_route, b_route, w_gate, w_up, w_down, ln2_g, ln2_b):
    l = 0
    bp = x_prompt.shape[0]
    bs = x_sample.shape[0]
    mod = _ada_mod(jnp.concatenate([c_sample, c_prompt], axis=0), w_ada[l], b_ada[l])

    w_rt = jnp.zeros((ROUTE_ROWS, D), f32).at[0:N_GROUPS].set(w_group[l].T)
    b_rt = jnp.zeros((ROUTE_ROWS,), f32).at[0:N_GROUPS].set(b_group[l])
    for k in range(N_GROUPS):
        r0 = SUBLANES * (k + 1)
        w_rt = w_rt.at[r0:r0 + EPG].set(w_route[l][:, k * EPG:(k + 1) * EPG].T)
        b_rt = b_rt.at[r0:r0 + EPG].set(b_route[l][k * EPG:(k + 1) * EPG])

    wts = (
        w_in[l].astype(bf16),
        w_pool[l].astype(bf16),
        pool_scale[l].reshape(1, D_POOL),
        w_conv[l],
        b_conv[l].reshape(1, D_LRU),
        jnp.concatenate([_block_diag(w_a[l]), _block_diag(w_x[l])], axis=1).astype(bf16),
        jnp.concatenate([b_a[l], b_x[l]]).reshape(1, 2 * D_LRU),
        lru_lambda[l].reshape(1, D_LRU),
        w_out[l].astype(bf16),
        ln1_g[l].reshape(1, D),
        ln1_b[l].reshape(1, D),
        w_rt.astype(bf16),
        b_rt.reshape(ROUTE_ROWS, 1),
        w_gate[l].astype(bf16),
        w_up[l].astype(bf16),
        w_down[l].astype(bf16),
        ln2_g[l].reshape(1, D),
        ln2_b[l].reshape(1, D),
    )

    zp = jnp.zeros((POOL_BUF, bp, D_POOL), f32)
    zc = jnp.zeros((CONV_W - 1, bp, D_LRU), f32)
    zh = jnp.zeros((bp, D_LRU), f32)
    yp, pool_p, conv_p, lru_p = _run_layer(x_prompt, mod, bs // bp, zp, zc, zh, 0, bp, 64, wts)

    n_prev_s = min(PAST_LEN, POOL_BUF)
    ys, pool_s, conv_s, lru_s = _run_layer(
        x_sample, mod, 0, state_pool[l].transpose(1, 0, 2), state_conv[l].transpose(1, 0, 2),
        state_lru[l], n_prev_s, 64, x_sample.shape[1], wts)

    tr = lambda a: a.transpose(1, 0, 2)[None]
    return (yp, ys, tr(pool_p), tr(conv_p), lru_p[None], tr(pool_s), tr(conv_s), lru_s[None])
```

```python
import functools
import math

import jax
import jax.numpy as jnp
from jax import lax
from jax.experimental import pallas as pl
from jax.experimental.pallas import tpu as pltpu

D = 1024
D_POOL = 512
D_LRU = 512
D_IN = D_POOL + 2 * D_LRU
POOL_WINDOWS = (2, 4, 8, 16)
POOL_GROUP = 128
POOL_BUF = 15
CONV_W = 4
LRU_C = 8.0
N_GROUPS = 4
EPG = 4
D_EXPERT = 256
DEPTH = 1
ALPHA = (2.0 * DEPTH) ** 0.25
LN_EPS = 1e-5
PAST_LEN = 16384

LANES = 128
SUBLANES = 8
N_SLABS = D // LANES
ROUTE_ROWS = 48
CHUNK = 256
LANE_GSEL = 4
LANE_RANK = 5
VMEM_LIMIT = 60 * 1024 * 1024

f32 = jnp.float32
bf16 = jnp.bfloat16
i32 = jnp.int32


def _dot(a, b):
    return jnp.dot(a, b, preferred_element_type=f32)


def _sigmoid(x):
    return 1.0 / (1.0 + jnp.exp(-x))


def _gelu_tanh(x):
    c = math.sqrt(2.0 / math.pi)
    return 0.5 * x * (1.0 + jnp.tanh(c * (x + 0.044715 * (x * x * x))))


def _layer_norm(v, g, b):
    mu = jnp.mean(v, axis=-1, keepdims=True)
    c = v - mu
    var = jnp.mean(c * c, axis=-1, keepdims=True)
    return c * lax.rsqrt(var + LN_EPS) * g + b


def _ada_body(c_ref, w_ref, b_ref, o_ref):
    o_ref[...] = _dot(c_ref[...].astype(bf16), w_ref[...].astype(bf16)) + b_ref[...]


def _ada_mod(c_all, w_ada, b_ada):
    n = c_all.shape[0]
    bn = 512
    return pl.pallas_call(
        _ada_body,
        grid=(6 * D // bn,),
        in_specs=[
            pl.BlockSpec((n, D), lambda j: (0, 0)),
            pl.BlockSpec((D, bn), lambda j: (0, j)),
            pl.BlockSpec((1, bn), lambda j: (0, j)),
        ],
        out_specs=pl.BlockSpec((n, bn), lambda j: (0, j)),
        out_shape=jax.ShapeDtypeStruct((n, 6 * D), f32),
        name="ada_mod",
    )(c_all, w_ada, b_ada.reshape(1, 6 * D))


def _layer_body(nb, tb, pitch, n_prev, n_tt, n_tiles, pipelined,
                x_ref, mod_ref, pool0_ref, conv0_ref, h0_ref,
                w_in_ref, w_pool_ref, pscale_ref, w_conv_ref, b_conv_ref,
                w_ax_ref, b_ax_ref, lam_ref, w_out_ref, ln1g_ref, ln1b_ref,
                w_rt_ref, b_rt_ref, tri_ref, wg_ref, wu_ref, wd_ref, ln2g_ref, ln2b_ref,
                y_ref, npool_ref, nconv_ref, nh_ref,
                slab, xt, x1, u2, tok_t, tok3, rows_s, cnt_s,
                ycat, yacc, zpool, zconv, a_s, b_s, gl_s, h_s):
    tm = nb * tb
    q = pl.program_id(0)
    g = pl.program_id(1)
    mix_tile = jnp.minimum(q, n_tiles - 1)
    ti = mix_tile % n_tt
    if pipelined:
        slot_mix = q % 2
        slot_moe = (q + 1) % 2
    else:
        slot_mix = 0
        slot_moe = 0

    def mod_part(k):
        return mod_ref[:, pl.ds(k * D, D)]

    def stage0():
        for b in range(nb):
            for j in range(N_SLABS):
                slab[j, pl.ds(b * pitch, tb), :] = x_ref[b, :, pl.ds(j * LANES, LANES)]
        for t in range(tb):
            for j in range(N_SLABS):
                xt[t, :, pl.ds(j * LANES, LANES)] = slab[j, pl.ds(t, nb, stride=pitch), :]
        sh1 = mod_part(0)
        sc1 = mod_part(1)
        u = (xt[...] * (1.0 + sc1)[None] + sh1[None]).reshape(tm, D)
        proj = _dot(u.astype(bf16), w_in_ref[...])
        zpool[pl.ds(POOL_BUF * nb, tm), :] = proj[:, :D_POOL]
        zconv[pl.ds((CONV_W - 1) * nb, tm), :] = proj[:, D_POOL:D_POOL + D_LRU]
        gl_s[...] = _gelu_tanh(proj[:, D_POOL + D_LRU:])

    def stage1():
        row = lax.broadcasted_iota(i32, (tm, LANES), 0)
        t_loc = lax.shift_right_logical(row, int(math.log2(nb)))
        t_glob = (ti * tb + t_loc + (1 + n_prev)).astype(f32)
        for c, w in enumerate(POOL_WINDOWS):
            lanes = pl.ds(c * LANES, LANES)
            s = zpool[pl.ds((POOL_BUF + 1 - w) * nb, tm + (w - 1) * nb), lanes]
            step = 1
            while step < w:
                s = s[step * nb:] + s[:-step * nb]
                step *= 2
            cnt = jnp.minimum(f32(w), t_glob)
            dlt = s / cnt - zpool[pl.ds(POOL_BUF * nb, tm), lanes]
            yp = _dot(dlt.astype(bf16), w_pool_ref[c]) * pscale_ref[:, lanes]
            ycat[:, lanes] = yp.astype(bf16)
        xc = b_conv_ref[...] + zconv[pl.ds(0, tm), :] * w_conv_ref[0:1, :]
        for k in range(1, CONV_W):
            xc = xc + zconv[pl.ds(k * nb, tm), :] * w_conv_ref[k:k + 1, :]
        gates = _dot(xc.astype(bf16), w_ax_ref[...]) + b_ax_ref[...]
        r = _sigmoid(gates[:, :D_LRU])
        ig = _sigmoid(gates[:, D_LRU:])
        nl = -lam_ref[...]
        softplus = jnp.maximum(nl, 0.0) + jnp.log(1.0 + jnp.exp(-jnp.abs(nl)))
        log_a = (-LRU_C) * r * softplus
        a_s[...] = jnp.exp(log_a)
        mult = jnp.sqrt(jnp.maximum(1.0 - jnp.exp(2.0 * log_a), 0.0))
        b_s[...] = mult * ig * xc
        zpool[pl.ds(0, POOL_BUF * nb), :] = zpool[pl.ds(tm, POOL_BUF * nb), :]
        zconv[pl.ds(0, (CONV_W - 1) * nb), :] = zconv[pl.ds(tm, (CONV_W - 1) * nb), :]

    def stage2():
        def scan_step(t, h):
            rows = pl.ds(pl.multiple_of(t * nb, nb), nb)
            h = a_s[rows, :] * h + b_s[rows, :]
            b_s[rows, :] = h
            return h

        h_s[...] = lax.fori_loop(0, tb, scan_step, h_s[...], unroll=min(tb, 8))
        ycat[:, pl.ds(D_POOL, D_LRU)] = (b_s[...] * gl_s[...]).astype(bf16)
        mix = _dot(ycat[...], w_out_ref[...]).reshape(tb, nb, D)
        g1 = mod_part(2)
        v = ALPHA * xt[...] + (1.0 + g1)[None] * mix
        x1[slot_mix] = _layer_norm(v, ln1g_ref[...][None], ln1b_ref[...][None])

    def stage3():
        sh2 = mod_part(3)
        sc2 = mod_part(4)
        u2v = (x1[slot_mix] * (1.0 + sc2)[None] + sh2[None]).reshape(tm, D).astype(bf16)
        u2[slot_mix] = u2v
        lt = lax.dot_general(w_rt_ref[...], u2v, (((1,), (1,)), ((), ())),
                             preferred_element_type=f32) + b_rt_ref[...]
        gl = [lt[k:k + 1, :] for k in range(N_GROUPS)]
        best = gl[0]
        gsel = jnp.zeros_like(best, dtype=i32)
        for k in range(1, N_GROUPS):
            better = gl[k] > best
            best = jnp.where(better, gl[k], best)
            gsel = jnp.where(better, k, gsel)
        denom = jnp.exp(gl[0] - best)
        for k in range(1, N_GROUPS):
            denom = denom + jnp.exp(gl[k] - best)
        p_sel = 1.0 / denom
        es = []
        for j in range(EPG):
            v_j = lt[SUBLANES + j:SUBLANES + j + 1, :]
            for k in range(1, N_GROUPS):
                r0 = SUBLANES * (k + 1) + j
                v_j = jnp.where(gsel == k, lt[r0:r0 + 1, :], v_j)
            es.append(v_j)
        v1 = es[0]
        i1 = jnp.zeros_like(gsel)
        for j in range(1, EPG):
            better = es[j] > v1
            v1 = jnp.where(better, es[j], v1)
            i1 = jnp.where(better, j, i1)
        v2 = jnp.full_like(v1, -jnp.inf)
        i2 = jnp.full_like(gsel, -1)
        for j in range(EPG):
            cand = jnp.logical_and(i1 != j, jnp.logical_or(i2 < 0, es[j] > v2))
            v2 = jnp.where(cand, es[j], v2)
            i2 = jnp.where(cand, j, i2)
        e21 = jnp.exp(v2 - v1)
        w1 = p_sel / (1.0 + e21)
        w2 = p_sel * e21 / (1.0 + e21)
        rid8 = lax.broadcasted_iota(i32, (SUBLANES, tm), 0)
        onehot = jnp.where(rid8 == gsel, 1.0, 0.0)
        pref = _dot(onehot.astype(bf16), tri_ref[...])
        rank = jnp.sum(onehot * pref, axis=0, keepdims=True)
        for k in range(N_GROUPS):
            cnt_s[slot_mix * N_GROUPS + k] = jnp.sum(jnp.where(gsel == k, 1.0, 0.0)).astype(i32)
        rows_s[slot_mix, 0:1, :] = gsel
        rows_s[slot_mix, 1:2, :] = rank.astype(i32)
        rid = lax.broadcasted_iota(i32, (LANES, tm), 0)
        rec = jnp.zeros((LANES, tm), f32)
        for j in range(EPG):
            wj = jnp.where(i1 == j, w1, jnp.where(i2 == j, w2, 0.0))
            rec = jnp.where(rid == j, wj, rec)
        rec = jnp.where(rid == LANE_GSEL, gsel.astype(f32), rec)
        rec = jnp.where(rid == LANE_RANK, rank, rec)
        rt = rec.T
        tok_t[slot_mix] = rt
        hi = rt.astype(bf16)
        r1 = rt - hi.astype(f32)
        mid = r1.astype(bf16)
        lo = (r1 - mid.astype(f32)).astype(bf16)
        tok3[slot_mix, :, pl.ds(0, LANES)] = hi
        tok3[slot_mix, :, pl.ds(LANES, LANES)] = mid
        tok3[slot_mix, :, pl.ds(2 * LANES, LANES)] = lo

    def moe_chunk(k, base):
        gsel_row = rows_s[slot_moe, 0:1, :]
        rank_row = rows_s[slot_moe, 1:2, :]
        jj = lax.broadcasted_iota(i32, (CHUNK, tm), 0) + base
        pm = jnp.where(jnp.logical_and(gsel_row == k, rank_row == jj), 1.0, 0.0).astype(bf16)
        xs = _dot(pm, u2[slot_moe]).astype(bf16)
        cs3 = _dot(pm, tok3[slot_moe])
        cs = cs3[:, :LANES] + cs3[:, LANES:2 * LANES] + cs3[:, 2 * LANES:]
        ys = None
        for e in range(EPG):
            hg = _dot(xs, wg_ref[e])
            hu = _dot(xs, wu_ref[e])
            act = hg * _sigmoid(hg) * hu * cs[:, e:e + 1]
            part = _dot(act.astype(bf16), wd_ref[e])
            ys = part if ys is None else ys + part
        rec = tok_t[slot_moe]
        gsel_col = rec[:, LANE_GSEL:LANE_GSEL + 1]
        rank_col = rec[:, LANE_RANK:LANE_RANK + 1]
        jl = (lax.broadcasted_iota(i32, (tm, CHUNK), 1) + base).astype(f32)
        pt = jnp.where(jnp.logical_and(gsel_col == f32(k), rank_col == jl), 1.0, 0.0).astype(bf16)
        return _dot(pt, ys.astype(bf16))

    def moe_group(k):
        contrib = moe_chunk(k, 0)
        if k == 0:
            yacc[...] = contrib
        else:
            yacc[...] += contrib

    def moe_overflow(k):
        n_k = cnt_s[slot_moe * N_GROUPS + k]
        n_chunks = lax.shift_right_logical(n_k + (CHUNK - 1), int(math.log2(CHUNK)))

        def body(c, carry):
            yacc[...] += moe_chunk(k, c * CHUNK)
            return carry

        lax.fori_loop(1, n_chunks, body, 0)

    def finish():
        g2 = mod_part(5)
        v = ALPHA * x1[slot_moe] + (1.0 + g2)[None] * yacc[...].reshape(tb, nb, D)
        xt[...] = _layer_norm(v, ln2g_ref[...][None], ln2b_ref[...][None])
        for t in range(tb):
            for j in range(N_SLABS):
                slab[j, pl.ds(t, nb, stride=pitch), :] = xt[t, :, pl.ds(j * LANES, LANES)]
        for b in range(nb):
            for j in range(N_SLABS):
                y_ref[b, :, pl.ds(j * LANES, LANES)] = slab[j, pl.ds(b * pitch, tb), :]

    @pl.when(jnp.logical_and(g == 0, ti == 0))
    def _init_state():
        zpool[pl.ds(0, POOL_BUF * nb), :] = pool0_ref[...].reshape(POOL_BUF * nb, D_POOL)
        zconv[pl.ds(0, (CONV_W - 1) * nb), :] = conv0_ref[...].reshape((CONV_W - 1) * nb, D_LRU)
        h_s[...] = h0_ref[...]

    if pipelined:
        @pl.when(jnp.logical_and(g == 0, q == 0))
        def _init_moe_side():
            x1[1] = jnp.zeros((tb, nb, D), f32)
            u2[1] = jnp.zeros((tm, D), bf16)
            tok_t[1] = jnp.zeros((tm, LANES), f32)
            tok3[1] = jnp.zeros((tm, 3 * LANES), bf16)
            rows_s[1] = jnp.zeros((SUBLANES, tm), i32)
            for k in range(N_GROUPS):
                cnt_s[N_GROUPS + k] = 0

    stages = (stage0, stage1, stage2, stage3)
    for k in range(N_GROUPS):
        @pl.when(g == k)
        def _step(k=k):
            if pipelined:
                stages[k]()
            elif k == 0:
                for st in stages:
                    st()
            moe_group(k)

        emit_here = (k == 2) if pipelined else (k == 0)
        if emit_here:
            @pl.when(jnp.logical_and(g == k, jnp.logical_and(q < n_tiles, ti == n_tt - 1)))
            def _emit_state():
                npool_ref[...] = zpool[pl.ds(0, POOL_BUF * nb), :].reshape(POOL_BUF, nb, D_POOL)
                nconv_ref[...] = zconv[pl.ds(0, (CONV_W - 1) * nb), :].reshape(CONV_W - 1, nb, D_LRU)
                nh_ref[...] = h_s[...]

        @pl.when(g == k)
        def _overflow(k=k):
            moe_overflow(k)

    @pl.when(g == N_GROUPS - 1)
    def _finish():
        finish()


def _run_layer(x, mod, mod_blk0, pool0, conv0, h0, n_prev, nb, tb, pipelined, wts):
    bsz, t_len, _ = x.shape
    n_sb = bsz // nb
    n_tt = t_len // tb
    n_tiles = n_sb * n_tt
    tm = nb * tb
    assert not pipelined or n_sb == 1
    n_outer = n_tiles + (1 if pipelined else 0)
    n_slots = 2 if pipelined else 1
    pitch = tb + SUBLANES if (tb // SUBLANES) % 2 == 0 else tb

    def mix_st(q):
        tile = jnp.minimum(q, n_tiles - 1)
        return tile // n_tt, tile % n_tt

    def out_st(q):
        tile = jnp.maximum(q - 1, 0) if pipelined else q
        return tile // n_tt, tile % n_tt

    const2 = lambda q, g: (0, 0)
    const3 = lambda q, g: (0, 0, 0)
    grp3 = lambda q, g: (g, 0, 0)
    single = pl.Buffered(1)
    in_specs = [
        pl.BlockSpec((nb, tb, D), lambda q, g: (*mix_st(q), 0)),
        pl.BlockSpec((nb, 6 * D), lambda q, g: (mod_blk0 + mix_st(q)[0], 0)),
        pl.BlockSpec((POOL_BUF, nb, D_POOL), lambda q, g: (0, mix_st(q)[0], 0)),
        pl.BlockSpec((CONV_W - 1, nb, D_LRU), lambda q, g: (0, mix_st(q)[0], 0)),
        pl.BlockSpec((nb, D_LRU), lambda q, g: (mix_st(q)[0], 0)),
        pl.BlockSpec((D, D_IN), const2, pipeline_mode=single),
        pl.BlockSpec((4, POOL_GROUP, POOL_GROUP), const3),
        pl.BlockSpec((1, D_POOL), const2),
        pl.BlockSpec((CONV_W, D_LRU), const2),
        pl.BlockSpec((1, D_LRU), const2),
        pl.BlockSpec((D_LRU, 2 * D_LRU), const2, pipeline_mode=single),
        pl.BlockSpec((1, 2 * D_LRU), const2),
        pl.BlockSpec((1, D_LRU), const2),
        pl.BlockSpec((D, D), const2, pipeline_mode=single),
        pl.BlockSpec((1, D), const2),
        pl.BlockSpec((1, D), const2),
        pl.BlockSpec((ROUTE_ROWS, D), const2),
        pl.BlockSpec((ROUTE_ROWS, 1), const2),
        pl.BlockSpec((tm, tm), const2, pipeline_mode=single),
        pl.BlockSpec((EPG, D, D_EXPERT), grp3),
        pl.BlockSpec((EPG, D, D_EXPERT), grp3),
        pl.BlockSpec((EPG, D_EXPERT, D), grp3),
        pl.BlockSpec((1, D), const2),
        pl.BlockSpec((1, D), const2),
    ]
    out_specs = [
        pl.BlockSpec((nb, tb, D), lambda q, g: (*out_st(q), 0)),
        pl.BlockSpec((POOL_BUF, nb, D_POOL), lambda q, g: (0, mix_st(q)[0], 0)),
        pl.BlockSpec((CONV_W - 1, nb, D_LRU), lambda q, g: (0, mix_st(q)[0], 0)),
        pl.BlockSpec((nb, D_LRU), lambda q, g: (mix_st(q)[0], 0)),
    ]
    out_shape = [
        jax.ShapeDtypeStruct((bsz, t_len, D), f32),
        jax.ShapeDtypeStruct((POOL_BUF, bsz, D_POOL), f32),
        jax.ShapeDtypeStruct((CONV_W - 1, bsz, D_LRU), f32),
        jax.ShapeDtypeStruct((bsz, D_LRU), f32),
    ]
    scratch = [
        pltpu.VMEM((N_SLABS, nb * pitch, LANES), f32),
        pltpu.VMEM((tb, nb, D), f32),
        pltpu.VMEM((n_slots, tb, nb, D), f32),
        pltpu.VMEM((n_slots, tm, D), bf16),
        pltpu.VMEM((n_slots, tm, LANES), f32),
        pltpu.VMEM((n_slots, tm, 3 * LANES), bf16),
        pltpu.VMEM((n_slots, SUBLANES, tm), i32),
        pltpu.SMEM((n_slots * N_GROUPS,), i32),
        pltpu.VMEM((tm, D), bf16),
        pltpu.VMEM((tm, D), f32),
        pltpu.VMEM(((tb + POOL_BUF) * nb, D_POOL), f32),
        pltpu.VMEM(((tb + CONV_W - 1) * nb, D_LRU), f32),
        pltpu.VMEM((tm, D_LRU), f32),
        pltpu.VMEM((tm, D_LRU), f32),
        pltpu.VMEM((tm, D_LRU), f32),
        pltpu.VMEM((nb, D_LRU), f32),
    ]
    body = functools.partial(_layer_body, nb, tb, pitch, n_prev, n_tt, n_tiles, pipelined)
    return pl.pallas_call(
        body,
        grid=(n_outer, N_GROUPS),
        in_specs=in_specs,
        out_specs=out_specs,
        out_shape=out_shape,
        scratch_shapes=scratch,
        compiler_params=pltpu.CompilerParams(
            dimension_semantics=("arbitrary", "arbitrary"),
            vmem_limit_bytes=VMEM_LIMIT),
        name=f"layer_nb{nb}_tb{tb}",
    )(x, mod, pool0, conv0, h0, *wts)


def _block_diag(w):
    n, k, _ = w.shape
    eye = jnp.eye(n, dtype=w.dtype)
    return jnp.einsum('hij,hg->higj', w, eye).reshape(n * k, n * k)


def kernel(x_prompt, x_sample, c_prompt, c_sample, state_pool, state_conv, state_lru, w_ada, b_ada, w_in, w_pool, pool_scale, w_conv, b_conv, w_a, b_a, w_x, b_x, lru_lambda, w_out, ln1_g, ln1_b, w_group, b_group, w_route, b_route, w_gate, w_up, w_down, ln2_g, ln2_b):
    l = 0
    bp = x_prompt.shape[0]
    bs = x_sample.shape[0]
    tile_rows = 512
    mod = _ada_mod(jnp.concatenate([c_sample, c_prompt], axis=0), w_ada[l], b_ada[l])

    w_rt = jnp.zeros((ROUTE_ROWS, D), f32).at[0:N_GROUPS].set(w_group[l].T)
    b_rt = jnp.zeros((ROUTE_ROWS,), f32).at[0:N_GROUPS].set(b_group[l])
    for k in range(N_GROUPS):
        r0 = SUBLANES * (k + 1)
        w_rt = w_rt.at[r0:r0 + EPG].set(w_route[l][:, k * EPG:(k + 1) * EPG].T)
        b_rt = b_rt.at[r0:r0 + EPG].set(b_route[l][k * EPG:(k + 1) * EPG])
    tri = jnp.triu(jnp.ones((tile_rows, tile_rows), bf16), 1)

    wts = (
        w_in[l].astype(bf16),
        w_pool[l].astype(bf16),
        pool_scale[l].reshape(1, D_POOL),
        w_conv[l],
        b_conv[l].reshape(1, D_LRU),
        jnp.concatenate([_block_diag(w_a[l]), _block_diag(w_x[l])], axis=1).astype(bf16),
        jnp.concatenate([b_a[l], b_x[l]]).reshape(1, 2 * D_LRU),
        lru_lambda[l].reshape(1, D_LRU),
        w_out[l].astype(bf16),
        ln1_g[l].reshape(1, D),
        ln1_b[l].reshape(1, D),
        w_rt.astype(bf16),
        b_rt.reshape(ROUTE_ROWS, 1),
        tri,
        w_gate[l].astype(bf16),
        w_up[l].astype(bf16),
        w_down[l].astype(bf16),
        ln2_g[l].reshape(1, D),
        ln2_b[l].reshape(1, D),
    )

    zp = jnp.zeros((POOL_BUF, bp, D_POOL), f32)
    zc = jnp.zeros((CONV_W - 1, bp, D_LRU), f32)
    zh = jnp.zeros((bp, D_LRU), f32)
    yp, pool_p, conv_p, lru_p = _run_layer(
        x_prompt, mod, bs // bp, zp, zc, zh, 0, bp, tile_rows // bp, False, wts)

    n_prev_s = min(PAST_LEN, POOL_BUF)
    ts = x_sample.shape[1]
    ys, pool_s, conv_s, lru_s = _run_layer(
        x_sample, mod, 0, state_pool[l].transpose(1, 0, 2), state_conv[l].transpose(1, 0, 2),
        state_lru[l], n_prev_s, tile_rows // ts, ts, False, wts)

    tr = lambda a: a.transpose(1, 0, 2)[None]
    return (yp, ys, tr(pool_p), tr(conv_p), lru_p[None], tr(pool_s), tr(conv_s), lru_s[None])
```

```python
import functools
import math

import jax
import jax.numpy as jnp
from jax import lax
from jax.experimental import pallas as pl
from jax.experimental.pallas import tpu as pltpu

D = 1024
D_POOL = 512
D_LRU = 512
D_IN = D_POOL + 2 * D_LRU
POOL_WINDOWS = (2, 4, 8, 16)
POOL_GROUP = 128
POOL_BUF = 15
CONV_W = 4
LRU_C = 8.0
N_GROUPS = 4
EPG = 4
D_EXPERT = 256
DEPTH = 1
ALPHA = (2.0 * DEPTH) ** 0.25
LN_EPS = 1e-5
PAST_LEN = 16384

LANES = 128
SUBLANES = 8
N_SLABS = D // LANES
ROUTE_ROWS = 48
CHUNK = 256
LANE_GSEL = 4
LANE_RANK = 5
VMEM_LIMIT = 60 * 1024 * 1024

f32 = jnp.float32
bf16 = jnp.bfloat16
i32 = jnp.int32


def _dot(a, b):
    return jnp.dot(a, b, preferred_element_type=f32)


def _sigmoid(x):
    return 1.0 / (1.0 + jnp.exp(-x))


def _gelu_tanh(x):
    c = math.sqrt(2.0 / math.pi)
    return 0.5 * x * (1.0 + jnp.tanh(c * (x + 0.044715 * (x * x * x))))


def _layer_norm(v, g, b):
    mu = jnp.mean(v, axis=-1, keepdims=True)
    c = v - mu
    var = jnp.mean(c * c, axis=-1, keepdims=True)
    return c * lax.rsqrt(var + LN_EPS) * g + b


def _drain(gen):
    while True:
        try:
            next(gen)
        except StopIteration as stop:
            return stop.value


def _alternate(first, second, ratio):
    live = [first, second]
    quota = {id(first): 1, id(second): ratio}
    while live:
        for gen in list(live):
            for _ in range(quota[id(gen)]):
                try:
                    next(gen)
                except StopIteration:
                    live.remove(gen)
                    break


def _ada_body(c_ref, w_ref, b_ref, o_ref):
    o_ref[...] = _dot(c_ref[...].astype(bf16), w_ref[...].astype(bf16)) + b_ref[...]


def _ada_mod(c_all, w_ada, b_ada):
    n = c_all.shape[0]
    bn = 512
    return pl.pallas_call(
        _ada_body,
        grid=(6 * D // bn,),
        in_specs=[
            pl.BlockSpec((n, D), lambda j: (0, 0)),
            pl.BlockSpec((D, bn), lambda j: (0, j)),
            pl.BlockSpec((1, bn), lambda j: (0, j)),
        ],
        out_specs=pl.BlockSpec((n, bn), lambda j: (0, j)),
        out_shape=jax.ShapeDtypeStruct((n, 6 * D), f32),
        name="ada_mod",
    )(c_all, w_ada, b_ada.reshape(1, 6 * D))


def _layer_body(nb, tb, pitch, n_prev, n_tt, n_tiles, pipelined,
                x_ref, mod_ref, pool0_ref, conv0_ref, h0_ref,
                w_in_ref, w_pool_ref, pscale_ref, w_conv_ref, b_conv_ref,
                w_ax_ref, b_ax_ref, lam_ref, w_out_ref, ln1g_ref, ln1b_ref,
                w_rt_ref, b_rt_ref, tri_ref, wg_ref, wu_ref, wd_ref, ln2g_ref, ln2b_ref,
                y_ref, npool_ref, nconv_ref, nh_ref,
                slab, xt, x1, u2, tok_t, tok3, rows_s, cnt_s,
                ycat, yacc, zpool, zconv, a_s, b_s, gl_s, h_s):
    tm = nb * tb
    q = pl.program_id(0)
    g = pl.program_id(1)
    mix_tile = jnp.minimum(q, n_tiles - 1)
    ti = mix_tile % n_tt
    if pipelined:
        slot_mix = q % 2
        slot_moe = (q + 1) % 2
    else:
        slot_mix = 0
        slot_moe = 0

    def mod_part(k):
        return mod_ref[:, pl.ds(k * D, D)]

    n_rb = 4
    tq = tb // n_rb
    rq = tq * nb

    def stage0():
        for b in range(nb):
            for j in range(N_SLABS):
                slab[j, pl.ds(b * pitch, tb), :] = x_ref[b, :, pl.ds(j * LANES, LANES)]
            if (b + 1) % (nb // n_rb) == 0:
                yield
        for t in range(tb):
            for j in range(N_SLABS):
                xt[t, :, pl.ds(j * LANES, LANES)] = slab[j, pl.ds(t, nb, stride=pitch), :]
            if (t + 1) % tq == 0:
                yield
        sh1 = mod_part(0)
        sc1 = mod_part(1)
        for r in range(n_rb):
            u = xt[pl.ds(r * tq, tq)] * (1.0 + sc1)[None] + sh1[None]
            ycat[pl.ds(r * rq, rq), :] = u.reshape(rq, D).astype(bf16)
            yield
        ub = ycat[...]
        zpool[pl.ds(POOL_BUF * nb, tm), :] = _dot(ub, w_in_ref[:, pl.ds(0, D_POOL)])
        yield
        zconv[pl.ds((CONV_W - 1) * nb, tm), :] = _dot(ub, w_in_ref[:, pl.ds(D_POOL, D_LRU)])
        yield
        half = D_LRU // 2
        for hpart in range(2):
            cols = pl.ds(D_POOL + D_LRU + hpart * half, half)
            gl_s[:, pl.ds(hpart * half, half)] = _gelu_tanh(_dot(ub, w_in_ref[:, cols]))
            yield

    def stage1():
        row = lax.broadcasted_iota(i32, (tm, LANES), 0)
        t_loc = lax.shift_right_logical(row, int(math.log2(nb)))
        t_glob = (ti * tb + t_loc + (1 + n_prev)).astype(f32)
        for c, w in enumerate(POOL_WINDOWS):
            lanes = pl.ds(c * LANES, LANES)
            s = zpool[pl.ds((POOL_BUF + 1 - w) * nb, tm + (w - 1) * nb), lanes]
            step = 1
            while step < w:
                s = s[step * nb:] + s[:-step * nb]
                step *= 2
            cnt = jnp.minimum(f32(w), t_glob)
            dlt = s / cnt - zpool[pl.ds(POOL_BUF * nb, tm), lanes]
            yp = _dot(dlt.astype(bf16), w_pool_ref[c]) * pscale_ref[:, lanes]
            ycat[:, lanes] = yp.astype(bf16)
            yield
        nl = -lam_ref[...]
        softplus = jnp.maximum(nl, 0.0) + jnp.log(1.0 + jnp.exp(-jnp.abs(nl)))
        for j in range(D_LRU // LANES):
            lanes = pl.ds(j * LANES, LANES)
            xc = b_conv_ref[:, lanes] + zconv[pl.ds(0, tm), lanes] * w_conv_ref[0:1, lanes]
            for k in range(1, CONV_W):
                xc = xc + zconv[pl.ds(k * nb, tm), lanes] * w_conv_ref[k:k + 1, lanes]
            xcb = xc.astype(bf16)
            r = _sigmoid(_dot(xcb, w_ax_ref[0, j]) + b_ax_ref[:, lanes])
            ig = _sigmoid(_dot(xcb, w_ax_ref[1, j]) + b_ax_ref[:, pl.ds(D_LRU + j * LANES, LANES)])
            log_a = (-LRU_C) * r * softplus[:, j * LANES:(j + 1) * LANES]
            a_s[:, lanes] = jnp.exp(log_a)
            mult = jnp.sqrt(jnp.maximum(1.0 - jnp.exp(2.0 * log_a), 0.0))
            b_s[:, lanes] = mult * ig * xc
            yield
        zpool[pl.ds(0, POOL_BUF * nb), :] = zpool[pl.ds(tm, POOL_BUF * nb), :]
        zconv[pl.ds(0, (CONV_W - 1) * nb), :] = zconv[pl.ds(tm, (CONV_W - 1) * nb), :]
        yield

    def stage2():
        h = h_s[...]
        for t in range(tb):
            rows = pl.ds(t * nb, nb)
            h = a_s[rows, :] * h + b_s[rows, :]
            b_s[rows, :] = h
            if (t + 1) % tq == 0:
                yield
        h_s[...] = h
        ycat[:, pl.ds(D_POOL, D_LRU)] = (b_s[...] * gl_s[...]).astype(bf16)
        yield
        g1 = mod_part(2)
        yc = ycat[...]
        cb = D // n_rb
        for j in range(n_rb):
            cols = pl.ds(j * cb, cb)
            mix = _dot(yc, w_out_ref[:, cols]).reshape(tb, nb, cb)
            xt[:, :, cols] = ALPHA * xt[:, :, cols] + (1.0 + g1[:, j * cb:(j + 1) * cb])[None] * mix
            yield
        for r in range(n_rb):
            ts = pl.ds(r * tq, tq)
            x1[slot_mix, ts] = _layer_norm(xt[ts], ln1g_ref[...][None], ln1b_ref[...][None])
            yield

    def stage3():
        sh2 = mod_part(3)
        sc2 = mod_part(4)
        for r in range(n_rb):
            v = x1[slot_mix, pl.ds(r * tq, tq)] * (1.0 + sc2)[None] + sh2[None]
            u2[slot_mix, pl.ds(r * rq, rq), :] = v.reshape(rq, D).astype(bf16)
            yield
        lt = lax.dot_general(w_rt_ref[...], u2[slot_mix], (((1,), (1,)), ((), ())),
                             preferred_element_type=f32) + b_rt_ref[...]
        gl = [lt[k:k + 1, :] for k in range(N_GROUPS)]
        best = gl[0]
        gsel = jnp.zeros_like(best, dtype=i32)
        for k in range(1, N_GROUPS):
            better = gl[k] > best
            best = jnp.where(better, gl[k], best)
            gsel = jnp.where(better, k, gsel)
        denom = jnp.exp(gl[0] - best)
        for k in range(1, N_GROUPS):
            denom = denom + jnp.exp(gl[k] - best)
        p_sel = 1.0 / denom
        es = []
        for j in range(EPG):
            v_j = lt[SUBLANES + j:SUBLANES + j + 1, :]
            for k in range(1, N_GROUPS):
                r0 = SUBLANES * (k + 1) + j
                v_j = jnp.where(gsel == k, lt[r0:r0 + 1, :], v_j)
            es.append(v_j)
        v1 = es[0]
        i1 = jnp.zeros_like(gsel)
        for j in range(1, EPG):
            better = es[j] > v1
            v1 = jnp.where(better, es[j], v1)
            i1 = jnp.where(better, j, i1)
        v2 = jnp.full_like(v1, -jnp.inf)
        i2 = jnp.full_like(gsel, -1)
        for j in range(EPG):
            cand = jnp.logical_and(i1 != j, jnp.logical_or(i2 < 0, es[j] > v2))
            v2 = jnp.where(cand, es[j], v2)
            i2 = jnp.where(cand, j, i2)
        e21 = jnp.exp(v2 - v1)
        w1 = p_sel / (1.0 + e21)
        w2 = p_sel * e21 / (1.0 + e21)
        rid8 = lax.broadcasted_iota(i32, (SUBLANES, tm), 0)
        onehot = jnp.where(rid8 == gsel, 1.0, 0.0)
        pref = _dot(onehot.astype(bf16), tri_ref[...])
        rank = jnp.sum(onehot * pref, axis=0, keepdims=True)
        for k in range(N_GROUPS):
            cnt_s[slot_mix * N_GROUPS + k] = jnp.sum(jnp.where(gsel == k, 1.0, 0.0)).astype(i32)
        rows_s[slot_mix, 0:1, :] = gsel
        rows_s[slot_mix, 1:2, :] = rank.astype(i32)
        yield
        rid = lax.broadcasted_iota(i32, (LANES, tm), 0)
        rec = jnp.zeros((LANES, tm), f32)
        for j in range(EPG):
            wj = jnp.where(i1 == j, w1, jnp.where(i2 == j, w2, 0.0))
            rec = jnp.where(rid == j, wj, rec)
        rec = jnp.where(rid == LANE_GSEL, gsel.astype(f32), rec)
        rec = jnp.where(rid == LANE_RANK, rank, rec)
        rt = rec.T
        tok_t[slot_mix] = rt
        hi = rt.astype(bf16)
        r1 = rt - hi.astype(f32)
        mid = r1.astype(bf16)
        lo = (r1 - mid.astype(f32)).astype(bf16)
        tok3[slot_mix, :, pl.ds(0, LANES)] = hi
        tok3[slot_mix, :, pl.ds(LANES, LANES)] = mid
        tok3[slot_mix, :, pl.ds(2 * LANES, LANES)] = lo
        yield

    def moe_chunk(k, base):
        gsel_row = rows_s[slot_moe, 0:1, :]
        rank_row = rows_s[slot_moe, 1:2, :]
        jj = lax.broadcasted_iota(i32, (CHUNK, tm), 0) + base
        pm = jnp.where(jnp.logical_and(gsel_row == k, rank_row == jj), 1.0, 0.0).astype(bf16)
        xs = _dot(pm, u2[slot_moe]).astype(bf16)
        yield
        cs3 = _dot(pm, tok3[slot_moe])
        cs = cs3[:, :LANES] + cs3[:, LANES:2 * LANES] + cs3[:, 2 * LANES:]
        ys = None
        for e in range(EPG):
            hg = _dot(xs, wg_ref[e])
            hu = _dot(xs, wu_ref[e])
            act = hg * _sigmoid(hg) * hu * cs[:, e:e + 1]
            part = _dot(act.astype(bf16), wd_ref[e])
            ys = part if ys is None else ys + part
            yield
        rec = tok_t[slot_moe]
        gsel_col = rec[:, LANE_GSEL:LANE_GSEL + 1]
        rank_col = rec[:, LANE_RANK:LANE_RANK + 1]
        jl = (lax.broadcasted_iota(i32, (tm, CHUNK), 1) + base).astype(f32)
        pt = jnp.where(jnp.logical_and(gsel_col == f32(k), rank_col == jl), 1.0, 0.0).astype(bf16)
        return _dot(pt, ys.astype(bf16))

    def moe_group(k):
        contrib = yield from moe_chunk(k, 0)
        if k == 0:
            yacc[...] = contrib
        else:
            yacc[...] += contrib
        yield

    def moe_overflow(k):
        n_k = cnt_s[slot_moe * N_GROUPS + k]
        n_chunks = lax.shift_right_logical(n_k + (CHUNK - 1), int(math.log2(CHUNK)))

        def body(c, carry):
            yacc[...] += _drain(moe_chunk(k, c * CHUNK))
            return carry

        lax.fori_loop(1, n_chunks, body, 0)

    def finish():
        g2 = mod_part(5)
        for r in range(n_rb):
            ts = pl.ds(r * tq, tq)
            v = ALPHA * x1[slot_moe, ts] + (1.0 + g2)[None] * yacc[pl.ds(r * rq, rq), :].reshape(tq, nb, D)
            xt[ts] = _layer_norm(v, ln2g_ref[...][None], ln2b_ref[...][None])
            yield
        for t in range(tb):
            for j in range(N_SLABS):
                slab[j, pl.ds(t, nb, stride=pitch), :] = xt[t, :, pl.ds(j * LANES, LANES)]
            if (t + 1) % tq == 0:
                yield
        for b in range(nb):
            for j in range(N_SLABS):
                y_ref[b, :, pl.ds(j * LANES, LANES)] = slab[j, pl.ds(b * pitch, tb), :]
            if (b + 1) % (nb // n_rb) == 0:
                yield

    @pl.when(jnp.logical_and(g == 0, ti == 0))
    def _init_state():
        zpool[pl.ds(0, POOL_BUF * nb), :] = pool0_ref[...].reshape(POOL_BUF * nb, D_POOL)
        zconv[pl.ds(0, (CONV_W - 1) * nb), :] = conv0_ref[...].reshape((CONV_W - 1) * nb, D_LRU)
        h_s[...] = h0_ref[...]

    if pipelined:
        @pl.when(jnp.logical_and(g == 0, q == 0))
        def _init_moe_side():
            x1[1] = jnp.zeros((tb, nb, D), f32)
            u2[1] = jnp.zeros((tm, D), bf16)
            tok_t[1] = jnp.zeros((tm, LANES), f32)
            tok3[1] = jnp.zeros((tm, 3 * LANES), bf16)
            rows_s[1] = jnp.zeros((SUBLANES, tm), i32)
            for k in range(N_GROUPS):
                cnt_s[N_GROUPS + k] = 0

    stages = (stage0, stage1, stage2, stage3)
    for k in range(N_GROUPS):
        @pl.when(g == k)
        def _step(k=k):
            if pipelined:
                _alternate(moe_group(k), stages[k](), 2)
            else:
                if k == 0:
                    for st in stages:
                        _drain(st())
                _drain(moe_group(k))

        emit_here = (k == 2) if pipelined else (k == 0)
        if emit_here:
            @pl.when(jnp.logical_and(g == k, jnp.logical_and(q < n_tiles, ti == n_tt - 1)))
            def _emit_state():
                npool_ref[...] = zpool[pl.ds(0, POOL_BUF * nb), :].reshape(POOL_BUF, nb, D_POOL)
                nconv_ref[...] = zconv[pl.ds(0, (CONV_W - 1) * nb), :].reshape(CONV_W - 1, nb, D_LRU)
                nh_ref[...] = h_s[...]

        @pl.when(g == k)
        def _overflow(k=k):
            moe_overflow(k)

    @pl.when(g == N_GROUPS - 1)
    def _finish():
        _drain(finish())


def _run_layer(x, mod, mod_blk0, pool0, conv0, h0, n_prev, nb, tb, pipelined, wts):
    bsz, t_len, _ = x.shape
    n_sb = bsz // nb
    n_tt = t_len // tb
    n_tiles = n_sb * n_tt
    tm = nb * tb
    assert not pipelined or n_sb == 1
    n_outer = n_tiles + (1 if pipelined else 0)
    n_slots = 2 if pipelined else 1
    pitch = tb + SUBLANES if (tb // SUBLANES) % 2 == 0 else tb

    def mix_st(q):
        tile = jnp.minimum(q, n_tiles - 1)
        return tile // n_tt, tile % n_tt

    def out_st(q):
        tile = jnp.maximum(q - 1, 0) if pipelined else q
        return tile // n_tt, tile % n_tt

    const2 = lambda q, g: (0, 0)
    const3 = lambda q, g: (0, 0, 0)
    const4 = lambda q, g: (0, 0, 0, 0)
    grp3 = lambda q, g: (g, 0, 0)
    single = pl.Buffered(1)
    in_specs = [
        pl.BlockSpec((nb, tb, D), lambda q, g: (*mix_st(q), 0)),
        pl.BlockSpec((nb, 6 * D), lambda q, g: (mod_blk0 + mix_st(q)[0], 0)),
        pl.BlockSpec((POOL_BUF, nb, D_POOL), lambda q, g: (0, mix_st(q)[0], 0)),
        pl.BlockSpec((CONV_W - 1, nb, D_LRU), lambda q, g: (0, mix_st(q)[0], 0)),
        pl.BlockSpec((nb, D_LRU), lambda q, g: (mix_st(q)[0], 0)),
        pl.BlockSpec((D, D_IN), const2, pipeline_mode=single),
        pl.BlockSpec((4, POOL_GROUP, POOL_GROUP), const3),
        pl.BlockSpec((1, D_POOL), const2),
        pl.BlockSpec((CONV_W, D_LRU), const2),
        pl.BlockSpec((1, D_LRU), const2),
        pl.BlockSpec((2, D_LRU // LANES, LANES, LANES), const4),
        pl.BlockSpec((1, 2 * D_LRU), const2),
        pl.BlockSpec((1, D_LRU), const2),
        pl.BlockSpec((D, D), const2, pipeline_mode=single),
        pl.BlockSpec((1, D), const2),
        pl.BlockSpec((1, D), const2),
        pl.BlockSpec((ROUTE_ROWS, D), const2),
        pl.BlockSpec((ROUTE_ROWS, 1), const2),
        pl.BlockSpec((tm, tm), const2, pipeline_mode=single),
        pl.BlockSpec((EPG, D, D_EXPERT), grp3),
        pl.BlockSpec((EPG, D, D_EXPERT), grp3),
        pl.BlockSpec((EPG, D_EXPERT, D), grp3),
        pl.BlockSpec((1, D), const2),
        pl.BlockSpec((1, D), const2),
    ]
    out_specs = [
        pl.BlockSpec((nb, tb, D), lambda q, g: (*out_st(q), 0)),
        pl.BlockSpec((POOL_BUF, nb, D_POOL), lambda q, g: (0, mix_st(q)[0], 0)),
        pl.BlockSpec((CONV_W - 1, nb, D_LRU), lambda q, g: (0, mix_st(q)[0], 0)),
        pl.BlockSpec((nb, D_LRU), lambda q, g: (mix_st(q)[0], 0)),
    ]
    out_shape = [
        jax.ShapeDtypeStruct((bsz, t_len, D), f32),
        jax.ShapeDtypeStruct((POOL_BUF, bsz, D_POOL), f32),
        jax.ShapeDtypeStruct((CONV_W - 1, bsz, D_LRU), f32),
        jax.ShapeDtypeStruct((bsz, D_LRU), f32),
    ]
    scratch = [
        pltpu.VMEM((N_SLABS, nb * pitch, LANES), f32),
        pltpu.VMEM((tb, nb, D), f32),
        pltpu.VMEM((n_slots, tb, nb, D), f32),
        pltpu.VMEM((n_slots, tm, D), bf16),
        pltpu.VMEM((n_slots, tm, LANES), f32),
        pltpu.VMEM((n_slots, tm, 3 * LANES), bf16),
        pltpu.VMEM((n_slots, SUBLANES, tm), i32),
        pltpu.SMEM((n_slots * N_GROUPS,), i32),
        pltpu.VMEM((tm, D), bf16),
        pltpu.VMEM((tm, D), f32),
        pltpu.VMEM(((tb + POOL_BUF) * nb, D_POOL), f32),
        pltpu.VMEM(((tb + CONV_W - 1) * nb, D_LRU), f32),
        pltpu.VMEM((tm, D_LRU), f32),
        pltpu.VMEM((tm, D_LRU), f32),
        pltpu.VMEM((tm, D_LRU), f32),
        pltpu.VMEM((nb, D_LRU), f32),
    ]
    body = functools.partial(_layer_body, nb, tb, pitch, n_prev, n_tt, n_tiles, pipelined)
    return pl.pallas_call(
        body,
        grid=(n_outer, N_GROUPS),
        in_specs=in_specs,
        out_specs=out_specs,
        out_shape=out_shape,
        scratch_shapes=scratch,
        compiler_params=pltpu.CompilerParams(
            dimension_semantics=("arbitrary", "arbitrary"),
            vmem_limit_bytes=VMEM_LIMIT),
        name=f"layer_nb{nb}_tb{tb}",
    )(x, mod, pool0, conv0, h0, *wts)


def _pair_blocks(w):
    n, k, _ = w.shape
    wp = w.reshape(n // 2, 2, k, k)
    eye = jnp.eye(2, dtype=w.dtype)
    return jnp.einsum('phij,hg->phigj', wp, eye).reshape(n // 2, 2 * k, 2 * k)


def kernel(x_prompt, x_sample, c_prompt, c_sample, state_pool, state_conv, state_lru, w_ada, b_ada, w_in, w_pool, pool_scale, w_conv, b_conv, w_a, b_a, w_x, b_x, lru_lambda, w_out, ln1_g, ln1_b, w_group, b_group, w_route, b_route, w_gate, w_up, w_down, ln2_g, ln2_b):
    l = 0
    bp = x_prompt.shape[0]
    bs = x_sample.shape[0]
    tile_rows = 512
    mod = _ada_mod(jnp.concatenate([c_sample, c_prompt], axis=0), w_ada[l], b_ada[l])

    w_rt = jnp.zeros((ROUTE_ROWS, D), f32).at[0:N_GROUPS].set(w_group[l].T)
    b_rt = jnp.zeros((ROUTE_ROWS,), f32).at[0:N_GROUPS].set(b_group[l])
    for k in range(N_GROUPS):
        r0 = SUBLANES * (k + 1)
        w_rt = w_rt.at[r0:r0 + EPG].set(w_route[l][:, k * EPG:(k + 1) * EPG].T)
        b_rt = b_rt.at[r0:r0 + EPG].set(b_route[l][k * EPG:(k + 1) * EPG])
    tri = jnp.triu(jnp.ones((tile_rows, tile_rows), bf16), 1)

    wts = (
        w_in[l].astype(bf16),
        w_pool[l].astype(bf16),
        pool_scale[l].reshape(1, D_POOL),
        w_conv[l],
        b_conv[l].reshape(1, D_LRU),
        jnp.stack([_pair_blocks(w_a[l]), _pair_blocks(w_x[l])]).astype(bf16),
        jnp.concatenate([b_a[l], b_x[l]]).reshape(1, 2 * D_LRU),
        lru_lambda[l].reshape(1, D_LRU),
        w_out[l].astype(bf16),
        ln1_g[l].reshape(1, D),
        ln1_b[l].reshape(1, D),
        w_rt.astype(bf16),
        b_rt.reshape(ROUTE_ROWS, 1),
        tri,
        w_gate[l].astype(bf16),
        w_up[l].astype(bf16),
        w_down[l].astype(bf16),
        ln2_g[l].reshape(1, D),
        ln2_b[l].reshape(1, D),
    )

    zp = jnp.zeros((POOL_BUF, bp, D_POOL), f32)
    zc = jnp.zeros((CONV_W - 1, bp, D_LRU), f32)
    zh = jnp.zeros((bp, D_LRU), f32)
    yp, pool_p, conv_p, lru_p = _run_layer(
        x_prompt, mod, bs // bp, zp, zc, zh, 0, bp, tile_rows // bp, True, wts)

    n_prev_s = min(PAST_LEN, POOL_BUF)
    ts = x_sample.shape[1]
    ys, pool_s, conv_s, lru_s = _run_layer(
        x_sample, mod, 0, state_pool[l].transpose(1, 0, 2), state_conv[l].transpose(1, 0, 2),
        state_lru[l], n_prev_s, tile_rows // ts, ts, False, wts)

    tr = lambda a: a.transpose(1, 0, 2)[None]
    return (yp, ys, tr(pool_p), tr(conv_p), lru_p[None], tr(pool_s), tr(conv_s), lru_s[None])
```

```python
import functools
import math

import jax
import jax.numpy as jnp
from jax import lax
from jax.experimental import pallas as pl
from jax.experimental.pallas import tpu as pltpu

D = 1024
D_POOL = 512
D_LRU = 512
D_IN = D_POOL + 2 * D_LRU
POOL_WINDOWS = (2, 4, 8, 16)
POOL_GROUP = 128
POOL_BUF = 15
CONV_W = 4
LRU_C = 8.0
N_GROUPS = 4
EPG = 4
D_EXPERT = 256
DEPTH = 1
ALPHA = (2.0 * DEPTH) ** 0.25
LN_EPS = 1e-5
PAST_LEN = 16384

LANES = 128
SUBLANES = 8
N_SLABS = D // LANES
ROUTE_ROWS = 48
MOE_ROWS = 1024
CHUNK = 320
ROW_LO = 4
ROW_LO2 = 8
ROW_GSEL = 12
ROW_RANK = 13
N_RB = 4
VMEM_LIMIT = 60 * 1024 * 1024

f32 = jnp.float32
bf16 = jnp.bfloat16
i32 = jnp.int32


def _dot(a, b):
    return jnp.dot(a, b, preferred_element_type=f32)


def _sigmoid(x):
    return 1.0 / (1.0 + jnp.exp(-x))


def _gelu_tanh(x):
    c = math.sqrt(2.0 / math.pi)
    return 0.5 * x * (1.0 + jnp.tanh(c * (x + 0.044715 * (x * x * x))))


def _layer_norm(v, g, b):
    mu = jnp.mean(v, axis=-1, keepdims=True)
    c = v - mu
    var = jnp.mean(c * c, axis=-1, keepdims=True)
    return c * lax.rsqrt(var + LN_EPS) * g + b


def _slab_pitch(tb):
    return tb + SUBLANES if (tb // SUBLANES) % 2 == 0 else tb


def _ada_body(c_ref, w_ref, b_ref, o_ref):
    o_ref[...] = _dot(c_ref[...].astype(bf16), w_ref[...].astype(bf16)) + b_ref[...]


def _ada_mod(c_all, w_ada, b_ada):
    n = c_all.shape[0]
    bn = 512
    return pl.pallas_call(
        _ada_body,
        grid=(6 * D // bn,),
        in_specs=[
            pl.BlockSpec((n, D), lambda j: (0, 0)),
            pl.BlockSpec((D, bn), lambda j: (0, j)),
            pl.BlockSpec((1, bn), lambda j: (0, j)),
        ],
        out_specs=pl.BlockSpec((n, bn), lambda j: (0, j)),
        out_shape=jax.ShapeDtypeStruct((n, 6 * D), f32),
        name="ada_mod",
    )(c_all, w_ada, b_ada.reshape(1, 6 * D))


def _mix_body(nb, tb, pitch, n_prev, n_tt,
              x_ref, mod_ref, pool0_ref, conv0_ref, h0_ref,
              w_in_ref, w_pool_ref, pscale_ref, w_conv_ref, b_conv_ref,
              w_ax_ref, b_ax_ref, lam_ref, w_out_ref, ln1g_ref, ln1b_ref,
              x1_ref, npool_ref, nconv_ref, nh_ref,
              slab, xt, ycat, zpool, zconv, a_s, b_s, gl_s, h_s):
    tm = nb * tb
    tq = tb // N_RB
    rq = tq * nb
    ti = pl.program_id(0) % n_tt

    def mod_part(k):
        return mod_ref[:, pl.ds(k * D, D)]

    @pl.when(ti == 0)
    def _init_state():
        zpool[pl.ds(0, POOL_BUF * nb), :] = pool0_ref[...].reshape(POOL_BUF * nb, D_POOL)
        zconv[pl.ds(0, (CONV_W - 1) * nb), :] = conv0_ref[...].reshape((CONV_W - 1) * nb, D_LRU)
        h_s[...] = h0_ref[...]

    for b in range(nb):
        for j in range(N_SLABS):
            slab[j, pl.ds(b * pitch, tb), :] = x_ref[b, :, pl.ds(j * LANES, LANES)]
    for t in range(tb):
        for j in range(N_SLABS):
            xt[t, :, pl.ds(j * LANES, LANES)] = slab[j, pl.ds(t, nb, stride=pitch), :]

    sh1 = mod_part(0)
    sc1 = mod_part(1)
    for r in range(N_RB):
        u = xt[pl.ds(r * tq, tq)] * (1.0 + sc1)[None] + sh1[None]
        ycat[pl.ds(r * rq, rq), :] = u.reshape(rq, D).astype(bf16)
    ub = ycat[...]
    zpool[pl.ds(POOL_BUF * nb, tm), :] = _dot(ub, w_in_ref[:, pl.ds(0, D_POOL)])
    zconv[pl.ds((CONV_W - 1) * nb, tm), :] = _dot(ub, w_in_ref[:, pl.ds(D_POOL, D_LRU)])
    half = D_LRU // 2
    for hpart in range(2):
        cols = pl.ds(D_POOL + D_LRU + hpart * half, half)
        gl_s[:, pl.ds(hpart * half, half)] = _gelu_tanh(_dot(ub, w_in_ref[:, cols]))

    row = lax.broadcasted_iota(i32, (tm, LANES), 0)
    t_loc = lax.shift_right_logical(row, int(math.log2(nb)))
    t_glob = (ti * tb + t_loc + (1 + n_prev)).astype(f32)
    for c, w in enumerate(POOL_WINDOWS):
        lanes = pl.ds(c * LANES, LANES)
        s = zpool[pl.ds((POOL_BUF + 1 - w) * nb, tm + (w - 1) * nb), lanes]
        step = 1
        while step < w:
            s = s[step * nb:] + s[:-step * nb]
            step *= 2
        cnt = jnp.minimum(f32(w), t_glob)
        dlt = s / cnt - zpool[pl.ds(POOL_BUF * nb, tm), lanes]
        yp = _dot(dlt.astype(bf16), w_pool_ref[c]) * pscale_ref[:, lanes]
        ycat[:, lanes] = yp.astype(bf16)

    nl = -lam_ref[...]
    softplus = jnp.maximum(nl, 0.0) + jnp.log(1.0 + jnp.exp(-jnp.abs(nl)))
    for j in range(D_LRU // LANES):
        lanes = pl.ds(j * LANES, LANES)
        xc = b_conv_ref[:, lanes] + zconv[pl.ds(0, tm), lanes] * w_conv_ref[0:1, lanes]
        for k in range(1, CONV_W):
            xc = xc + zconv[pl.ds(k * nb, tm), lanes] * w_conv_ref[k:k + 1, lanes]
        xcb = xc.astype(bf16)
        r = _sigmoid(_dot(xcb, w_ax_ref[0, j]) + b_ax_ref[:, lanes])
        ig = _sigmoid(_dot(xcb, w_ax_ref[1, j]) + b_ax_ref[:, pl.ds(D_LRU + j * LANES, LANES)])
        log_a = (-LRU_C) * r * softplus[:, j * LANES:(j + 1) * LANES]
        a_s[:, lanes] = jnp.exp(log_a)
        mult = jnp.sqrt(jnp.maximum(1.0 - jnp.exp(2.0 * log_a), 0.0))
        b_s[:, lanes] = mult * ig * xc

    zpool[pl.ds(0, POOL_BUF * nb), :] = zpool[pl.ds(tm, POOL_BUF * nb), :]
    zconv[pl.ds(0, (CONV_W - 1) * nb), :] = zconv[pl.ds(tm, (CONV_W - 1) * nb), :]

    h = h_s[...]
    for t in range(tb):
        rows = pl.ds(t * nb, nb)
        h = a_s[rows, :] * h + b_s[rows, :]
        b_s[rows, :] = h
    h_s[...] = h
    ycat[:, pl.ds(D_POOL, D_LRU)] = (b_s[...] * gl_s[...]).astype(bf16)

    g1 = mod_part(2)
    yc = ycat[...]
    cb = D // N_RB
    for j in range(N_RB):
        cols = pl.ds(j * cb, cb)
        mix = _dot(yc, w_out_ref[:, cols]).reshape(tb, nb, cb)
        xt[:, :, cols] = ALPHA * xt[:, :, cols] + (1.0 + g1[:, j * cb:(j + 1) * cb])[None] * mix
    for r in range(N_RB):
        xn = _layer_norm(xt[pl.ds(r * tq, tq)], ln1g_ref[...][None], ln1b_ref[...][None])
        x1_ref[pl.ds(r * rq, rq), :] = xn.reshape(rq, D)

    @pl.when(ti == n_tt - 1)
    def _emit_state():
        npool_ref[...] = zpool[pl.ds(0, POOL_BUF * nb), :].reshape(POOL_BUF, nb, D_POOL)
        nconv_ref[...] = zconv[pl.ds(0, (CONV_W - 1) * nb), :].reshape(CONV_W - 1, nb, D_LRU)
        nh_ref[...] = h_s[...]


def _run_mix(x, mod, mod_blk0, pool0, conv0, h0, n_prev, nb, tb, wts):
    bsz, t_len, _ = x.shape
    n_sb = bsz // nb
    n_tt = t_len // tb
    tm = nb * tb
    pitch = _slab_pitch(tb)
    st = lambda q: (q // n_tt, q % n_tt)
    const2 = lambda q: (0, 0)
    const3 = lambda q: (0, 0, 0)
    const4 = lambda q: (0, 0, 0, 0)
    single = pl.Buffered(1)
    in_specs = [
        pl.BlockSpec((nb, tb, D), lambda q: (*st(q), 0)),
        pl.BlockSpec((nb, 6 * D), lambda q: (mod_blk0 + st(q)[0], 0)),
        pl.BlockSpec((POOL_BUF, nb, D_POOL), lambda q: (0, st(q)[0], 0)),
        pl.BlockSpec((CONV_W - 1, nb, D_LRU), lambda q: (0, st(q)[0], 0)),
        pl.BlockSpec((nb, D_LRU), lambda q: (st(q)[0], 0)),
        pl.BlockSpec((D, D_IN), const2, pipeline_mode=single),
        pl.BlockSpec((4, POOL_GROUP, POOL_GROUP), const3),
        pl.BlockSpec((1, D_POOL), const2),
        pl.BlockSpec((CONV_W, D_LRU), const2),
        pl.BlockSpec((1, D_LRU), const2),
        pl.BlockSpec((2, D_LRU // LANES, LANES, LANES), const4),
        pl.BlockSpec((1, 2 * D_LRU), const2),
        pl.BlockSpec((1, D_LRU), const2),
        pl.BlockSpec((D, D), const2, pipeline_mode=single),
        pl.BlockSpec((1, D), const2),
        pl.BlockSpec((1, D), const2),
    ]
    out_specs = [
        pl.BlockSpec((tm, D), lambda q: (q, 0)),
        pl.BlockSpec((POOL_BUF, nb, D_POOL), lambda q: (0, st(q)[0], 0)),
        pl.BlockSpec((CONV_W - 1, nb, D_LRU), lambda q: (0, st(q)[0], 0)),
        pl.BlockSpec((nb, D_LRU), lambda q: (st(q)[0], 0)),
    ]
    out_shape = [
        jax.ShapeDtypeStruct((bsz * t_len, D), f32),
        jax.ShapeDtypeStruct((POOL_BUF, bsz, D_POOL), f32),
        jax.ShapeDtypeStruct((CONV_W - 1, bsz, D_LRU), f32),
        jax.ShapeDtypeStruct((bsz, D_LRU), f32),
    ]
    scratch = [
        pltpu.VMEM((N_SLABS, nb * pitch, LANES), f32),
        pltpu.VMEM((tb, nb, D), f32),
        pltpu.VMEM((tm, D), bf16),
        pltpu.VMEM(((tb + POOL_BUF) * nb, D_POOL), f32),
        pltpu.VMEM(((tb + CONV_W - 1) * nb, D_LRU), f32),
        pltpu.VMEM((tm, D_LRU), f32),
        pltpu.VMEM((tm, D_LRU), f32),
        pltpu.VMEM((tm, D_LRU), f32),
        pltpu.VMEM((nb, D_LRU), f32),
    ]
    body = functools.partial(_mix_body, nb, tb, pitch, n_prev, n_tt)
    return pl.pallas_call(
        body,
        grid=(n_sb * n_tt,),
        in_specs=in_specs,
        out_specs=out_specs,
        out_shape=out_shape,
        scratch_shapes=scratch,
        compiler_params=pltpu.CompilerParams(
            dimension_semantics=("arbitrary",),
            vmem_limit_bytes=VMEM_LIMIT),
        name=f"mix_nb{nb}_tb{tb}",
    )(x, mod, pool0, conv0, h0, *wts)


def _moe_body(n_half, nb, tb, pitch,
              x1_ref, mod_ref, w_rt_ref, b_rt_ref, tri_ref, wg_ref, wu_ref, wd_ref, ln2g_ref, ln2b_ref,
              y_ref,
              u2, tok_t, tok3, rows_s, cnt_s, yacc, stage, slab):
    tm = MOE_ROWS
    hm = nb * tb
    tq = tb // N_RB
    rq = tq * nb
    g = pl.program_id(1)

    def mod_part(h, k):
        return mod_ref[pl.ds(h * nb, nb), pl.ds(k * D, D)]

    @pl.when(g == 0)
    def _route():
        for h in range(n_half):
            sh2 = mod_part(h, 3)
            sc2 = mod_part(h, 4)
            for r in range(N_RB):
                rows = pl.ds(h * hm + r * rq, rq)
                v = x1_ref[rows, :].reshape(tq, nb, D) * (1.0 + sc2)[None] + sh2[None]
                u2[rows, :] = v.reshape(rq, D).astype(bf16)
        lt = lax.dot_general(w_rt_ref[...], u2[...], (((1,), (1,)), ((), ())),
                             preferred_element_type=f32) + b_rt_ref[...]
        gl = [lt[k:k + 1, :] for k in range(N_GROUPS)]
        best = gl[0]
        gsel = jnp.zeros_like(best, dtype=i32)
        for k in range(1, N_GROUPS):
            better = gl[k] > best
            best = jnp.where(better, gl[k], best)
            gsel = jnp.where(better, k, gsel)
        denom = jnp.exp(gl[0] - best)
        for k in range(1, N_GROUPS):
            denom = denom + jnp.exp(gl[k] - best)
        p_sel = 1.0 / denom
        es = []
        for j in range(EPG):
            v_j = lt[SUBLANES + j:SUBLANES + j + 1, :]
            for k in range(1, N_GROUPS):
                r0 = SUBLANES * (k + 1) + j
                v_j = jnp.where(gsel == k, lt[r0:r0 + 1, :], v_j)
            es.append(v_j)
        v1 = es[0]
        i1 = jnp.zeros_like(gsel)
        for j in range(1, EPG):
            better = es[j] > v1
            v1 = jnp.where(better, es[j], v1)
            i1 = jnp.where(better, j, i1)
        v2 = jnp.full_like(v1, -jnp.inf)
        i2 = jnp.full_like(gsel, -1)
        for j in range(EPG):
            cand = jnp.logical_and(i1 != j, jnp.logical_or(i2 < 0, es[j] > v2))
            v2 = jnp.where(cand, es[j], v2)
            i2 = jnp.where(cand, j, i2)
        e21 = jnp.exp(v2 - v1)
        w1 = p_sel / (1.0 + e21)
        w2 = p_sel * e21 / (1.0 + e21)
        rid8 = lax.broadcasted_iota(i32, (SUBLANES, tm), 0)
        onehot = jnp.where(rid8 == gsel, 1.0, 0.0)
        pref = _dot(onehot.astype(bf16), tri_ref[...])
        rank = jnp.sum(onehot * pref, axis=0, keepdims=True)
        for k in range(N_GROUPS):
            cnt_s[k] = jnp.sum(jnp.where(gsel == k, 1.0, 0.0)).astype(i32)
        rows_s[0:1, :] = gsel
        rows_s[1:2, :] = rank.astype(i32)
        rid = lax.broadcasted_iota(i32, (LANES, tm), 0)
        rec = jnp.zeros((LANES, tm), f32)
        for j in range(EPG):
            wj = jnp.where(i1 == j, w1, jnp.where(i2 == j, w2, 0.0))
            hi = wj.astype(bf16).astype(f32)
            mid = (wj - hi).astype(bf16).astype(f32)
            lo = (wj - hi - mid).astype(bf16).astype(f32)
            rec = jnp.where(rid == j, hi, rec)
            rec = jnp.where(rid == ROW_LO + j, mid, rec)
            rec = jnp.where(rid == ROW_LO2 + j, lo, rec)
        rec = jnp.where(rid == ROW_GSEL, gsel.astype(f32), rec)
        rec = jnp.where(rid == ROW_RANK, rank, rec)
        rt = rec.T
        tok_t[...] = rt
        tok3[...] = rt.astype(bf16)

    def moe_chunk(base):
        gsel_row = rows_s[0:1, :]
        rank_row = rows_s[1:2, :]
        jj = lax.broadcasted_iota(i32, (CHUNK, tm), 0) + base
        pm = jnp.where(jnp.logical_and(gsel_row == g, rank_row == jj), 1.0, 0.0).astype(bf16)
        xs = _dot(pm, u2[...]).astype(bf16)
        cs3 = _dot(pm, tok3[...])
        cs = (cs3 + pltpu.roll(cs3, LANES - ROW_LO, axis=1)
              + pltpu.roll(cs3, LANES - ROW_LO2, axis=1))
        ys = None
        for e in range(EPG):
            hg = _dot(xs, wg_ref[e])
            hu = _dot(xs, wu_ref[e])
            act = hg * _sigmoid(hg) * hu * cs[:, e:e + 1]
            part = _dot(act.astype(bf16), wd_ref[e])
            ys = part if ys is None else ys + part
        rec = tok_t[...]
        gsel_col = rec[:, ROW_GSEL:ROW_GSEL + 1]
        rank_col = rec[:, ROW_RANK:ROW_RANK + 1]
        jl = (lax.broadcasted_iota(i32, (tm, CHUNK), 1) + base).astype(f32)
        pt = jnp.where(jnp.logical_and(gsel_col == g.astype(f32), rank_col == jl), 1.0, 0.0).astype(bf16)
        return _dot(pt, ys.astype(bf16))

    contrib = moe_chunk(0)

    @pl.when(g == 0)
    def _first():
        yacc[...] = contrib

    @pl.when(g > 0)
    def _rest():
        yacc[...] += contrib

    n_chunks = pl.cdiv(cnt_s[g], CHUNK)

    def overflow(c, carry):
        yacc[...] += moe_chunk(c * CHUNK)
        return carry

    lax.fori_loop(1, n_chunks, overflow, 0)

    @pl.when(g == N_GROUPS - 1)
    def _finish():
        for h in range(n_half):
            g2 = mod_part(h, 5)
            for r in range(N_RB):
                rows = pl.ds(h * hm + r * rq, rq)
                v = (ALPHA * x1_ref[rows, :] + ((1.0 + g2)[None] * yacc[rows, :].reshape(tq, nb, D)).reshape(rq, D))
                stage[rows, :] = _layer_norm(v, ln2g_ref[...], ln2b_ref[...])
            for t in range(tb):
                for j in range(N_SLABS):
                    slab[j, pl.ds(t, nb, stride=pitch), :] = stage[pl.ds(h * hm + t * nb, nb), pl.ds(j * LANES, LANES)]
            for b in range(nb):
                for j in range(N_SLABS):
                    y_ref[h * nb + b, :, pl.ds(j * LANES, LANES)] = slab[j, pl.ds(b * pitch, tb), :]


def _run_moe(x1, mod, mod_blk0, n_half, nb, tb, out_seqs, out_len, wts):
    n_tiles = x1.shape[0] // MOE_ROWS
    assert n_half * nb * tb == MOE_ROWS
    seq_per_tile = n_half * nb
    n_tt = out_len // tb
    pitch = _slab_pitch(tb)
    st = lambda q: (q // n_tt, q % n_tt)
    const2 = lambda q, g: (0, 0)
    grp3 = lambda q, g: (g, 0, 0)
    single = pl.Buffered(1)
    in_specs = [
        pl.BlockSpec((MOE_ROWS, D), lambda q, g: (q, 0)),
        pl.BlockSpec((seq_per_tile, 6 * D), lambda q, g: (mod_blk0 + st(q)[0], 0)),
        pl.BlockSpec((ROUTE_ROWS, D), const2),
        pl.BlockSpec((ROUTE_ROWS, 1), const2),
        pl.BlockSpec((MOE_ROWS, MOE_ROWS), const2, pipeline_mode=single),
        pl.BlockSpec((EPG, D, D_EXPERT), grp3),
        pl.BlockSpec((EPG, D, D_EXPERT), grp3),
        pl.BlockSpec((EPG, D_EXPERT, D), grp3),
        pl.BlockSpec((1, D), const2),
        pl.BlockSpec((1, D), const2),
    ]
    scratch = [
        pltpu.VMEM((MOE_ROWS, D), bf16),
        pltpu.VMEM((MOE_ROWS, LANES), f32),
        pltpu.VMEM((MOE_ROWS, LANES), bf16),
        pltpu.VMEM((SUBLANES, MOE_ROWS), i32),
        pltpu.SMEM((N_GROUPS,), i32),
        pltpu.VMEM((MOE_ROWS, D), f32),
        pltpu.VMEM((MOE_ROWS, D), f32),
        pltpu.VMEM((N_SLABS, nb * pitch, LANES), f32),
    ]
    body = functools.partial(_moe_body, n_half, nb, tb, pitch)
    return pl.pallas_call(
        body,
        grid=(n_tiles, N_GROUPS),
        in_specs=in_specs,
        out_specs=pl.BlockSpec((seq_per_tile, tb, D), lambda q, g: (*st(q), 0)),
        out_shape=jax.ShapeDtypeStruct((out_seqs, out_len, D), f32),
        scratch_shapes=scratch,
        compiler_params=pltpu.CompilerParams(
            dimension_semantics=("arbitrary", "arbitrary"),
            vmem_limit_bytes=VMEM_LIMIT),
        name=f"moe_nb{nb}_tb{tb}",
    )(x1, mod, *wts)


def _pair_blocks(w):
    n, k, _ = w.shape
    wp = w.reshape(n // 2, 2, k, k)
    eye = jnp.eye(2, dtype=w.dtype)
    return jnp.einsum('phij,hg->phigj', wp, eye).reshape(n // 2, 2 * k, 2 * k)


def kernel(x_prompt, x_sample, c_prompt, c_sample, state_pool, state_conv, state_lru, w_ada, b_ada, w_in, w_pool, pool_scale, w_conv, b_conv, w_a, b_a, w_x, b_x, lru_lambda, w_out, ln1_g, ln1_b, w_group, b_group, w_route, b_route, w_gate, w_up, w_down, ln2_g, ln2_b):
    l = 0
    bp, tp, _ = x_prompt.shape
    bs, ts, _ = x_sample.shape
    mod = _ada_mod(jnp.concatenate([c_sample, c_prompt], axis=0), w_ada[l], b_ada[l])

    w_rt = jnp.zeros((ROUTE_ROWS, D), f32).at[0:N_GROUPS].set(w_group[l].T)
    b_rt = jnp.zeros((ROUTE_ROWS,), f32).at[0:N_GROUPS].set(b_group[l])
    for k in range(N_GROUPS):
        r0 = SUBLANES * (k + 1)
        w_rt = w_rt.at[r0:r0 + EPG].set(w_route[l][:, k * EPG:(k + 1) * EPG].T)
        b_rt = b_rt.at[r0:r0 + EPG].set(b_route[l][k * EPG:(k + 1) * EPG])
    tri = jnp.triu(jnp.ones((MOE_ROWS, MOE_ROWS), bf16), 1)

    mix_wts = (
        w_in[l].astype(bf16),
        w_pool[l].astype(bf16),
        pool_scale[l].reshape(1, D_POOL),
        w_conv[l],
        b_conv[l].reshape(1, D_LRU),
        jnp.stack([_pair_blocks(w_a[l]), _pair_blocks(w_x[l])]).astype(bf16),
        jnp.concatenate([b_a[l], b_x[l]]).reshape(1, 2 * D_LRU),
        lru_lambda[l].reshape(1, D_LRU),
        w_out[l].astype(bf16),
        ln1_g[l].reshape(1, D),
        ln1_b[l].reshape(1, D),
    )
    moe_wts = (
        w_rt.astype(bf16),
        b_rt.reshape(ROUTE_ROWS, 1),
        tri,
        w_gate[l].astype(bf16),
        w_up[l].astype(bf16),
        w_down[l].astype(bf16),
        ln2_g[l].reshape(1, D),
        ln2_b[l].reshape(1, D),
    )

    tb_p = MOE_ROWS // bp
    zp = jnp.zeros((POOL_BUF, bp, D_POOL), f32)
    zc = jnp.zeros((CONV_W - 1, bp, D_LRU), f32)
    zh = jnp.zeros((bp, D_LRU), f32)
    x1p, pool_p, conv_p, lru_p = _run_mix(x_prompt, mod, bs // bp, zp, zc, zh, 0, bp, tb_p, mix_wts)
    yp = _run_moe(x1p, mod, bs // bp, 1, bp, tb_p, bp, tp, moe_wts)

    n_prev_s = min(PAST_LEN, POOL_BUF)
    nb_s = bs // 2
    x1s, pool_s, conv_s, lru_s = _run_mix(
        x_sample, mod, 0, state_pool[l].transpose(1, 0, 2), state_conv[l].transpose(1, 0, 2),
        state_lru[l], n_prev_s, nb_s, ts, mix_wts)
    ys = _run_moe(x1s, mod, 0, 2, nb_s, ts, bs, ts, moe_wts)

    tr = lambda a: a.transpose(1, 0, 2)[None]
    return (yp, ys, tr(pool_p), tr(conv_p), lru_p[None], tr(pool_s), tr(conv_s), lru_s[None])
```

```python
import functools
import math

import jax
import jax.numpy as jnp
from jax import lax
from jax.experimental import pallas as pl
from jax.experimental.pallas import tpu as pltpu

D = 1024
D_POOL = 512
D_LRU = 512
D_IN = D_POOL + 2 * D_LRU
POOL_WINDOWS = (2, 4, 8, 16)
POOL_GROUP = 128
POOL_BUF = 15
CONV_W = 4
LRU_C = 8.0
N_GROUPS = 4
EPG = 4
D_EXPERT = 256
DEPTH = 1
ALPHA = (2.0 * DEPTH) ** 0.25
LN_EPS = 1e-5
PAST_LEN = 16384

LANES = 128
SUBLANES = 8
N_SLABS = D // LANES
ROUTE_ROWS = 48
MOE_ROWS = 1024
CHUNK = 288
CHUNK_PAD = 320
ROW_LO = 4
ROW_LO2 = 8
ROW_GSEL = 12
ROW_RANK = 13
N_RB = 4
VMEM_LIMIT = 60 * 1024 * 1024

f32 = jnp.float32
bf16 = jnp.bfloat16
i32 = jnp.int32


def _dot(a, b):
    return jnp.dot(a, b, preferred_element_type=f32)


def _sigmoid(x):
    return 1.0 / (1.0 + jnp.exp(-x))


def _gelu_tanh(x):
    c = math.sqrt(2.0 / math.pi)
    return 0.5 * x * (1.0 + jnp.tanh(c * (x + 0.044715 * (x * x * x))))


def _layer_norm(v, g, b):
    mu = jnp.mean(v, axis=-1, keepdims=True)
    c = v - mu
    var = jnp.mean(c * c, axis=-1, keepdims=True)
    return c * lax.rsqrt(var + LN_EPS) * g + b


def _slab_pitch(tb):
    return tb + SUBLANES if (tb // SUBLANES) % 2 == 0 else tb


def _ada_body(c_ref, w_ref, b_ref, o_ref):
    o_ref[...] = _dot(c_ref[...].astype(bf16), w_ref[...].astype(bf16)) + b_ref[...]


def _ada_mod(c_all, w_ada, b_ada):
    n = c_all.shape[0]
    bn = 512
    return pl.pallas_call(
        _ada_body,
        grid=(6 * D // bn,),
        in_specs=[
            pl.BlockSpec((n, D), lambda j: (0, 0)),
            pl.BlockSpec((D, bn), lambda j: (0, j)),
            pl.BlockSpec((1, bn), lambda j: (0, j)),
        ],
        out_specs=pl.BlockSpec((n, bn), lambda j: (0, j)),
        out_shape=jax.ShapeDtypeStruct((n, 6 * D), f32),
        name="ada_mod",
    )(c_all, w_ada, b_ada.reshape(1, 6 * D))


def _mix_body(nb, tb, pitch, n_prev, n_tt,
              x_ref, mod_ref, pool0_ref, conv0_ref, h0_ref,
              w_in_ref, w_pool_ref, pscale_ref, w_conv_ref, b_conv_ref,
              w_ax_ref, b_ax_ref, lam_ref, w_out_ref, ln1g_ref, ln1b_ref,
              x1_ref, npool_ref, nconv_ref, nh_ref,
              slab, xt, ycat, zpool, zconv, a_s, b_s, gl_s, h_s):
    tm = nb * tb
    tq = tb // N_RB
    rq = tq * nb
    ti = pl.program_id(0) % n_tt

    def mod_part(k):
        return mod_ref[:, pl.ds(k * D, D)]

    @pl.when(ti == 0)
    def _init_state():
        zpool[pl.ds(0, POOL_BUF * nb), :] = pool0_ref[...].reshape(POOL_BUF * nb, D_POOL)
        zconv[pl.ds(0, (CONV_W - 1) * nb), :] = conv0_ref[...].reshape((CONV_W - 1) * nb, D_LRU)
        h_s[...] = h0_ref[...]

    for b in range(nb):
        for j in range(N_SLABS):
            slab[j, pl.ds(b * pitch, tb), :] = x_ref[b, :, pl.ds(j * LANES, LANES)]
    for t in range(tb):
        for j in range(N_SLABS):
            xt[t, :, pl.ds(j * LANES, LANES)] = slab[j, pl.ds(t, nb, stride=pitch), :]

    sh1 = mod_part(0)
    sc1 = mod_part(1)
    for r in range(N_RB):
        u = xt[pl.ds(r * tq, tq)] * (1.0 + sc1)[None] + sh1[None]
        ycat[pl.ds(r * rq, rq), :] = u.reshape(rq, D).astype(bf16)
    ub = ycat[...]
    zpool[pl.ds(POOL_BUF * nb, tm), :] = _dot(ub, w_in_ref[:, pl.ds(0, D_POOL)])
    zconv[pl.ds((CONV_W - 1) * nb, tm), :] = _dot(ub, w_in_ref[:, pl.ds(D_POOL, D_LRU)])
    half = D_LRU // 2
    for hpart in range(2):
        cols = pl.ds(D_POOL + D_LRU + hpart * half, half)
        gl_s[:, pl.ds(hpart * half, half)] = _gelu_tanh(_dot(ub, w_in_ref[:, cols]))

    row = lax.broadcasted_iota(i32, (tm, LANES), 0)
    t_loc = lax.shift_right_logical(row, int(math.log2(nb)))
    t_glob = (ti * tb + t_loc + (1 + n_prev)).astype(f32)
    for c, w in enumerate(POOL_WINDOWS):
        lanes = pl.ds(c * LANES, LANES)
        s = zpool[pl.ds((POOL_BUF + 1 - w) * nb, tm + (w - 1) * nb), lanes]
        step = 1
        while step < w:
            s = s[step * nb:] + s[:-step * nb]
            step *= 2
        cnt = jnp.minimum(f32(w), t_glob)
        dlt = s / cnt - zpool[pl.ds(POOL_BUF * nb, tm), lanes]
        yp = _dot(dlt.astype(bf16), w_pool_ref[c]) * pscale_ref[:, lanes]
        ycat[:, lanes] = yp.astype(bf16)

    nl = -lam_ref[...]
    softplus = jnp.maximum(nl, 0.0) + jnp.log(1.0 + jnp.exp(-jnp.abs(nl)))
    for j in range(D_LRU // LANES):
        lanes = pl.ds(j * LANES, LANES)
        xc = b_conv_ref[:, lanes] + zconv[pl.ds(0, tm), lanes] * w_conv_ref[0:1, lanes]
        for k in range(1, CONV_W):
            xc = xc + zconv[pl.ds(k * nb, tm), lanes] * w_conv_ref[k:k + 1, lanes]
        xcb = xc.astype(bf16)
        r = _sigmoid(_dot(xcb, w_ax_ref[0, j]) + b_ax_ref[:, lanes])
        ig = _sigmoid(_dot(xcb, w_ax_ref[1, j]) + b_ax_ref[:, pl.ds(D_LRU + j * LANES, LANES)])
        log_a = (-LRU_C) * r * softplus[:, j * LANES:(j + 1) * LANES]
        a_s[:, lanes] = jnp.exp(log_a)
        mult = jnp.sqrt(jnp.maximum(1.0 - jnp.exp(2.0 * log_a), 0.0))
        b_s[:, lanes] = mult * ig * xc

    zpool[pl.ds(0, POOL_BUF * nb), :] = zpool[pl.ds(tm, POOL_BUF * nb), :]
    zconv[pl.ds(0, (CONV_W - 1) * nb), :] = zconv[pl.ds(tm, (CONV_W - 1) * nb), :]

    h = h_s[...]
    for t in range(tb):
        rows = pl.ds(t * nb, nb)
        h = a_s[rows, :] * h + b_s[rows, :]
        b_s[rows, :] = h
    h_s[...] = h
    ycat[:, pl.ds(D_POOL, D_LRU)] = (b_s[...] * gl_s[...]).astype(bf16)

    g1 = mod_part(2)
    yc = ycat[...]
    cb = D // N_RB
    for j in range(N_RB):
        cols = pl.ds(j * cb, cb)
        mix = _dot(yc, w_out_ref[:, cols]).reshape(tb, nb, cb)
        xt[:, :, cols] = ALPHA * xt[:, :, cols] + (1.0 + g1[:, j * cb:(j + 1) * cb])[None] * mix
    for r in range(N_RB):
        xn = _layer_norm(xt[pl.ds(r * tq, tq)], ln1g_ref[...][None], ln1b_ref[...][None])
        x1_ref[pl.ds(r * rq, rq), :] = xn.reshape(rq, D)

    @pl.when(ti == n_tt - 1)
    def _emit_state():
        npool_ref[...] = zpool[pl.ds(0, POOL_BUF * nb), :].reshape(POOL_BUF, nb, D_POOL)
        nconv_ref[...] = zconv[pl.ds(0, (CONV_W - 1) * nb), :].reshape(CONV_W - 1, nb, D_LRU)
        nh_ref[...] = h_s[...]


def _run_mix(x, mod, mod_blk0, pool0, conv0, h0, n_prev, nb, tb, wts):
    bsz, t_len, _ = x.shape
    n_sb = bsz // nb
    n_tt = t_len // tb
    tm = nb * tb
    pitch = _slab_pitch(tb)
    st = lambda q: (q // n_tt, q % n_tt)
    const2 = lambda q: (0, 0)
    const3 = lambda q: (0, 0, 0)
    const4 = lambda q: (0, 0, 0, 0)
    single = pl.Buffered(1)
    in_specs = [
        pl.BlockSpec((nb, tb, D), lambda q: (*st(q), 0)),
        pl.BlockSpec((nb, 6 * D), lambda q: (mod_blk0 + st(q)[0], 0)),
        pl.BlockSpec((POOL_BUF, nb, D_POOL), lambda q: (0, st(q)[0], 0)),
        pl.BlockSpec((CONV_W - 1, nb, D_LRU), lambda q: (0, st(q)[0], 0)),
        pl.BlockSpec((nb, D_LRU), lambda q: (st(q)[0], 0)),
        pl.BlockSpec((D, D_IN), const2, pipeline_mode=single),
        pl.BlockSpec((4, POOL_GROUP, POOL_GROUP), const3),
        pl.BlockSpec((1, D_POOL), const2),
        pl.BlockSpec((CONV_W, D_LRU), const2),
        pl.BlockSpec((1, D_LRU), const2),
        pl.BlockSpec((2, D_LRU // LANES, LANES, LANES), const4),
        pl.BlockSpec((1, 2 * D_LRU), const2),
        pl.BlockSpec((1, D_LRU), const2),
        pl.BlockSpec((D, D), const2, pipeline_mode=single),
        pl.BlockSpec((1, D), const2),
        pl.BlockSpec((1, D), const2),
    ]
    out_specs = [
        pl.BlockSpec((tm, D), lambda q: (q, 0)),
        pl.BlockSpec((POOL_BUF, nb, D_POOL), lambda q: (0, st(q)[0], 0)),
        pl.BlockSpec((CONV_W - 1, nb, D_LRU), lambda q: (0, st(q)[0], 0)),
        pl.BlockSpec((nb, D_LRU), lambda q: (st(q)[0], 0)),
    ]
    out_shape = [
        jax.ShapeDtypeStruct((bsz * t_len, D), f32),
        jax.ShapeDtypeStruct((POOL_BUF, bsz, D_POOL), f32),
        jax.ShapeDtypeStruct((CONV_W - 1, bsz, D_LRU), f32),
        jax.ShapeDtypeStruct((bsz, D_LRU), f32),
    ]
    scratch = [
        pltpu.VMEM((N_SLABS, nb * pitch, LANES), f32),
        pltpu.VMEM((tb, nb, D), f32),
        pltpu.VMEM((tm, D), bf16),
        pltpu.VMEM(((tb + POOL_BUF) * nb, D_POOL), f32),
        pltpu.VMEM(((tb + CONV_W - 1) * nb, D_LRU), f32),
        pltpu.VMEM((tm, D_LRU), f32),
        pltpu.VMEM((tm, D_LRU), f32),
        pltpu.VMEM((tm, D_LRU), f32),
        pltpu.VMEM((nb, D_LRU), f32),
    ]
    body = functools.partial(_mix_body, nb, tb, pitch, n_prev, n_tt)
    return pl.pallas_call(
        body,
        grid=(n_sb * n_tt,),
        in_specs=in_specs,
        out_specs=out_specs,
        out_shape=out_shape,
        scratch_shapes=scratch,
        compiler_params=pltpu.CompilerParams(
            dimension_semantics=("arbitrary",),
            vmem_limit_bytes=VMEM_LIMIT),
        name=f"mix_nb{nb}_tb{tb}",
    )(x, mod, pool0, conv0, h0, *wts)


def _moe_body(n_half, nb, tb, pitch,
              x1_ref, mod_ref, w_rt_ref, b_rt_ref, tri_ref, wg_ref, wu_ref, wd_ref, ln2g_ref, ln2b_ref,
              y_ref,
              u2, tok_t, tok3, rows_s, cnt_s, ys_all, yacc, stage, slab):
    tm = MOE_ROWS
    hm = nb * tb
    tq = tb // N_RB
    rq = tq * nb
    g = pl.program_id(1)

    def mod_part(h, k):
        return mod_ref[pl.ds(h * nb, nb), pl.ds(k * D, D)]

    @pl.when(g == 0)
    def _route():
        for h in range(n_half):
            sh2 = mod_part(h, 3)
            sc2 = mod_part(h, 4)
            for r in range(N_RB):
                rows = pl.ds(h * hm + r * rq, rq)
                v = x1_ref[rows, :].reshape(tq, nb, D) * (1.0 + sc2)[None] + sh2[None]
                u2[rows, :] = v.reshape(rq, D).astype(bf16)
        lt = lax.dot_general(w_rt_ref[...], u2[...], (((1,), (1,)), ((), ())),
                             preferred_element_type=f32) + b_rt_ref[...]
        gl = [lt[k:k + 1, :] for k in range(N_GROUPS)]
        best = gl[0]
        gsel = jnp.zeros_like(best, dtype=i32)
        for k in range(1, N_GROUPS):
            better = gl[k] > best
            best = jnp.where(better, gl[k], best)
            gsel = jnp.where(better, k, gsel)
        denom = jnp.exp(gl[0] - best)
        for k in range(1, N_GROUPS):
            denom = denom + jnp.exp(gl[k] - best)
        p_sel = 1.0 / denom
        es = []
        for j in range(EPG):
            v_j = lt[SUBLANES + j:SUBLANES + j + 1, :]
            for k in range(1, N_GROUPS):
                r0 = SUBLANES * (k + 1) + j
                v_j = jnp.where(gsel == k, lt[r0:r0 + 1, :], v_j)
            es.append(v_j)
        v1 = es[0]
        i1 = jnp.zeros_like(gsel)
        for j in range(1, EPG):
            better = es[j] > v1
            v1 = jnp.where(better, es[j], v1)
            i1 = jnp.where(better, j, i1)
        v2 = jnp.full_like(v1, -jnp.inf)
        i2 = jnp.full_like(gsel, -1)
        for j in range(EPG):
            cand = jnp.logical_and(i1 != j, jnp.logical_or(i2 < 0, es[j] > v2))
            v2 = jnp.where(cand, es[j], v2)
            i2 = jnp.where(cand, j, i2)
        e21 = jnp.exp(v2 - v1)
        w1 = p_sel / (1.0 + e21)
        w2 = p_sel * e21 / (1.0 + e21)
        rid8 = lax.broadcasted_iota(i32, (SUBLANES, tm), 0)
        onehot = jnp.where(rid8 == gsel, 1.0, 0.0)
        pref = _dot(onehot.astype(bf16), tri_ref[...])
        rank = jnp.sum(onehot * pref, axis=0, keepdims=True)
        for k in range(N_GROUPS):
            cnt_s[k] = jnp.sum(jnp.where(gsel == k, 1.0, 0.0)).astype(i32)
        rows_s[0:1, :] = gsel
        rows_s[1:2, :] = rank.astype(i32)
        rid = lax.broadcasted_iota(i32, (LANES, tm), 0)
        rec = jnp.zeros((LANES, tm), f32)
        for j in range(EPG):
            wj = jnp.where(i1 == j, w1, jnp.where(i2 == j, w2, 0.0))
            hi = wj.astype(bf16).astype(f32)
            mid = (wj - hi).astype(bf16).astype(f32)
            lo = (wj - hi - mid).astype(bf16).astype(f32)
            rec = jnp.where(rid == j, hi, rec)
            rec = jnp.where(rid == ROW_LO + j, mid, rec)
            rec = jnp.where(rid == ROW_LO2 + j, lo, rec)
        rec = jnp.where(rid == ROW_GSEL, gsel.astype(f32), rec)
        rec = jnp.where(rid == ROW_RANK, rank, rec)
        rt = rec.T
        tok_t[...] = rt
        tok3[...] = rt.astype(bf16)

    def sorted_experts(base):
        gsel_row = rows_s[0:1, :]
        rank_row = rows_s[1:2, :]
        jj = lax.broadcasted_iota(i32, (CHUNK, tm), 0) + base
        pm = jnp.where(jnp.logical_and(gsel_row == g, rank_row == jj), 1.0, 0.0).astype(bf16)
        xs = _dot(pm, u2[...]).astype(bf16)
        cs3 = _dot(pm, tok3[...])
        cs = (cs3 + pltpu.roll(cs3, LANES - ROW_LO, axis=1)
              + pltpu.roll(cs3, LANES - ROW_LO2, axis=1))
        acts = []
        for e in range(EPG):
            hg = _dot(xs, wg_ref[e])
            hu = _dot(xs, wu_ref[e])
            acts.append((hg * _sigmoid(hg) * hu * cs[:, e:e + 1]).astype(bf16))
        hb = jnp.concatenate(acts, axis=1)
        return _dot(hb, wd_ref[...].reshape(EPG * D_EXPERT, D)).astype(bf16)

    ys_all[pl.ds(pl.multiple_of(g * CHUNK_PAD, CHUNK_PAD), CHUNK), :] = sorted_experts(0)

    @pl.when(g == 0)
    def _zero():
        yacc[...] = jnp.zeros((tm, D), f32)
        for k in range(N_GROUPS):
            ys_all[pl.ds(k * CHUNK_PAD + CHUNK, CHUNK_PAD - CHUNK), :] = jnp.zeros((CHUNK_PAD - CHUNK, D), bf16)

    n_chunks = pl.cdiv(cnt_s[g], CHUNK)

    def overflow(c, carry):
        base = c * CHUNK
        ys = sorted_experts(base)
        rec = tok_t[...]
        gsel_col = rec[:, ROW_GSEL:ROW_GSEL + 1]
        rank_col = rec[:, ROW_RANK:ROW_RANK + 1]
        jl = (lax.broadcasted_iota(i32, (tm, CHUNK), 1) + base).astype(f32)
        pt = jnp.where(jnp.logical_and(gsel_col == g.astype(f32), rank_col == jl), 1.0, 0.0).astype(bf16)
        yacc[...] += _dot(pt, ys)
        return carry

    lax.fori_loop(1, n_chunks, overflow, 0)

    @pl.when(g == N_GROUPS - 1)
    def _finish():
        rec = tok_t[...]
        gsel_col = rec[:, ROW_GSEL:ROW_GSEL + 1]
        rank_col = rec[:, ROW_RANK:ROW_RANK + 1]
        key = jnp.where(rank_col < f32(CHUNK), gsel_col * f32(CHUNK_PAD) + rank_col, -1.0)
        jl = lax.broadcasted_iota(i32, (tm, N_GROUPS * CHUNK_PAD), 1).astype(f32)
        pt = jnp.where(key == jl, 1.0, 0.0).astype(bf16)
        yacc[...] += _dot(pt, ys_all[...])
        for h in range(n_half):
            g2 = mod_part(h, 5)
            for r in range(N_RB):
                rows = pl.ds(h * hm + r * rq, rq)
                v = (ALPHA * x1_ref[rows, :] + ((1.0 + g2)[None] * yacc[rows, :].reshape(tq, nb, D)).reshape(rq, D))
                stage[rows, :] = _layer_norm(v, ln2g_ref[...], ln2b_ref[...])
            for t in range(tb):
                for j in range(N_SLABS):
                    slab[j, pl.ds(t, nb, stride=pitch), :] = stage[pl.ds(h * hm + t * nb, nb), pl.ds(j * LANES, LANES)]
            for b in range(nb):
                for j in range(N_SLABS):
                    y_ref[h * nb + b, :, pl.ds(j * LANES, LANES)] = slab[j, pl.ds(b * pitch, tb), :]


def _run_moe(x1, mod, mod_blk0, n_half, nb, tb, out_seqs, out_len, wts):
    n_tiles = x1.shape[0] // MOE_ROWS
    assert n_half * nb * tb == MOE_ROWS
    seq_per_tile = n_half * nb
    n_tt = out_len // tb
    pitch = _slab_pitch(tb)
    st = lambda q: (q // n_tt, q % n_tt)
    const2 = lambda q, g: (0, 0)
    grp3 = lambda q, g: (g, 0, 0)
    single = pl.Buffered(1)
    in_specs = [
        pl.BlockSpec((MOE_ROWS, D), lambda q, g: (q, 0)),
        pl.BlockSpec((seq_per_tile, 6 * D), lambda q, g: (mod_blk0 + st(q)[0], 0)),
        pl.BlockSpec((ROUTE_ROWS, D), const2),
        pl.BlockSpec((ROUTE_ROWS, 1), const2),
        pl.BlockSpec((MOE_ROWS, MOE_ROWS), const2, pipeline_mode=single),
        pl.BlockSpec((EPG, D, D_EXPERT), grp3),
        pl.BlockSpec((EPG, D, D_EXPERT), grp3),
        pl.BlockSpec((EPG, D_EXPERT, D), grp3),
        pl.BlockSpec((1, D), const2),
        pl.BlockSpec((1, D), const2),
    ]
    scratch = [
        pltpu.VMEM((MOE_ROWS, D), bf16),
        pltpu.VMEM((MOE_ROWS, LANES), f32),
        pltpu.VMEM((MOE_ROWS, LANES), bf16),
        pltpu.VMEM((SUBLANES, MOE_ROWS), i32),
        pltpu.SMEM((N_GROUPS,), i32),
        pltpu.VMEM((N_GROUPS * CHUNK_PAD, D), bf16),
        pltpu.VMEM((MOE_ROWS, D), f32),
        pltpu.VMEM((MOE_ROWS, D), f32),
        pltpu.VMEM((N_SLABS, nb * pitch, LANES), f32),
    ]
    body = functools.partial(_moe_body, n_half, nb, tb, pitch)
    return pl.pallas_call(
        body,
        grid=(n_tiles, N_GROUPS),
        in_specs=in_specs,
        out_specs=pl.BlockSpec((seq_per_tile, tb, D), lambda q, g: (*st(q), 0)),
        out_shape=jax.ShapeDtypeStruct((out_seqs, out_len, D), f32),
        scratch_shapes=scratch,
        compiler_params=pltpu.CompilerParams(
            dimension_semantics=("arbitrary", "arbitrary"),
            vmem_limit_bytes=VMEM_LIMIT),
        name=f"moe_nb{nb}_tb{tb}",
    )(x1, mod, *wts)


def _pair_blocks(w):
    n, k, _ = w.shape
    wp = w.reshape(n // 2, 2, k, k)
    eye = jnp.eye(2, dtype=w.dtype)
    return jnp.einsum('phij,hg->phigj', wp, eye).reshape(n // 2, 2 * k, 2 * k)


def kernel(x_prompt, x_sample, c_prompt, c_sample, state_pool, state_conv, state_lru, w_ada, b_ada, w_in, w_pool, pool_scale, w_conv, b_conv, w_a, b_a, w_x, b_x, lru_lambda, w_out, ln1_g, ln1_b, w_group, b_group, w_route, b_route, w_gate, w_up, w_down, ln2_g, ln2_b):
    l = 0
    bp, tp, _ = x_prompt.shape
    bs, ts, _ = x_sample.shape
    mod = _ada_mod(jnp.concatenate([c_sample, c_prompt], axis=0), w_ada[l], b_ada[l])

    w_rt = jnp.zeros((ROUTE_ROWS, D), f32).at[0:N_GROUPS].set(w_group[l].T)
    b_rt = jnp.zeros((ROUTE_ROWS,), f32).at[0:N_GROUPS].set(b_group[l])
    for k in range(N_GROUPS):
        r0 = SUBLANES * (k + 1)
        w_rt = w_rt.at[r0:r0 + EPG].set(w_route[l][:, k * EPG:(k + 1) * EPG].T)
        b_rt = b_rt.at[r0:r0 + EPG].set(b_route[l][k * EPG:(k + 1) * EPG])
    tri = jnp.triu(jnp.ones((MOE_ROWS, MOE_ROWS), bf16), 1)

    mix_wts = (
        w_in[l].astype(bf16),
        w_pool[l].astype(bf16),
        pool_scale[l].reshape(1, D_POOL),
        w_conv[l],
        b_conv[l].reshape(1, D_LRU),
        jnp.stack([_pair_blocks(w_a[l]), _pair_blocks(w_x[l])]).astype(bf16),
        jnp.concatenate([b_a[l], b_x[l]]).reshape(1, 2 * D_LRU),
        lru_lambda[l].reshape(1, D_LRU),
        w_out[l].astype(bf16),
        ln1_g[l].reshape(1, D),
        ln1_b[l].reshape(1, D),
    )
    moe_wts = (
        w_rt.astype(bf16),
        b_rt.reshape(ROUTE_ROWS, 1),
        tri,
        w_gate[l].astype(bf16),
        w_up[l].astype(bf16),
        w_down[l].astype(bf16),
        ln2_g[l].reshape(1, D),
        ln2_b[l].reshape(1, D),
    )

    tb_p = MOE_ROWS // bp
    zp = jnp.zeros((POOL_BUF, bp, D_POOL), f32)
    zc = jnp.zeros((CONV_W - 1, bp, D_LRU), f32)
    zh = jnp.zeros((bp, D_LRU), f32)
    x1p, pool_p, conv_p, lru_p = _run_mix(x_prompt, mod, bs // bp, zp, zc, zh, 0, bp, tb_p, mix_wts)
    yp = _run_moe(x1p, mod, bs // bp, 1, bp, tb_p, bp, tp, moe_wts)

    n_prev_s = min(PAST_LEN, POOL_BUF)
    nb_s = bs // 2
    x1s, pool_s, conv_s, lru_s = _run_mix(
        x_sample, mod, 0, state_pool[l].transpose(1, 0, 2), state_conv[l].transpose(1, 0, 2),
        state_lru[l], n_prev_s, nb_s, ts, mix_wts)
    ys = _run_moe(x1s, mod, 0, 2, nb_s, ts, bs, ts, moe_wts)

    tr = lambda a: a.transpose(1, 0, 2)[None]
    return (yp, ys, tr(pool_p), tr(conv_p), lru_p[None], tr(pool_s), tr(conv_s), lru_s[None])
```

```python
import functools
import math

import jax
import jax.numpy as jnp
from jax import lax
from jax.experimental import pallas as pl
from jax.experimental.pallas import tpu as pltpu

D = 1024
D_POOL = 512
D_LRU = 512
D_IN = D_POOL + 2 * D_LRU
POOL_WINDOWS = (2, 4, 8, 16)
POOL_GROUP = 128
POOL_BUF = 15
CONV_W = 4
LRU_C = 8.0
N_GROUPS = 4
EPG = 4
D_EXPERT = 256
DEPTH = 1
ALPHA = (2.0 * DEPTH) ** 0.25
LN_EPS = 1e-5
PAST_LEN = 16384

LANES = 128
SUBLANES = 8
N_SLABS = D // LANES
ROUTE_ROWS = 48
MOE_ROWS = 1024
CHUNK = 288
CHUNK_PAD = 320
ROW_LO = 4
ROW_LO2 = 8
ROW_GSEL = 12
ROW_RANK = 13
N_RB = 4
VMEM_LIMIT = 60 * 1024 * 1024

f32 = jnp.float32
bf16 = jnp.bfloat16
i32 = jnp.int32


def _dot(a, b):
    return jnp.dot(a, b, preferred_element_type=f32)


LOG2E = 1.0 / math.log(2.0)


def _sigmoid(x):
    return 1.0 / (1.0 + jnp.exp2(x * (-LOG2E)))


def _gelu_tanh(x):
    c = math.sqrt(2.0 / math.pi)
    hx = 0.5 * x
    return hx + hx * jnp.tanh(x * (c + (c * 0.044715) * (x * x)))


def _layer_norm(v, g, b):
    mu = jnp.mean(v, axis=-1, keepdims=True)
    c = v - mu
    var = jnp.mean(c * c, axis=-1, keepdims=True)
    return c * lax.rsqrt(var + LN_EPS) * g + b


def _slab_pitch(tb):
    return tb + SUBLANES if (tb // SUBLANES) % 2 == 0 else tb


def _ada_body(c_ref, w_ref, b_ref, o_ref):
    o_ref[...] = _dot(c_ref[...].astype(bf16), w_ref[...].astype(bf16)) + b_ref[...]


def _ada_mod(c_all, w_ada, b_ada):
    n = c_all.shape[0]
    bn = 1024
    return pl.pallas_call(
        _ada_body,
        grid=(6 * D // bn,),
        in_specs=[
            pl.BlockSpec((n, D), lambda j: (0, 0)),
            pl.BlockSpec((D, bn), lambda j: (0, j)),
            pl.BlockSpec((1, bn), lambda j: (0, j)),
        ],
        out_specs=pl.BlockSpec((n, bn), lambda j: (0, j)),
        out_shape=jax.ShapeDtypeStruct((n, 6 * D), f32),
        name="ada_mod",
    )(c_all, w_ada, b_ada.reshape(1, 6 * D))


def _mix_body(nb, tb, pitch, n_prev, n_tt,
              x_ref, mod_ref, pool0_ref, conv0_ref, h0_ref,
              w_in_ref, w_pool_ref, pscale_ref, w_conv_ref, b_conv_ref,
              w_ax_ref, b_ax_ref, lam_ref, w_out_ref, ln1g_ref, ln1b_ref,
              x1_ref, npool_ref, nconv_ref, nh_ref,
              slab, xt, ycat, zpool, zconv, a_s, b_s, gl_s, h_s):
    tm = nb * tb
    tq = tb // N_RB
    rq = tq * nb
    ti = pl.program_id(0) % n_tt

    def mod_part(k):
        return mod_ref[:, pl.ds(k * D, D)]

    @pl.when(ti == 0)
    def _init_state():
        zpool[pl.ds(0, POOL_BUF * nb), :] = pool0_ref[...].reshape(POOL_BUF * nb, D_POOL)
        zconv[pl.ds(0, (CONV_W - 1) * nb), :] = conv0_ref[...].reshape((CONV_W - 1) * nb, D_LRU)
        h_s[...] = h0_ref[...]

    for b in range(nb):
        for j in range(N_SLABS):
            slab[j, pl.ds(b * pitch, tb), :] = x_ref[b, :, pl.ds(j * LANES, LANES)]
    for t in range(tb):
        for j in range(N_SLABS):
            xt[t, :, pl.ds(j * LANES, LANES)] = slab[j, pl.ds(t, nb, stride=pitch), :]

    sh1 = mod_part(0)
    sc1 = mod_part(1)
    for r in range(N_RB):
        u = xt[pl.ds(r * tq, tq)] * (1.0 + sc1)[None] + sh1[None]
        ycat[pl.ds(r * rq, rq), :] = u.reshape(rq, D).astype(bf16)
    ub = ycat[...]
    zpool[pl.ds(POOL_BUF * nb, tm), :] = _dot(ub, w_in_ref[:, pl.ds(0, D_POOL)])
    zconv[pl.ds((CONV_W - 1) * nb, tm), :] = _dot(ub, w_in_ref[:, pl.ds(D_POOL, D_LRU)])
    half = D_LRU // 2
    for hpart in range(2):
        cols = pl.ds(D_POOL + D_LRU + hpart * half, half)
        gl_s[:, pl.ds(hpart * half, half)] = _gelu_tanh(_dot(ub, w_in_ref[:, cols]))

    row = lax.broadcasted_iota(i32, (tm, LANES), 0)
    t_loc = lax.shift_right_logical(row, int(math.log2(nb)))
    t_glob = (ti * tb + t_loc + (1 + n_prev)).astype(f32)
    for c, w in enumerate(POOL_WINDOWS):
        lanes = pl.ds(c * LANES, LANES)
        s = zpool[pl.ds((POOL_BUF + 1 - w) * nb, tm + (w - 1) * nb), lanes]
        step = 1
        while step < w:
            s = s[step * nb:] + s[:-step * nb]
            step *= 2
        cnt = jnp.minimum(f32(w), t_glob)
        dlt = s / cnt - zpool[pl.ds(POOL_BUF * nb, tm), lanes]
        yp = _dot(dlt.astype(bf16), w_pool_ref[c]) * pscale_ref[:, lanes]
        ycat[:, lanes] = yp.astype(bf16)

    nl = -lam_ref[...]
    softplus = jnp.maximum(nl, 0.0) + jnp.log(1.0 + jnp.exp(-jnp.abs(nl)))
    log2a_unit = (-LRU_C * LOG2E) * softplus
    for j in range(D_LRU // LANES):
        lanes = pl.ds(j * LANES, LANES)
        xc = b_conv_ref[:, lanes] + zconv[pl.ds(0, tm), lanes] * w_conv_ref[0:1, lanes]
        for k in range(1, CONV_W):
            xc = xc + zconv[pl.ds(k * nb, tm), lanes] * w_conv_ref[k:k + 1, lanes]
        xcb = xc.astype(bf16)
        r = _sigmoid(_dot(xcb, w_ax_ref[0, j]) + b_ax_ref[:, lanes])
        ig = _sigmoid(_dot(xcb, w_ax_ref[1, j]) + b_ax_ref[:, pl.ds(D_LRU + j * LANES, LANES)])
        a = jnp.exp2(r * log2a_unit[:, j * LANES:(j + 1) * LANES])
        a_s[:, lanes] = a
        om = jnp.maximum(1.0 - a * a, 0.0)
        mult = jnp.where(om > 0.0, om * lax.rsqrt(om), 0.0)
        b_s[:, lanes] = mult * ig * xc

    zpool[pl.ds(0, POOL_BUF * nb), :] = zpool[pl.ds(tm, POOL_BUF * nb), :]
    zconv[pl.ds(0, (CONV_W - 1) * nb), :] = zconv[pl.ds(tm, (CONV_W - 1) * nb), :]

    h = h_s[...]
    for t in range(tb):
        rows = pl.ds(t * nb, nb)
        h = a_s[rows, :] * h + b_s[rows, :]
        b_s[rows, :] = h
    h_s[...] = h
    ycat[:, pl.ds(D_POOL, D_LRU)] = (b_s[...] * gl_s[...]).astype(bf16)

    g1 = mod_part(2)
    yc = ycat[...]
    cb = D // N_RB
    for j in range(N_RB):
        cols = pl.ds(j * cb, cb)
        mix = _dot(yc, w_out_ref[:, cols]).reshape(tb, nb, cb)
        xt[:, :, cols] = ALPHA * xt[:, :, cols] + (1.0 + g1[:, j * cb:(j + 1) * cb])[None] * mix
    for r in range(N_RB):
        xn = _layer_norm(xt[pl.ds(r * tq, tq)], ln1g_ref[...][None], ln1b_ref[...][None])
        x1_ref[pl.ds(r * rq, rq), :] = xn.reshape(rq, D)

    @pl.when(ti == n_tt - 1)
    def _emit_state():
        npool_ref[...] = zpool[pl.ds(0, POOL_BUF * nb), :].reshape(POOL_BUF, nb, D_POOL)
        nconv_ref[...] = zconv[pl.ds(0, (CONV_W - 1) * nb), :].reshape(CONV_W - 1, nb, D_LRU)
        nh_ref[...] = h_s[...]


def _run_mix(x, mod, mod_blk0, pool0, conv0, h0, n_prev, nb, tb, wts):
    bsz, t_len, _ = x.shape
    n_sb = bsz // nb
    n_tt = t_len // tb
    tm = nb * tb
    pitch = _slab_pitch(tb)
    st = lambda q: (q // n_tt, q % n_tt)
    const2 = lambda q: (0, 0)
    const3 = lambda q: (0, 0, 0)
    const4 = lambda q: (0, 0, 0, 0)
    single = pl.Buffered(1)
    in_specs = [
        pl.BlockSpec((nb, tb, D), lambda q: (*st(q), 0)),
        pl.BlockSpec((nb, 6 * D), lambda q: (mod_blk0 + st(q)[0], 0)),
        pl.BlockSpec((POOL_BUF, nb, D_POOL), lambda q: (0, st(q)[0], 0)),
        pl.BlockSpec((CONV_W - 1, nb, D_LRU), lambda q: (0, st(q)[0], 0)),
        pl.BlockSpec((nb, D_LRU), lambda q: (st(q)[0], 0)),
        pl.BlockSpec((D, D_IN), const2, pipeline_mode=single),
        pl.BlockSpec((4, POOL_GROUP, POOL_GROUP), const3),
        pl.BlockSpec((1, D_POOL), const2),
        pl.BlockSpec((CONV_W, D_LRU), const2),
        pl.BlockSpec((1, D_LRU), const2),
        pl.BlockSpec((2, D_LRU // LANES, LANES, LANES), const4),
        pl.BlockSpec((1, 2 * D_LRU), const2),
        pl.BlockSpec((1, D_LRU), const2),
        pl.BlockSpec((D, D), const2, pipeline_mode=single),
        pl.BlockSpec((1, D), const2),
        pl.BlockSpec((1, D), const2),
    ]
    out_specs = [
        pl.BlockSpec((tm, D), lambda q: (q, 0)),
        pl.BlockSpec((POOL_BUF, nb, D_POOL), lambda q: (0, st(q)[0], 0)),
        pl.BlockSpec((CONV_W - 1, nb, D_LRU), lambda q: (0, st(q)[0], 0)),
        pl.BlockSpec((nb, D_LRU), lambda q: (st(q)[0], 0)),
    ]
    out_shape = [
        jax.ShapeDtypeStruct((bsz * t_len, D), f32),
        jax.ShapeDtypeStruct((POOL_BUF, bsz, D_POOL), f32),
        jax.ShapeDtypeStruct((CONV_W - 1, bsz, D_LRU), f32),
        jax.ShapeDtypeStruct((bsz, D_LRU), f32),
    ]
    scratch = [
        pltpu.VMEM((N_SLABS, nb * pitch, LANES), f32),
        pltpu.VMEM((tb, nb, D), f32),
        pltpu.VMEM((tm, D), bf16),
        pltpu.VMEM(((tb + POOL_BUF) * nb, D_POOL), f32),
        pltpu.VMEM(((tb + CONV_W - 1) * nb, D_LRU), f32),
        pltpu.VMEM((tm, D_LRU), f32),
        pltpu.VMEM((tm, D_LRU), f32),
        pltpu.VMEM((tm, D_LRU), f32),
        pltpu.VMEM((nb, D_LRU), f32),
    ]
    body = functools.partial(_mix_body, nb, tb, pitch, n_prev, n_tt)
    return pl.pallas_call(
        body,
        grid=(n_sb * n_tt,),
        in_specs=in_specs,
        out_specs=out_specs,
        out_shape=out_shape,
        scratch_shapes=scratch,
        compiler_params=pltpu.CompilerParams(
            dimension_semantics=("arbitrary",),
            vmem_limit_bytes=VMEM_LIMIT),
        name=f"mix_nb{nb}_tb{tb}",
    )(x, mod, pool0, conv0, h0, *wts)


def _moe_body(n_half, nb, tb, pitch,
              x1_ref, mod_ref, w_rt_ref, b_rt_ref, tri_ref, wg_ref, wu_ref, wd_ref, ln2g_ref, ln2b_ref,
              y_ref,
              u2, tok_t, tok3, rows_s, cnt_s, ys_all, yacc, slab):
    tm = MOE_ROWS
    hm = nb * tb
    tq = tb // N_RB
    rq = tq * nb
    g = pl.program_id(1)

    def mod_part(h, k):
        return mod_ref[pl.ds(h * nb, nb), pl.ds(k * D, D)]

    @pl.when(g == 0)
    def _route():
        for h in range(n_half):
            sh2 = mod_part(h, 3)
            sc2 = mod_part(h, 4)
            for r in range(N_RB):
                rows = pl.ds(h * hm + r * rq, rq)
                v = x1_ref[rows, :].reshape(tq, nb, D) * (1.0 + sc2)[None] + sh2[None]
                u2[rows, :] = v.reshape(rq, D).astype(bf16)
        lt = lax.dot_general(w_rt_ref[...], u2[...], (((1,), (1,)), ((), ())),
                             preferred_element_type=f32) + b_rt_ref[...]
        gl = [lt[k:k + 1, :] for k in range(N_GROUPS)]
        best = gl[0]
        gsel = jnp.zeros_like(best, dtype=i32)
        for k in range(1, N_GROUPS):
            better = gl[k] > best
            best = jnp.where(better, gl[k], best)
            gsel = jnp.where(better, k, gsel)
        denom = jnp.exp(gl[0] - best)
        for k in range(1, N_GROUPS):
            denom = denom + jnp.exp(gl[k] - best)
        p_sel = 1.0 / denom
        es = []
        for j in range(EPG):
            v_j = lt[SUBLANES + j:SUBLANES + j + 1, :]
            for k in range(1, N_GROUPS):
                r0 = SUBLANES * (k + 1) + j
                v_j = jnp.where(gsel == k, lt[r0:r0 + 1, :], v_j)
            es.append(v_j)
        v1 = es[0]
        i1 = jnp.zeros_like(gsel)
        for j in range(1, EPG):
            better = es[j] > v1
            v1 = jnp.where(better, es[j], v1)
            i1 = jnp.where(better, j, i1)
        v2 = jnp.full_like(v1, -jnp.inf)
        i2 = jnp.full_like(gsel, -1)
        for j in range(EPG):
            cand = jnp.logical_and(i1 != j, jnp.logical_or(i2 < 0, es[j] > v2))
            v2 = jnp.where(cand, es[j], v2)
            i2 = jnp.where(cand, j, i2)
        e21 = jnp.exp(v2 - v1)
        w1 = p_sel / (1.0 + e21)
        w2 = p_sel * e21 / (1.0 + e21)
        rid8 = lax.broadcasted_iota(i32, (SUBLANES, tm), 0)
        onehot = jnp.where(rid8 == gsel, 1.0, 0.0)
        pref = _dot(onehot.astype(bf16), tri_ref[...])
        rank = jnp.sum(onehot * pref, axis=0, keepdims=True)
        for k in range(N_GROUPS):
            cnt_s[k] = jnp.sum(jnp.where(gsel == k, 1.0, 0.0)).astype(i32)
        rows_s[0:1, :] = gsel
        rows_s[1:2, :] = rank.astype(i32)
        rid = lax.broadcasted_iota(i32, (LANES, tm), 0)
        rec = jnp.zeros((LANES, tm), f32)
        for j in range(EPG):
            wj = jnp.where(i1 == j, w1, jnp.where(i2 == j, w2, 0.0))
            hi = wj.astype(bf16).astype(f32)
            mid = (wj - hi).astype(bf16).astype(f32)
            lo = (wj - hi - mid).astype(bf16).astype(f32)
            rec = jnp.where(rid == j, hi, rec)
            rec = jnp.where(rid == ROW_LO + j, mid, rec)
            rec = jnp.where(rid == ROW_LO2 + j, lo, rec)
        rec = jnp.where(rid == ROW_GSEL, gsel.astype(f32), rec)
        rec = jnp.where(rid == ROW_RANK, rank, rec)
        rt = rec.T
        tok_t[...] = rt
        tok3[...] = rt.astype(bf16)

    def sorted_experts(base):
        gsel_row = rows_s[0:1, :]
        rank_row = rows_s[1:2, :]
        jj = lax.broadcasted_iota(i32, (CHUNK, tm), 0) + base
        pm = jnp.where(jnp.logical_and(gsel_row == g, rank_row == jj), 1.0, 0.0).astype(bf16)
        xs = _dot(pm, u2[...]).astype(bf16)
        cs3 = _dot(pm, tok3[...])
        cs = (cs3 + pltpu.roll(cs3, LANES - ROW_LO, axis=1)
              + pltpu.roll(cs3, LANES - ROW_LO2, axis=1))
        acts = []
        for e in range(EPG):
            hg = _dot(xs, wg_ref[e])
            hu = _dot(xs, wu_ref[e])
            acts.append((hg * _sigmoid(hg) * hu * cs[:, e:e + 1]).astype(bf16))
        hb = jnp.concatenate(acts, axis=1)
        return _dot(hb, wd_ref[...].reshape(EPG * D_EXPERT, D)).astype(bf16)

    ys_all[pl.ds(pl.multiple_of(g * CHUNK_PAD, CHUNK_PAD), CHUNK), :] = sorted_experts(0)

    @pl.when(g == 0)
    def _zero():
        yacc[...] = jnp.zeros((tm, D), f32)
        for k in range(N_GROUPS):
            ys_all[pl.ds(k * CHUNK_PAD + CHUNK, CHUNK_PAD - CHUNK), :] = jnp.zeros((CHUNK_PAD - CHUNK, D), bf16)

    n_chunks = pl.cdiv(cnt_s[g], CHUNK)

    def overflow(c, carry):
        base = c * CHUNK
        ys = sorted_experts(base)
        rec = tok_t[...]
        gsel_col = rec[:, ROW_GSEL:ROW_GSEL + 1]
        rank_col = rec[:, ROW_RANK:ROW_RANK + 1]
        jl = (lax.broadcasted_iota(i32, (tm, CHUNK), 1) + base).astype(f32)
        pt = jnp.where(jnp.logical_and(gsel_col == g.astype(f32), rank_col == jl), 1.0, 0.0).astype(bf16)
        yacc[...] += _dot(pt, ys)
        return carry

    lax.fori_loop(1, n_chunks, overflow, 0)

    @pl.when(g == N_GROUPS - 1)
    def _finish():
        rec = tok_t[...]
        gsel_col = rec[:, ROW_GSEL:ROW_GSEL + 1]
        rank_col = rec[:, ROW_RANK:ROW_RANK + 1]
        key = jnp.where(rank_col < f32(CHUNK), gsel_col * f32(CHUNK_PAD) + rank_col, -1.0)
        jl = lax.broadcasted_iota(i32, (tm, N_GROUPS * CHUNK_PAD), 1).astype(f32)
        pt = jnp.where(key == jl, 1.0, 0.0).astype(bf16)
        yacc[...] += _dot(pt, ys_all[...])
        for h in range(n_half):
            g2 = mod_part(h, 5)
            for r in range(N_RB):
                rows = pl.ds(h * hm + r * rq, rq)
                v = (ALPHA * x1_ref[rows, :] + ((1.0 + g2)[None] * yacc[rows, :].reshape(tq, nb, D)).reshape(rq, D))
                yn = _layer_norm(v, ln2g_ref[...], ln2b_ref[...])
                for tl in range(tq):
                    for j in range(N_SLABS):
                        slab[j, pl.ds(r * tq + tl, nb, stride=pitch), :] = (
                            yn[tl * nb:(tl + 1) * nb, j * LANES:(j + 1) * LANES])
            for b in range(nb):
                for j in range(N_SLABS):
                    y_ref[h * nb + b, :, pl.ds(j * LANES, LANES)] = slab[j, pl.ds(b * pitch, tb), :]


def _run_moe(x1, mod, mod_blk0, n_half, nb, tb, out_seqs, out_len, wts):
    n_tiles = x1.shape[0] // MOE_ROWS
    assert n_half * nb * tb == MOE_ROWS
    seq_per_tile = n_half * nb
    n_tt = out_len // tb
    pitch = _slab_pitch(tb)
    st = lambda q: (q // n_tt, q % n_tt)
    const2 = lambda q, g: (0, 0)
    grp3 = lambda q, g: (g, 0, 0)
    single = pl.Buffered(1)
    in_specs = [
        pl.BlockSpec((MOE_ROWS, D), lambda q, g: (q, 0)),
        pl.BlockSpec((seq_per_tile, 6 * D), lambda q, g: (mod_blk0 + st(q)[0], 0)),
        pl.BlockSpec((ROUTE_ROWS, D), const2),
        pl.BlockSpec((ROUTE_ROWS, 1), const2),
        pl.BlockSpec((MOE_ROWS, MOE_ROWS), const2, pipeline_mode=single),
        pl.BlockSpec((EPG, D, D_EXPERT), grp3),
        pl.BlockSpec((EPG, D, D_EXPERT), grp3),
        pl.BlockSpec((EPG, D_EXPERT, D), grp3),
        pl.BlockSpec((1, D), const2),
        pl.BlockSpec((1, D), const2),
    ]
    scratch = [
        pltpu.VMEM((MOE_ROWS, D), bf16),
        pltpu.VMEM((MOE_ROWS, LANES), f32),
        pltpu.VMEM((MOE_ROWS, LANES), bf16),
        pltpu.VMEM((SUBLANES, MOE_ROWS), i32),
        pltpu.SMEM((N_GROUPS,), i32),
        pltpu.VMEM((N_GROUPS * CHUNK_PAD, D), bf16),
        pltpu.VMEM((MOE_ROWS, D), f32),
        pltpu.VMEM((N_SLABS, nb * pitch, LANES), f32),
    ]
    body = functools.partial(_moe_body, n_half, nb, tb, pitch)
    return pl.pallas_call(
        body,
        grid=(n_tiles, N_GROUPS),
        in_specs=in_specs,
        out_specs=pl.BlockSpec((seq_per_tile, tb, D), lambda q, g: (*st(q), 0)),
        out_shape=jax.ShapeDtypeStruct((out_seqs, out_len, D), f32),
        scratch_shapes=scratch,
        compiler_params=pltpu.CompilerParams(
            dimension_semantics=("arbitrary", "arbitrary"),
            vmem_limit_bytes=VMEM_LIMIT),
        name=f"moe_nb{nb}_tb{tb}",
    )(x1, mod, *wts)


def _pair_blocks(w):
    n, k, _ = w.shape
    wp = w.reshape(n // 2, 2, k, k)
    eye = jnp.eye(2, dtype=w.dtype)
    return jnp.einsum('phij,hg->phigj', wp, eye).reshape(n // 2, 2 * k, 2 * k)


def kernel(x_prompt, x_sample, c_prompt, c_sample, state_pool, state_conv, state_lru, w_ada, b_ada, w_in, w_pool, pool_scale, w_conv, b_conv, w_a, b_a, w_x, b_x, lru_lambda, w_out, ln1_g, ln1_b, w_group, b_group, w_route, b_route, w_gate, w_up, w_down, ln2_g, ln2_b):
    l = 0
    bp, tp, _ = x_prompt.shape
    bs, ts, _ = x_sample.shape
    mod = _ada_mod(jnp.concatenate([c_sample, c_prompt], axis=0), w_ada[l], b_ada[l])

    w_rt = jnp.zeros((ROUTE_ROWS, D), f32).at[0:N_GROUPS].set(w_group[l].T)
    b_rt = jnp.zeros((ROUTE_ROWS,), f32).at[0:N_GROUPS].set(b_group[l])
    for k in range(N_GROUPS):
        r0 = SUBLANES * (k + 1)
        w_rt = w_rt.at[r0:r0 + EPG].set(w_route[l][:, k * EPG:(k + 1) * EPG].T)
        b_rt = b_rt.at[r0:r0 + EPG].set(b_route[l][k * EPG:(k + 1) * EPG])
    tri = jnp.triu(jnp.ones((MOE_ROWS, MOE_ROWS), bf16), 1)

    mix_wts = (
        w_in[l].astype(bf16),
        w_pool[l].astype(bf16),
        pool_scale[l].reshape(1, D_POOL),
        w_conv[l],
        b_conv[l].reshape(1, D_LRU),
        jnp.stack([_pair_blocks(w_a[l]), _pair_blocks(w_x[l])]).astype(bf16),
        jnp.concatenate([b_a[l], b_x[l]]).reshape(1, 2 * D_LRU),
        lru_lambda[l].reshape(1, D_LRU),
        w_out[l].astype(bf16),
        ln1_g[l].reshape(1, D),
        ln1_b[l].reshape(1, D),
    )
    moe_wts = (
        w_rt.astype(bf16),
        b_rt.reshape(ROUTE_ROWS, 1),
        tri,
        w_gate[l].astype(bf16),
        w_up[l].astype(bf16),
        w_down[l].astype(bf16),
        ln2_g[l].reshape(1, D),
        ln2_b[l].reshape(1, D),
    )

    tb_p = MOE_ROWS // bp
    zp = jnp.zeros((POOL_BUF, bp, D_POOL), f32)
    zc = jnp.zeros((CONV_W - 1, bp, D_LRU), f32)
    zh = jnp.zeros((bp, D_LRU), f32)
    x1p, pool_p, conv_p, lru_p = _run_mix(x_prompt, mod, bs // bp, zp, zc, zh, 0, bp, tb_p, mix_wts)
    yp = _run_moe(x1p, mod, bs // bp, 1, bp, tb_p, bp, tp, moe_wts)

    n_prev_s = min(PAST_LEN, POOL_BUF)
    nb_s = bs // 2
    x1s, pool_s, conv_s, lru_s = _run_mix(
        x_sample, mod, 0, state_pool[l].transpose(1, 0, 2), state_conv[l].transpose(1, 0, 2),
        state_lru[l], n_prev_s, nb_s, ts, mix_wts)
    ys = _run_moe(x1s, mod, 0, 2, nb_s, ts, bs, ts, moe_wts)

    tr = lambda a: a.transpose(1, 0, 2)[None]
    return (yp, ys, tr(pool_p), tr(conv_p), lru_p[None], tr(pool_s), tr(conv_s), lru_s[None])
```

```python
import functools
import math

import jax
import jax.numpy as jnp
from jax import lax
from jax.experimental import pallas as pl
from jax.experimental.pallas import tpu as pltpu

D = 1024
D_POOL = 512
D_LRU = 512
D_IN = D_POOL + 2 * D_LRU
POOL_WINDOWS = (2, 4, 8, 16)
POOL_GROUP = 128
POOL_BUF = 15
CONV_W = 4
LRU_C = 8.0
N_GROUPS = 4
EPG = 4
D_EXPERT = 256
DEPTH = 1
ALPHA = (2.0 * DEPTH) ** 0.25
LN_EPS = 1e-5
PAST_LEN = 16384

LANES = 128
SUBLANES = 8
N_SLABS = D // LANES
ROUTE_ROWS = 48
MOE_ROWS = 1024
CHUNK = 288
CHUNK_PAD = 320
ROW_LO = 4
ROW_LO2 = 8
ROW_GSEL = 12
ROW_RANK = 13
N_RB = 4
FIN_ROWS = 256
VMEM_LIMIT = 60 * 1024 * 1024

f32 = jnp.float32
bf16 = jnp.bfloat16
i32 = jnp.int32


def _dot(a, b):
    return jnp.dot(a, b, preferred_element_type=f32)


LOG2E = 1.0 / math.log(2.0)


def _sigmoid(x):
    return 1.0 / (1.0 + jnp.exp2(x * (-LOG2E)))


def _gelu_tanh(x):
    c = math.sqrt(2.0 / math.pi)
    hx = 0.5 * x
    return hx + hx * jnp.tanh(x * (c + (c * 0.044715) * (x * x)))


def _layer_norm(v, g, b):
    mu = jnp.mean(v, axis=-1, keepdims=True)
    c = v - mu
    var = jnp.mean(c * c, axis=-1, keepdims=True)
    return c * lax.rsqrt(var + LN_EPS) * g + b


def _slab_pitch(tb):
    return tb + SUBLANES if (tb // SUBLANES) % 2 == 0 else tb


def _ada_body(c_ref, w_ref, b_ref, o_ref):
    o_ref[...] = _dot(c_ref[...].astype(bf16), w_ref[...].astype(bf16)) + b_ref[...]


def _ada_mod(c_all, w_ada, b_ada):
    n = c_all.shape[0]
    bn = 1024
    return pl.pallas_call(
        _ada_body,
        grid=(6 * D // bn,),
        in_specs=[
            pl.BlockSpec((n, D), lambda j: (0, 0)),
            pl.BlockSpec((D, bn), lambda j: (0, j)),
            pl.BlockSpec((1, bn), lambda j: (0, j)),
        ],
        out_specs=pl.BlockSpec((n, bn), lambda j: (0, j)),
        out_shape=jax.ShapeDtypeStruct((n, 6 * D), f32),
        name="ada_mod",
    )(c_all, w_ada, b_ada.reshape(1, 6 * D))


def _mix_body(nb, tb, pitch, n_prev, n_tt,
              x_ref, mod_ref, pool0_ref, conv0_ref, h0_ref,
              w_in_ref, w_pool_ref, pscale_ref, w_conv_ref, b_conv_ref,
              w_ax_ref, b_ax_ref, lam_ref, w_out_ref, ln1g_ref, ln1b_ref,
              x1_ref, npool_ref, nconv_ref, nh_ref,
              slab, xt, ycat, zpool, zconv, a_s, b_s, gl_s, h_s):
    tm = nb * tb
    tq = tb // N_RB
    rq = tq * nb
    ti = pl.program_id(0) % n_tt

    def mod_part(k):
        return mod_ref[:, pl.ds(k * D, D)]

    @pl.when(ti == 0)
    def _init_state():
        zpool[pl.ds(0, POOL_BUF * nb), :] = pool0_ref[...].reshape(POOL_BUF * nb, D_POOL)
        zconv[pl.ds(0, (CONV_W - 1) * nb), :] = conv0_ref[...].reshape((CONV_W - 1) * nb, D_LRU)
        h_s[...] = h0_ref[...]

    for b in range(nb):
        for j in range(N_SLABS):
            slab[j, pl.ds(b * pitch, tb), :] = x_ref[b, :, pl.ds(j * LANES, LANES)]
    for t in range(tb):
        for j in range(N_SLABS):
            xt[t, :, pl.ds(j * LANES, LANES)] = slab[j, pl.ds(t, nb, stride=pitch), :]

    sh1 = mod_part(0)
    sc1 = mod_part(1)
    for r in range(N_RB):
        u = xt[pl.ds(r * tq, tq)] * (1.0 + sc1)[None] + sh1[None]
        ycat[pl.ds(r * rq, rq), :] = u.reshape(rq, D).astype(bf16)
    ub = ycat[...]
    zpool[pl.ds(POOL_BUF * nb, tm), :] = _dot(ub, w_in_ref[:, pl.ds(0, D_POOL)])
    zconv[pl.ds((CONV_W - 1) * nb, tm), :] = _dot(ub, w_in_ref[:, pl.ds(D_POOL, D_LRU)])
    half = D_LRU // 2
    for hpart in range(2):
        cols = pl.ds(D_POOL + D_LRU + hpart * half, half)
        gl_s[:, pl.ds(hpart * half, half)] = _gelu_tanh(_dot(ub, w_in_ref[:, cols]))

    row = lax.broadcasted_iota(i32, (tm, LANES), 0)
    t_loc = lax.shift_right_logical(row, int(math.log2(nb)))
    t_glob = (ti * tb + t_loc + (1 + n_prev)).astype(f32)
    for c, w in enumerate(POOL_WINDOWS):
        lanes = pl.ds(c * LANES, LANES)
        s = zpool[pl.ds((POOL_BUF + 1 - w) * nb, tm + (w - 1) * nb), lanes]
        step = 1
        while step < w:
            s = s[step * nb:] + s[:-step * nb]
            step *= 2
        cnt = jnp.minimum(f32(w), t_glob)
        dlt = s / cnt - zpool[pl.ds(POOL_BUF * nb, tm), lanes]
        yp = _dot(dlt.astype(bf16), w_pool_ref[c]) * pscale_ref[:, lanes]
        ycat[:, lanes] = yp.astype(bf16)

    nl = -lam_ref[...]
    softplus = jnp.maximum(nl, 0.0) + jnp.log(1.0 + jnp.exp(-jnp.abs(nl)))
    log2a_unit = (-LRU_C * LOG2E) * softplus
    for j in range(D_LRU // LANES):
        lanes = pl.ds(j * LANES, LANES)
        xc = b_conv_ref[:, lanes] + zconv[pl.ds(0, tm), lanes] * w_conv_ref[0:1, lanes]
        for k in range(1, CONV_W):
            xc = xc + zconv[pl.ds(k * nb, tm), lanes] * w_conv_ref[k:k + 1, lanes]
        xcb = xc.astype(bf16)
        r = _sigmoid(_dot(xcb, w_ax_ref[0, j]) + b_ax_ref[:, lanes])
        ig = _sigmoid(_dot(xcb, w_ax_ref[1, j]) + b_ax_ref[:, pl.ds(D_LRU + j * LANES, LANES)])
        a = jnp.exp2(r * log2a_unit[:, j * LANES:(j + 1) * LANES])
        a_s[:, lanes] = a
        om = jnp.maximum(1.0 - a * a, 0.0)
        mult = jnp.where(om > 0.0, om * lax.rsqrt(om), 0.0)
        b_s[:, lanes] = mult * ig * xc

    zpool[pl.ds(0, POOL_BUF * nb), :] = zpool[pl.ds(tm, POOL_BUF * nb), :]
    zconv[pl.ds(0, (CONV_W - 1) * nb), :] = zconv[pl.ds(tm, (CONV_W - 1) * nb), :]

    h = h_s[...]
    for t in range(tb):
        rows = pl.ds(t * nb, nb)
        h = a_s[rows, :] * h + b_s[rows, :]
        b_s[rows, :] = h
    h_s[...] = h
    ycat[:, pl.ds(D_POOL, D_LRU)] = (b_s[...] * gl_s[...]).astype(bf16)

    g1 = mod_part(2)
    yc = ycat[...]
    cb = D // N_RB
    for j in range(N_RB):
        cols = pl.ds(j * cb, cb)
        mix = _dot(yc, w_out_ref[:, cols]).reshape(tb, nb, cb)
        xt[:, :, cols] = ALPHA * xt[:, :, cols] + (1.0 + g1[:, j * cb:(j + 1) * cb])[None] * mix
    for r in range(N_RB):
        xn = _layer_norm(xt[pl.ds(r * tq, tq)], ln1g_ref[...][None], ln1b_ref[...][None])
        x1_ref[pl.ds(r * rq, rq), :] = xn.reshape(rq, D)

    @pl.when(ti == n_tt - 1)
    def _emit_state():
        npool_ref[...] = zpool[pl.ds(0, POOL_BUF * nb), :].reshape(POOL_BUF, nb, D_POOL)
        nconv_ref[...] = zconv[pl.ds(0, (CONV_W - 1) * nb), :].reshape(CONV_W - 1, nb, D_LRU)
        nh_ref[...] = h_s[...]


def _run_mix(x, mod, mod_blk0, pool0, conv0, h0, n_prev, nb, tb, wts):
    bsz, t_len, _ = x.shape
    n_sb = bsz // nb
    n_tt = t_len // tb
    tm = nb * tb
    pitch = _slab_pitch(tb)
    st = lambda q: (q // n_tt, q % n_tt)
    const2 = lambda q: (0, 0)
    const3 = lambda q: (0, 0, 0)
    const4 = lambda q: (0, 0, 0, 0)
    single = pl.Buffered(1)
    in_specs = [
        pl.BlockSpec((nb, tb, D), lambda q: (*st(q), 0)),
        pl.BlockSpec((nb, 6 * D), lambda q: (mod_blk0 + st(q)[0], 0)),
        pl.BlockSpec((POOL_BUF, nb, D_POOL), lambda q: (0, st(q)[0], 0)),
        pl.BlockSpec((CONV_W - 1, nb, D_LRU), lambda q: (0, st(q)[0], 0)),
        pl.BlockSpec((nb, D_LRU), lambda q: (st(q)[0], 0)),
        pl.BlockSpec((D, D_IN), const2, pipeline_mode=single),
        pl.BlockSpec((4, POOL_GROUP, POOL_GROUP), const3),
        pl.BlockSpec((1, D_POOL), const2),
        pl.BlockSpec((CONV_W, D_LRU), const2),
        pl.BlockSpec((1, D_LRU), const2),
        pl.BlockSpec((2, D_LRU // LANES, LANES, LANES), const4),
        pl.BlockSpec((1, 2 * D_LRU), const2),
        pl.BlockSpec((1, D_LRU), const2),
        pl.BlockSpec((D, D), const2, pipeline_mode=single),
        pl.BlockSpec((1, D), const2),
        pl.BlockSpec((1, D), const2),
    ]
    out_specs = [
        pl.BlockSpec((tm, D), lambda q: (q, 0)),
        pl.BlockSpec((POOL_BUF, nb, D_POOL), lambda q: (0, st(q)[0], 0)),
        pl.BlockSpec((CONV_W - 1, nb, D_LRU), lambda q: (0, st(q)[0], 0)),
        pl.BlockSpec((nb, D_LRU), lambda q: (st(q)[0], 0)),
    ]
    out_shape = [
        jax.ShapeDtypeStruct((bsz * t_len, D), f32),
        jax.ShapeDtypeStruct((POOL_BUF, bsz, D_POOL), f32),
        jax.ShapeDtypeStruct((CONV_W - 1, bsz, D_LRU), f32),
        jax.ShapeDtypeStruct((bsz, D_LRU), f32),
    ]
    scratch = [
        pltpu.VMEM((N_SLABS, nb * pitch, LANES), f32),
        pltpu.VMEM((tb, nb, D), f32),
        pltpu.VMEM((tm, D), bf16),
        pltpu.VMEM(((tb + POOL_BUF) * nb, D_POOL), f32),
        pltpu.VMEM(((tb + CONV_W - 1) * nb, D_LRU), f32),
        pltpu.VMEM((tm, D_LRU), f32),
        pltpu.VMEM((tm, D_LRU), f32),
        pltpu.VMEM((tm, D_LRU), f32),
        pltpu.VMEM((nb, D_LRU), f32),
    ]
    body = functools.partial(_mix_body, nb, tb, pitch, n_prev, n_tt)
    return pl.pallas_call(
        body,
        grid=(n_sb * n_tt,),
        in_specs=in_specs,
        out_specs=out_specs,
        out_shape=out_shape,
        scratch_shapes=scratch,
        compiler_params=pltpu.CompilerParams(
            dimension_semantics=("arbitrary",),
            vmem_limit_bytes=VMEM_LIMIT),
        name=f"mix_nb{nb}_tb{tb}",
    )(x, mod, pool0, conv0, h0, *wts)


def _moe_body(n_half, nb, tb, pitch,
              x1_ref, mod_ref, w_rt_ref, b_rt_ref, tri_ref, wg_ref, wu_ref, wd_ref, ln2g_ref, ln2b_ref,
              y_ref,
              u2, tok_t, tok3, rows_s, cnt_s, ys_all, yacc, slab):
    tm = MOE_ROWS
    hm = nb * tb
    tq = tb // N_RB
    rq = tq * nb
    g = pl.program_id(1)

    def mod_part(h, k):
        return mod_ref[pl.ds(h * nb, nb), pl.ds(k * D, D)]

    @pl.when(g == 0)
    def _route():
        for h in range(n_half):
            sh2 = mod_part(h, 3)
            sc2 = mod_part(h, 4)
            for r in range(N_RB):
                rows = pl.ds(h * hm + r * rq, rq)
                v = x1_ref[rows, :].reshape(tq, nb, D) * (1.0 + sc2)[None] + sh2[None]
                u2[rows, :] = v.reshape(rq, D).astype(bf16)
        lt = lax.dot_general(w_rt_ref[...], u2[...], (((1,), (1,)), ((), ())),
                             preferred_element_type=f32) + b_rt_ref[...]
        gl = [lt[k:k + 1, :] for k in range(N_GROUPS)]
        best = gl[0]
        gsel = jnp.zeros_like(best, dtype=i32)
        for k in range(1, N_GROUPS):
            better = gl[k] > best
            best = jnp.where(better, gl[k], best)
            gsel = jnp.where(better, k, gsel)
        denom = jnp.exp(gl[0] - best)
        for k in range(1, N_GROUPS):
            denom = denom + jnp.exp(gl[k] - best)
        p_sel = 1.0 / denom
        es = []
        for j in range(EPG):
            v_j = lt[SUBLANES + j:SUBLANES + j + 1, :]
            for k in range(1, N_GROUPS):
                r0 = SUBLANES * (k + 1) + j
                v_j = jnp.where(gsel == k, lt[r0:r0 + 1, :], v_j)
            es.append(v_j)
        v1 = es[0]
        i1 = jnp.zeros_like(gsel)
        for j in range(1, EPG):
            better = es[j] > v1
            v1 = jnp.where(better, es[j], v1)
            i1 = jnp.where(better, j, i1)
        v2 = jnp.full_like(v1, -jnp.inf)
        i2 = jnp.full_like(gsel, -1)
        for j in range(EPG):
            cand = jnp.logical_and(i1 != j, jnp.logical_or(i2 < 0, es[j] > v2))
            v2 = jnp.where(cand, es[j], v2)
            i2 = jnp.where(cand, j, i2)
        e21 = jnp.exp(v2 - v1)
        w1 = p_sel / (1.0 + e21)
        w2 = p_sel * e21 / (1.0 + e21)
        rid8 = lax.broadcasted_iota(i32, (SUBLANES, tm), 0)
        onehot = jnp.where(rid8 == gsel, 1.0, 0.0)
        pref = _dot(onehot.astype(bf16), tri_ref[...])
        rank = jnp.sum(onehot * pref, axis=0, keepdims=True)
        n_max = 0
        for k in range(N_GROUPS):
            n_k = jnp.sum(jnp.where(gsel == k, 1.0, 0.0)).astype(i32)
            cnt_s[k] = n_k
            n_max = jnp.maximum(n_max, n_k)
        cnt_s[N_GROUPS] = (n_max > CHUNK).astype(i32)
        rows_s[0:1, :] = gsel
        rows_s[1:2, :] = rank.astype(i32)
        rid = lax.broadcasted_iota(i32, (LANES, tm), 0)
        rec = jnp.zeros((LANES, tm), f32)
        for j in range(EPG):
            wj = jnp.where(i1 == j, w1, jnp.where(i2 == j, w2, 0.0))
            hi = wj.astype(bf16).astype(f32)
            mid = (wj - hi).astype(bf16).astype(f32)
            lo = (wj - hi - mid).astype(bf16).astype(f32)
            rec = jnp.where(rid == j, hi, rec)
            rec = jnp.where(rid == ROW_LO + j, mid, rec)
            rec = jnp.where(rid == ROW_LO2 + j, lo, rec)
        rec = jnp.where(rid == ROW_GSEL, gsel.astype(f32), rec)
        rec = jnp.where(rid == ROW_RANK, rank, rec)
        rt = rec.T
        tok_t[...] = rt
        tok3[...] = rt.astype(bf16)

    def sorted_experts(base):
        gsel_row = rows_s[0:1, :]
        rank_row = rows_s[1:2, :]
        jj = lax.broadcasted_iota(i32, (CHUNK, tm), 0) + base
        pm = jnp.where(jnp.logical_and(gsel_row == g, rank_row == jj), 1.0, 0.0).astype(bf16)
        xs = _dot(pm, u2[...]).astype(bf16)
        cs3 = _dot(pm, tok3[...])
        cs = (cs3 + pltpu.roll(cs3, LANES - ROW_LO, axis=1)
              + pltpu.roll(cs3, LANES - ROW_LO2, axis=1))
        acts = []
        for e in range(EPG):
            hg = _dot(xs, wg_ref[e])
            hu = _dot(xs, wu_ref[e])
            acts.append((hg * _sigmoid(hg) * hu * cs[:, e:e + 1]).astype(bf16))
        hb = jnp.concatenate(acts, axis=1)
        return _dot(hb, wd_ref[...].reshape(EPG * D_EXPERT, D)).astype(bf16)

    ys_all[pl.ds(pl.multiple_of(g * CHUNK_PAD, CHUNK_PAD), CHUNK), :] = sorted_experts(0)

    @pl.when(g == 0)
    def _zero_pad_rows():
        for k in range(N_GROUPS):
            ys_all[pl.ds(k * CHUNK_PAD + CHUNK, CHUNK_PAD - CHUNK), :] = jnp.zeros((CHUNK_PAD - CHUNK, D), bf16)

    has_overflow = cnt_s[N_GROUPS] > 0

    @pl.when(jnp.logical_and(g == 0, has_overflow))
    def _zero_acc():
        yacc[...] = jnp.zeros((tm, D), f32)

    n_chunks = pl.cdiv(cnt_s[g], CHUNK)

    def overflow(c, carry):
        base = c * CHUNK
        ys = sorted_experts(base)
        rec = tok_t[...]
        gsel_col = rec[:, ROW_GSEL:ROW_GSEL + 1]
        rank_col = rec[:, ROW_RANK:ROW_RANK + 1]
        jl = (lax.broadcasted_iota(i32, (tm, CHUNK), 1) + base).astype(f32)
        pt = jnp.where(jnp.logical_and(gsel_col == g.astype(f32), rank_col == jl), 1.0, 0.0).astype(bf16)
        yacc[...] += _dot(pt, ys)
        return carry

    lax.fori_loop(1, n_chunks, overflow, 0)

    def finish(with_acc):
        tqf = FIN_ROWS // nb
        jl = lax.broadcasted_iota(i32, (FIN_ROWS, N_GROUPS * CHUNK_PAD), 1).astype(f32)
        for h in range(n_half):
            g2 = mod_part(h, 5)
            for r in range(hm // FIN_ROWS):
                rows = pl.ds(h * hm + r * FIN_ROWS, FIN_ROWS)
                rec = tok_t[rows, :]
                gsel_col = rec[:, ROW_GSEL:ROW_GSEL + 1]
                rank_col = rec[:, ROW_RANK:ROW_RANK + 1]
                key = jnp.where(rank_col < f32(CHUNK), gsel_col * f32(CHUNK_PAD) + rank_col, -1.0)
                pt = jnp.where(key == jl, 1.0, 0.0).astype(bf16)
                y = _dot(pt, ys_all[...])
                if with_acc:
                    y = y + yacc[rows, :]
                v = ALPHA * x1_ref[rows, :] + ((1.0 + g2)[None] * y.reshape(tqf, nb, D)).reshape(FIN_ROWS, D)
                yn = _layer_norm(v, ln2g_ref[...], ln2b_ref[...])
                for tl in range(tqf):
                    for j in range(N_SLABS):
                        slab[j, pl.ds(r * tqf + tl, nb, stride=pitch), :] = (
                            yn[tl * nb:(tl + 1) * nb, j * LANES:(j + 1) * LANES])
            for b in range(nb):
                for j in range(N_SLABS):
                    y_ref[h * nb + b, :, pl.ds(j * LANES, LANES)] = slab[j, pl.ds(b * pitch, tb), :]

    @pl.when(jnp.logical_and(g == N_GROUPS - 1, jnp.logical_not(has_overflow)))
    def _finish():
        finish(False)

    @pl.when(jnp.logical_and(g == N_GROUPS - 1, has_overflow))
    def _finish_with_overflow():
        finish(True)


def _run_moe(x1, mod, mod_blk0, n_half, nb, tb, out_seqs, out_len, wts):
    n_tiles = x1.shape[0] // MOE_ROWS
    assert n_half * nb * tb == MOE_ROWS
    seq_per_tile = n_half * nb
    n_tt = out_len // tb
    pitch = _slab_pitch(tb)
    st = lambda q: (q // n_tt, q % n_tt)
    const2 = lambda q, g: (0, 0)
    grp3 = lambda q, g: (g, 0, 0)
    single = pl.Buffered(1)
    in_specs = [
        pl.BlockSpec((MOE_ROWS, D), lambda q, g: (q, 0)),
        pl.BlockSpec((seq_per_tile, 6 * D), lambda q, g: (mod_blk0 + st(q)[0], 0)),
        pl.BlockSpec((ROUTE_ROWS, D), const2),
        pl.BlockSpec((ROUTE_ROWS, 1), const2),
        pl.BlockSpec((MOE_ROWS, MOE_ROWS), const2, pipeline_mode=single),
        pl.BlockSpec((EPG, D, D_EXPERT), grp3),
        pl.BlockSpec((EPG, D, D_EXPERT), grp3),
        pl.BlockSpec((EPG, D_EXPERT, D), grp3),
        pl.BlockSpec((1, D), const2),
        pl.BlockSpec((1, D), const2),
    ]
    scratch = [
        pltpu.VMEM((MOE_ROWS, D), bf16),
        pltpu.VMEM((MOE_ROWS, LANES), f32),
        pltpu.VMEM((MOE_ROWS, LANES), bf16),
        pltpu.VMEM((SUBLANES, MOE_ROWS), i32),
        pltpu.SMEM((N_GROUPS + 1,), i32),
        pltpu.VMEM((N_GROUPS * CHUNK_PAD, D), bf16),
        pltpu.VMEM((MOE_ROWS, D), f32),
        pltpu.VMEM((N_SLABS, nb * pitch, LANES), f32),
    ]
    body = functools.partial(_moe_body, n_half, nb, tb, pitch)
    return pl.pallas_call(
        body,
        grid=(n_tiles, N_GROUPS),
        in_specs=in_specs,
        out_specs=pl.BlockSpec((seq_per_tile, tb, D), lambda q, g: (*st(q), 0)),
        out_shape=jax.ShapeDtypeStruct((out_seqs, out_len, D), f32),
        scratch_shapes=scratch,
        compiler_params=pltpu.CompilerParams(
            dimension_semantics=("arbitrary", "arbitrary"),
            vmem_limit_bytes=VMEM_LIMIT),
        name=f"moe_nb{nb}_tb{tb}",
    )(x1, mod, *wts)


def _pair_blocks(w):
    n, k, _ = w.shape
    wp = w.reshape(n // 2, 2, k, k)
    eye = jnp.eye(2, dtype=w.dtype)
    return jnp.einsum('phij,hg->phigj', wp, eye).reshape(n // 2, 2 * k, 2 * k)


def kernel(x_prompt, x_sample, c_prompt, c_sample, state_pool, state_conv, state_lru, w_ada, b_ada, w_in, w_pool, pool_scale, w_conv, b_conv, w_a, b_a, w_x, b_x, lru_lambda, w_out, ln1_g, ln1_b, w_group, b_group, w_route, b_route, w_gate, w_up, w_down, ln2_g, ln2_b):
    l = 0
    bp, tp, _ = x_prompt.shape
    bs, ts, _ = x_sample.shape
    mod = _ada_mod(jnp.concatenate([c_sample, c_prompt], axis=0), w_ada[l], b_ada[l])

    w_rt = jnp.zeros((ROUTE_ROWS, D), f32).at[0:N_GROUPS].set(w_group[l].T)
    b_rt = jnp.zeros((ROUTE_ROWS,), f32).at[0:N_GROUPS].set(b_group[l])
    for k in range(N_GROUPS):
        r0 = SUBLANES * (k + 1)
        w_rt = w_rt.at[r0:r0 + EPG].set(w_route[l][:, k * EPG:(k + 1) * EPG].T)
        b_rt = b_rt.at[r0:r0 + EPG].set(b_route[l][k * EPG:(k + 1) * EPG])
    tri = jnp.triu(jnp.ones((MOE_ROWS, MOE_ROWS), bf16), 1)

    mix_wts = (
        w_in[l].astype(bf16),
        w_pool[l].astype(bf16),
        pool_scale[l].reshape(1, D_POOL),
        w_conv[l],
        b_conv[l].reshape(1, D_LRU),
        jnp.stack([_pair_blocks(w_a[l]), _pair_blocks(w_x[l])]).astype(bf16),
        jnp.concatenate([b_a[l], b_x[l]]).reshape(1, 2 * D_LRU),
        lru_lambda[l].reshape(1, D_LRU),
        w_out[l].astype(bf16),
        ln1_g[l].reshape(1, D),
        ln1_b[l].reshape(1, D),
    )
    moe_wts = (
        w_rt.astype(bf16),
        b_rt.reshape(ROUTE_ROWS, 1),
        tri,
        w_gate[l].astype(bf16),
        w_up[l].astype(bf16),
        w_down[l].astype(bf16),
        ln2_g[l].reshape(1, D),
        ln2_b[l].reshape(1, D),
    )

    tb_p = MOE_ROWS // bp
    zp = jnp.zeros((POOL_BUF, bp, D_POOL), f32)
    zc = jnp.zeros((CONV_W - 1, bp, D_LRU), f32)
    zh = jnp.zeros((bp, D_LRU), f32)
    x1p, pool_p, conv_p, lru_p = _run_mix(x_prompt, mod, bs // bp, zp, zc, zh, 0, bp, tb_p, mix_wts)
    yp = _run_moe(x1p, mod, bs // bp, 1, bp, tb_p, bp, tp, moe_wts)

    n_prev_s = min(PAST_LEN, POOL_BUF)
    nb_s = bs // 2
    x1s, pool_s, conv_s, lru_s = _run_mix(
        x_sample, mod, 0, state_pool[l].transpose(1, 0, 2), state_conv[l].transpose(1, 0, 2),
        state_lru[l], n_prev_s, nb_s, ts, mix_wts)
    ys = _run_moe(x1s, mod, 0, 2, nb_s, ts, bs, ts, moe_wts)

    tr = lambda a: a.transpose(1, 0, 2)[None]
    return (yp, ys, tr(pool_p), tr(conv_p), lru_p[None], tr(pool_s), tr(conv_s), lru_s[None])
```

```python
import functools
import math

import jax
import jax.numpy as jnp
from jax import lax
from jax.experimental import pallas as pl
from jax.experimental.pallas import tpu as pltpu

D = 1024
D_POOL = 512
D_LRU = 512
D_IN = D_POOL + 2 * D_LRU
POOL_WINDOWS = (2, 4, 8, 16)
POOL_GROUP = 128
POOL_BUF = 15
CONV_W = 4
LRU_C = 8.0
N_GROUPS = 4
EPG = 4
D_EXPERT = 256
DEPTH = 1
ALPHA = (2.0 * DEPTH) ** 0.25
LN_EPS = 1e-5
PAST_LEN = 16384

LANES = 128
SUBLANES = 8
N_SLABS = D // LANES
ROUTE_ROWS = 48
MOE_ROWS = 1024
CHUNK = 288
CHUNK_PAD = 320
ROW_LO = 4
ROW_LO2 = 8
ROW_GSEL = 12
ROW_RANK = 13
N_RB = 4
FIN_ROWS = 256
VMEM_LIMIT = 60 * 1024 * 1024

f32 = jnp.float32
bf16 = jnp.bfloat16
i32 = jnp.int32


def _dot(a, b):
    return jnp.dot(a, b, preferred_element_type=f32)


LOG2E = 1.0 / math.log(2.0)


def _sigmoid(x):
    return 1.0 / (1.0 + jnp.exp2(x * (-LOG2E)))


def _gelu_tanh(x):
    c = math.sqrt(2.0 / math.pi)
    hx = 0.5 * x
    return hx + hx * jnp.tanh(x * (c + (c * 0.044715) * (x * x)))


def _layer_norm(v, g, b):
    mu = jnp.mean(v, axis=-1, keepdims=True)
    c = v - mu
    var = jnp.mean(c * c, axis=-1, keepdims=True)
    return c * lax.rsqrt(var + LN_EPS) * g + b


def _slab_pitch(tb):
    return tb + SUBLANES if (tb // SUBLANES) % 2 == 0 else tb


def _ada_body(c_ref, w_ref, b_ref, o_ref):
    o_ref[...] = _dot(c_ref[...].astype(bf16), w_ref[...].astype(bf16)) + b_ref[...]


def _ada_mod(c_all, w_ada, b_ada):
    n = c_all.shape[0]
    bn = 1024
    return pl.pallas_call(
        _ada_body,
        grid=(6 * D // bn,),
        in_specs=[
            pl.BlockSpec((n, D), lambda j: (0, 0)),
            pl.BlockSpec((D, bn), lambda j: (0, j)),
            pl.BlockSpec((1, bn), lambda j: (0, j)),
        ],
        out_specs=pl.BlockSpec((n, bn), lambda j: (0, j)),
        out_shape=jax.ShapeDtypeStruct((n, 6 * D), f32),
        name="ada_mod",
    )(c_all, w_ada, b_ada.reshape(1, 6 * D))


def _mix_body(nb, tb, pitch, n_prev, n_tt,
              x_ref, mod_ref, pool0_ref, conv0_ref, h0_ref,
              w_in_ref, w_pool_ref, pscale_ref, w_conv_ref, b_conv_ref,
              w_ax_ref, b_ax_ref, lam_ref, w_out_ref, ln1g_ref, ln1b_ref,
              x1_ref, npool_ref, nconv_ref, nh_ref,
              slab, xt, ycat, zpool, zconv, a_s, b_s, gl_s, h_s):
    tm = nb * tb
    tq = tb // N_RB
    rq = tq * nb
    ti = pl.program_id(0) % n_tt

    def mod_part(k):
        return mod_ref[:, pl.ds(k * D, D)]

    @pl.when(ti == 0)
    def _init_state():
        zpool[pl.ds(0, POOL_BUF * nb), :] = pool0_ref[...].reshape(POOL_BUF * nb, D_POOL)
        zconv[pl.ds(0, (CONV_W - 1) * nb), :] = conv0_ref[...].reshape((CONV_W - 1) * nb, D_LRU)
        h_s[...] = h0_ref[...]

    for b in range(nb):
        for j in range(N_SLABS):
            slab[j, pl.ds(b * pitch, tb), :] = x_ref[b, :, pl.ds(j * LANES, LANES)]
    for t in range(tb):
        for j in range(N_SLABS):
            xt[t, :, pl.ds(j * LANES, LANES)] = slab[j, pl.ds(t, nb, stride=pitch), :]

    sh1 = mod_part(0)
    sc1 = mod_part(1)
    for r in range(N_RB):
        u = xt[pl.ds(r * tq, tq)] * (1.0 + sc1)[None] + sh1[None]
        ycat[pl.ds(r * rq, rq), :] = u.reshape(rq, D).astype(bf16)
    ub = ycat[...]
    zpool[pl.ds(POOL_BUF * nb, tm), :] = _dot(ub, w_in_ref[:, pl.ds(0, D_POOL)])
    zconv[pl.ds((CONV_W - 1) * nb, tm), :] = _dot(ub, w_in_ref[:, pl.ds(D_POOL, D_LRU)])
    half = D_LRU // 2
    for hpart in range(2):
        cols = pl.ds(D_POOL + D_LRU + hpart * half, half)
        gl_s[:, pl.ds(hpart * half, half)] = _gelu_tanh(_dot(ub, w_in_ref[:, cols]))

    row = lax.broadcasted_iota(i32, (tm, LANES), 0)
    t_loc = lax.shift_right_logical(row, int(math.log2(nb)))
    t_glob = (ti * tb + t_loc + (1 + n_prev)).astype(f32)
    for c, w in enumerate(POOL_WINDOWS):
        lanes = pl.ds(c * LANES, LANES)
        s = zpool[pl.ds((POOL_BUF + 1 - w) * nb, tm + (w - 1) * nb), lanes]
        step = 1
        while step < w:
            s = s[step * nb:] + s[:-step * nb]
            step *= 2
        cnt = jnp.minimum(f32(w), t_glob)
        dlt = s / cnt - zpool[pl.ds(POOL_BUF * nb, tm), lanes]
        yp = _dot(dlt.astype(bf16), w_pool_ref[c]) * pscale_ref[:, lanes]
        ycat[:, lanes] = yp.astype(bf16)

    nl = -lam_ref[...]
    softplus = jnp.maximum(nl, 0.0) + jnp.log(1.0 + jnp.exp(-jnp.abs(nl)))
    log2a_unit = (-LRU_C * LOG2E) * softplus
    for j in range(D_LRU // LANES):
        lanes = pl.ds(j * LANES, LANES)
        xc = b_conv_ref[:, lanes] + zconv[pl.ds(0, tm), lanes] * w_conv_ref[0:1, lanes]
        for k in range(1, CONV_W):
            xc = xc + zconv[pl.ds(k * nb, tm), lanes] * w_conv_ref[k:k + 1, lanes]
        xcb = xc.astype(bf16)
        r = _sigmoid(_dot(xcb, w_ax_ref[0, j]) + b_ax_ref[:, lanes])
        ig = _sigmoid(_dot(xcb, w_ax_ref[1, j]) + b_ax_ref[:, pl.ds(D_LRU + j * LANES, LANES)])
        a = jnp.exp2(r * log2a_unit[:, j * LANES:(j + 1) * LANES])
        a_s[:, lanes] = a
        om = jnp.maximum(1.0 - a * a, 0.0)
        mult = jnp.where(om > 0.0, om * lax.rsqrt(om), 0.0)
        b_s[:, lanes] = mult * ig * xc

    zpool[pl.ds(0, POOL_BUF * nb), :] = zpool[pl.ds(tm, POOL_BUF * nb), :]
    zconv[pl.ds(0, (CONV_W - 1) * nb), :] = zconv[pl.ds(tm, (CONV_W - 1) * nb), :]

    h = h_s[...]
    for t in range(tb):
        rows = pl.ds(t * nb, nb)
        h = a_s[rows, :] * h + b_s[rows, :]
        b_s[rows, :] = h
    h_s[...] = h
    ycat[:, pl.ds(D_POOL, D_LRU)] = (b_s[...] * gl_s[...]).astype(bf16)

    g1 = mod_part(2)
    yc = ycat[...]
    cb = D // N_RB
    for j in range(N_RB):
        cols = pl.ds(j * cb, cb)
        mix = _dot(yc, w_out_ref[:, cols]).reshape(tb, nb, cb)
        xt[:, :, cols] = ALPHA * xt[:, :, cols] + (1.0 + g1[:, j * cb:(j + 1) * cb])[None] * mix
    for r in range(N_RB):
        xn = _layer_norm(xt[pl.ds(r * tq, tq)], ln1g_ref[...][None], ln1b_ref[...][None])
        x1_ref[pl.ds(r * rq, rq), :] = xn.reshape(rq, D)

    @pl.when(ti == n_tt - 1)
    def _emit_state():
        npool_ref[...] = zpool[pl.ds(0, POOL_BUF * nb), :].reshape(POOL_BUF, nb, D_POOL)
        nconv_ref[...] = zconv[pl.ds(0, (CONV_W - 1) * nb), :].reshape(CONV_W - 1, nb, D_LRU)
        nh_ref[...] = h_s[...]


def _run_mix(x, mod, mod_blk0, pool0, conv0, h0, n_prev, nb, tb, wts):
    bsz, t_len, _ = x.shape
    n_sb = bsz // nb
    n_tt = t_len // tb
    tm = nb * tb
    pitch = _slab_pitch(tb)
    st = lambda q: (q // n_tt, q % n_tt)
    const2 = lambda q: (0, 0)
    const3 = lambda q: (0, 0, 0)
    const4 = lambda q: (0, 0, 0, 0)
    single = pl.Buffered(1)
    in_specs = [
        pl.BlockSpec((nb, tb, D), lambda q: (*st(q), 0)),
        pl.BlockSpec((nb, 6 * D), lambda q: (mod_blk0 + st(q)[0], 0)),
        pl.BlockSpec((POOL_BUF, nb, D_POOL), lambda q: (0, st(q)[0], 0)),
        pl.BlockSpec((CONV_W - 1, nb, D_LRU), lambda q: (0, st(q)[0], 0)),
        pl.BlockSpec((nb, D_LRU), lambda q: (st(q)[0], 0)),
        pl.BlockSpec((D, D_IN), const2, pipeline_mode=single),
        pl.BlockSpec((4, POOL_GROUP, POOL_GROUP), const3),
        pl.BlockSpec((1, D_POOL), const2),
        pl.BlockSpec((CONV_W, D_LRU), const2),
        pl.BlockSpec((1, D_LRU), const2),
        pl.BlockSpec((2, D_LRU // LANES, LANES, LANES), const4),
        pl.BlockSpec((1, 2 * D_LRU), const2),
        pl.BlockSpec((1, D_LRU), const2),
        pl.BlockSpec((D, D), const2, pipeline_mode=single),
        pl.BlockSpec((1, D), const2),
        pl.BlockSpec((1, D), const2),
    ]
    out_specs = [
        pl.BlockSpec((tm, D), lambda q: (q, 0)),
        pl.BlockSpec((POOL_BUF, nb, D_POOL), lambda q: (0, st(q)[0], 0)),
        pl.BlockSpec((CONV_W - 1, nb, D_LRU), lambda q: (0, st(q)[0], 0)),
        pl.BlockSpec((nb, D_LRU), lambda q: (st(q)[0], 0)),
    ]
    out_shape = [
        jax.ShapeDtypeStruct((bsz * t_len, D), f32),
        jax.ShapeDtypeStruct((POOL_BUF, bsz, D_POOL), f32),
        jax.ShapeDtypeStruct((CONV_W - 1, bsz, D_LRU), f32),
        jax.ShapeDtypeStruct((bsz, D_LRU), f32),
    ]
    scratch = [
        pltpu.VMEM((N_SLABS, nb * pitch, LANES), f32),
        pltpu.VMEM((tb, nb, D), f32),
        pltpu.VMEM((tm, D), bf16),
        pltpu.VMEM(((tb + POOL_BUF) * nb, D_POOL), f32),
        pltpu.VMEM(((tb + CONV_W - 1) * nb, D_LRU), f32),
        pltpu.VMEM((tm, D_LRU), f32),
        pltpu.VMEM((tm, D_LRU), f32),
        pltpu.VMEM((tm, D_LRU), f32),
        pltpu.VMEM((nb, D_LRU), f32),
    ]
    body = functools.partial(_mix_body, nb, tb, pitch, n_prev, n_tt)
    return pl.pallas_call(
        body,
        grid=(n_sb * n_tt,),
        in_specs=in_specs,
        out_specs=out_specs,
        out_shape=out_shape,
        scratch_shapes=scratch,
        compiler_params=pltpu.CompilerParams(
            dimension_semantics=("arbitrary",),
            vmem_limit_bytes=VMEM_LIMIT),
        name=f"mix_nb{nb}_tb{tb}",
    )(x, mod, pool0, conv0, h0, *wts)


def _moe_body(n_half, nb, tb, pitch,
              x1_ref, mod_ref, w_rt_ref, b_rt_ref, tri_ref, wg_ref, wu_ref, wd_ref, ln2g_ref, ln2b_ref,
              y_ref,
              u2, tok_t, tok3, rows_s, cnt_s, ys_all, yacc, slab):
    tm = MOE_ROWS
    hm = nb * tb
    tq = tb // N_RB
    rq = tq * nb
    g = pl.program_id(1)

    def mod_part(h, k):
        return mod_ref[pl.ds(h * nb, nb), pl.ds(k * D, D)]

    @pl.when(g == 0)
    def _route():
        for h in range(n_half):
            sh2 = mod_part(h, 3)
            sc2 = mod_part(h, 4)
            for r in range(N_RB):
                rows = pl.ds(h * hm + r * rq, rq)
                v = x1_ref[rows, :].reshape(tq, nb, D) * (1.0 + sc2)[None] + sh2[None]
                u2[rows, :] = v.reshape(rq, D).astype(bf16)
        lt = lax.dot_general(w_rt_ref[...], u2[...], (((1,), (1,)), ((), ())),
                             preferred_element_type=f32) + b_rt_ref[...]
        gl = [lt[k:k + 1, :] for k in range(N_GROUPS)]
        best = gl[0]
        gsel = jnp.zeros_like(best, dtype=i32)
        for k in range(1, N_GROUPS):
            better = gl[k] > best
            best = jnp.where(better, gl[k], best)
            gsel = jnp.where(better, k, gsel)
        denom = jnp.exp(gl[0] - best)
        for k in range(1, N_GROUPS):
            denom = denom + jnp.exp(gl[k] - best)
        p_sel = 1.0 / denom
        es = []
        for j in range(EPG):
            v_j = lt[SUBLANES + j:SUBLANES + j + 1, :]
            for k in range(1, N_GROUPS):
                r0 = SUBLANES * (k + 1) + j
                v_j = jnp.where(gsel == k, lt[r0:r0 + 1, :], v_j)
            es.append(v_j)
        v1 = es[0]
        i1 = jnp.zeros_like(gsel)
        for j in range(1, EPG):
            better = es[j] > v1
            v1 = jnp.where(better, es[j], v1)
            i1 = jnp.where(better, j, i1)
        v2 = jnp.full_like(v1, -jnp.inf)
        i2 = jnp.full_like(gsel, -1)
        for j in range(EPG):
            cand = jnp.logical_and(i1 != j, jnp.logical_or(i2 < 0, es[j] > v2))
            v2 = jnp.where(cand, es[j], v2)
            i2 = jnp.where(cand, j, i2)
        e21 = jnp.exp(v2 - v1)
        w1 = p_sel / (1.0 + e21)
        w2 = p_sel * e21 / (1.0 + e21)
        rid8 = lax.broadcasted_iota(i32, (SUBLANES, tm), 0)
        onehot = jnp.where(rid8 == gsel, 1.0, 0.0)
        pref = _dot(onehot.astype(bf16), tri_ref[...])
        rank = jnp.sum(onehot * pref, axis=0, keepdims=True)
        n_max = 0
        for k in range(N_GROUPS):
            n_k = jnp.sum(jnp.where(gsel == k, 1.0, 0.0)).astype(i32)
            cnt_s[k] = n_k
            n_max = jnp.maximum(n_max, n_k)
        cnt_s[N_GROUPS] = (n_max > CHUNK).astype(i32)
        rows_s[0:1, :] = gsel
        rows_s[1:2, :] = rank.astype(i32)
        rid = lax.broadcasted_iota(i32, (LANES, tm), 0)
        rec = jnp.zeros((LANES, tm), f32)
        for j in range(EPG):
            wj = jnp.where(i1 == j, w1, jnp.where(i2 == j, w2, 0.0))
            hi = wj.astype(bf16).astype(f32)
            mid = (wj - hi).astype(bf16).astype(f32)
            lo = (wj - hi - mid).astype(bf16).astype(f32)
            rec = jnp.where(rid == j, hi, rec)
            rec = jnp.where(rid == ROW_LO + j, mid, rec)
            rec = jnp.where(rid == ROW_LO2 + j, lo, rec)
        rec = jnp.where(rid == ROW_GSEL, gsel.astype(f32), rec)
        rec = jnp.where(rid == ROW_RANK, rank, rec)
        rt = rec.T
        tok_t[...] = rt
        tok3[...] = rt.astype(bf16)

    def sorted_experts(base):
        gsel_row = rows_s[0:1, :]
        rank_row = rows_s[1:2, :]
        jj = lax.broadcasted_iota(i32, (CHUNK, tm), 0) + base
        pm = jnp.where(jnp.logical_and(gsel_row == g, rank_row == jj), 1.0, 0.0).astype(bf16)
        xs = _dot(pm, u2[...]).astype(bf16)
        cs3 = _dot(pm, tok3[...])
        cs = (cs3 + pltpu.roll(cs3, LANES - ROW_LO, axis=1)
              + pltpu.roll(cs3, LANES - ROW_LO2, axis=1))
        acts = []
        for e in range(EPG):
            hg = _dot(xs, wg_ref[g * EPG + e])
            hu = _dot(xs, wu_ref[g * EPG + e])
            acts.append((hg * _sigmoid(hg) * hu * cs[:, e:e + 1]).astype(bf16))
        hb = jnp.concatenate(acts, axis=1)
        wd_g = wd_ref[pl.ds(g * EPG, EPG)].reshape(EPG * D_EXPERT, D)
        return _dot(hb, wd_g).astype(bf16)

    ys_all[pl.ds(pl.multiple_of(g * CHUNK_PAD, CHUNK_PAD), CHUNK), :] = sorted_experts(0)

    @pl.when(g == 0)
    def _zero_pad_rows():
        for k in range(N_GROUPS):
            ys_all[pl.ds(k * CHUNK_PAD + CHUNK, CHUNK_PAD - CHUNK), :] = jnp.zeros((CHUNK_PAD - CHUNK, D), bf16)

    has_overflow = cnt_s[N_GROUPS] > 0

    @pl.when(jnp.logical_and(g == 0, has_overflow))
    def _zero_acc():
        yacc[...] = jnp.zeros((tm, D), f32)

    n_chunks = pl.cdiv(cnt_s[g], CHUNK)

    def overflow(c, carry):
        base = c * CHUNK
        ys = sorted_experts(base)
        rec = tok_t[...]
        gsel_col = rec[:, ROW_GSEL:ROW_GSEL + 1]
        rank_col = rec[:, ROW_RANK:ROW_RANK + 1]
        jl = (lax.broadcasted_iota(i32, (tm, CHUNK), 1) + base).astype(f32)
        pt = jnp.where(jnp.logical_and(gsel_col == g.astype(f32), rank_col == jl), 1.0, 0.0).astype(bf16)
        yacc[...] += _dot(pt, ys)
        return carry

    lax.fori_loop(1, n_chunks, overflow, 0)

    def finish(with_acc):
        tqf = FIN_ROWS // nb
        jl = lax.broadcasted_iota(i32, (FIN_ROWS, N_GROUPS * CHUNK_PAD), 1).astype(f32)
        for h in range(n_half):
            g2 = mod_part(h, 5)
            for r in range(hm // FIN_ROWS):
                rows = pl.ds(h * hm + r * FIN_ROWS, FIN_ROWS)
                rec = tok_t[rows, :]
                gsel_col = rec[:, ROW_GSEL:ROW_GSEL + 1]
                rank_col = rec[:, ROW_RANK:ROW_RANK + 1]
                key = jnp.where(rank_col < f32(CHUNK), gsel_col * f32(CHUNK_PAD) + rank_col, -1.0)
                pt = jnp.where(key == jl, 1.0, 0.0).astype(bf16)
                y = _dot(pt, ys_all[...])
                if with_acc:
                    y = y + yacc[rows, :]
                v = ALPHA * x1_ref[rows, :] + ((1.0 + g2)[None] * y.reshape(tqf, nb, D)).reshape(FIN_ROWS, D)
                yn = _layer_norm(v, ln2g_ref[...], ln2b_ref[...])
                for tl in range(tqf):
                    for j in range(N_SLABS):
                        slab[j, pl.ds(r * tqf + tl, nb, stride=pitch), :] = (
                            yn[tl * nb:(tl + 1) * nb, j * LANES:(j + 1) * LANES])
            for b in range(nb):
                for j in range(N_SLABS):
                    y_ref[h * nb + b, :, pl.ds(j * LANES, LANES)] = slab[j, pl.ds(b * pitch, tb), :]

    @pl.when(jnp.logical_and(g == N_GROUPS - 1, jnp.logical_not(has_overflow)))
    def _finish():
        finish(False)

    @pl.when(jnp.logical_and(g == N_GROUPS - 1, has_overflow))
    def _finish_with_overflow():
        finish(True)


def _run_moe(x1, mod, mod_blk0, n_half, nb, tb, out_seqs, out_len, wts):
    n_tiles = x1.shape[0] // MOE_ROWS
    assert n_half * nb * tb == MOE_ROWS
    seq_per_tile = n_half * nb
    n_tt = out_len // tb
    pitch = _slab_pitch(tb)
    st = lambda q: (q // n_tt, q % n_tt)
    const2 = lambda q, g: (0, 0)
    const3 = lambda q, g: (0, 0, 0)
    single = pl.Buffered(1)
    in_specs = [
        pl.BlockSpec((MOE_ROWS, D), lambda q, g: (q, 0)),
        pl.BlockSpec((seq_per_tile, 6 * D), lambda q, g: (mod_blk0 + st(q)[0], 0)),
        pl.BlockSpec((ROUTE_ROWS, D), const2),
        pl.BlockSpec((ROUTE_ROWS, 1), const2),
        pl.BlockSpec((MOE_ROWS, MOE_ROWS), const2, pipeline_mode=single),
        pl.BlockSpec((N_GROUPS * EPG, D, D_EXPERT), const3, pipeline_mode=single),
        pl.BlockSpec((N_GROUPS * EPG, D, D_EXPERT), const3, pipeline_mode=single),
        pl.BlockSpec((N_GROUPS * EPG, D_EXPERT, D), const3, pipeline_mode=single),
        pl.BlockSpec((1, D), const2),
        pl.BlockSpec((1, D), const2),
    ]
    scratch = [
        pltpu.VMEM((MOE_ROWS, D), bf16),
        pltpu.VMEM((MOE_ROWS, LANES), f32),
        pltpu.VMEM((MOE_ROWS, LANES), bf16),
        pltpu.VMEM((SUBLANES, MOE_ROWS), i32),
        pltpu.SMEM((N_GROUPS + 1,), i32),
        pltpu.VMEM((N_GROUPS * CHUNK_PAD, D), bf16),
        pltpu.VMEM((MOE_ROWS, D), f32),
        pltpu.VMEM((N_SLABS, nb * pitch, LANES), f32),
    ]
    body = functools.partial(_moe_body, n_half, nb, tb, pitch)
    return pl.pallas_call(
        body,
        grid=(n_tiles, N_GROUPS),
        in_specs=in_specs,
        out_specs=pl.BlockSpec((seq_per_tile, tb, D), lambda q, g: (*st(q), 0)),
        out_shape=jax.ShapeDtypeStruct((out_seqs, out_len, D), f32),
        scratch_shapes=scratch,
        compiler_params=pltpu.CompilerParams(
            dimension_semantics=("arbitrary", "arbitrary"),
            vmem_limit_bytes=VMEM_LIMIT),
        name=f"moe_nb{nb}_tb{tb}",
    )(x1, mod, *wts)


def _pair_blocks(w):
    n, k, _ = w.shape
    wp = w.reshape(n // 2, 2, k, k)
    eye = jnp.eye(2, dtype=w.dtype)
    return jnp.einsum('phij,hg->phigj', wp, eye).reshape(n // 2, 2 * k, 2 * k)


def kernel(x_prompt, x_sample, c_prompt, c_sample, state_pool, state_conv, state_lru, w_ada, b_ada, w_in, w_pool, pool_scale, w_conv, b_conv, w_a, b_a, w_x, b_x, lru_lambda, w_out, ln1_g, ln1_b, w_group, b_group, w_route, b_route, w_gate, w_up, w_down, ln2_g, ln2_b):
    l = 0
    bp, tp, _ = x_prompt.shape
    bs, ts, _ = x_sample.shape
    mod = _ada_mod(jnp.concatenate([c_sample, c_prompt], axis=0), w_ada[l], b_ada[l])

    w_rt = jnp.zeros((ROUTE_ROWS, D), f32).at[0:N_GROUPS].set(w_group[l].T)
    b_rt = jnp.zeros((ROUTE_ROWS,), f32).at[0:N_GROUPS].set(b_group[l])
    for k in range(N_GROUPS):
        r0 = SUBLANES * (k + 1)
        w_rt = w_rt.at[r0:r0 + EPG].set(w_route[l][:, k * EPG:(k + 1) * EPG].T)
        b_rt = b_rt.at[r0:r0 + EPG].set(b_route[l][k * EPG:(k + 1) * EPG])
    tri = jnp.triu(jnp.ones((MOE_ROWS, MOE_ROWS), bf16), 1)

    mix_wts = (
        w_in[l].astype(bf16),
        w_pool[l].astype(bf16),
        pool_scale[l].reshape(1, D_POOL),
        w_conv[l],
        b_conv[l].reshape(1, D_LRU),
        jnp.stack([_pair_blocks(w_a[l]), _pair_blocks(w_x[l])]).astype(bf16),
        jnp.concatenate([b_a[l], b_x[l]]).reshape(1, 2 * D_LRU),
        lru_lambda[l].reshape(1, D_LRU),
        w_out[l].astype(bf16),
        ln1_g[l].reshape(1, D),
        ln1_b[l].reshape(1, D),
    )
    moe_wts = (
        w_rt.astype(bf16),
        b_rt.reshape(ROUTE_ROWS, 1),
        tri,
        w_gate[l].astype(bf16),
        w_up[l].astype(bf16),
        w_down[l].astype(bf16),
        ln2_g[l].reshape(1, D),
        ln2_b[l].reshape(1, D),
    )

    tb_p = MOE_ROWS // bp
    zp = jnp.zeros((POOL_BUF, bp, D_POOL), f32)
    zc = jnp.zeros((CONV_W - 1, bp, D_LRU), f32)
    zh = jnp.zeros((bp, D_LRU), f32)
    x1p, pool_p, conv_p, lru_p = _run_mix(x_prompt, mod, bs // bp, zp, zc, zh, 0, bp, tb_p, mix_wts)
    yp = _run_moe(x1p, mod, bs // bp, 1, bp, tb_p, bp, tp, moe_wts)

    n_prev_s = min(PAST_LEN, POOL_BUF)
    nb_s = bs // 2
    x1s, pool_s, conv_s, lru_s = _run_mix(
        x_sample, mod, 0, state_pool[l].transpose(1, 0, 2), state_conv[l].transpose(1, 0, 2),
        state_lru[l], n_prev_s, nb_s, ts, mix_wts)
    ys = _run_moe(x1s, mod, 0, 2, nb_s, ts, bs, ts, moe_wts)

    tr = lambda a: a.transpose(1, 0, 2)[None]
    return (yp, ys, tr(pool_p), tr(conv_p), lru_p[None], tr(pool_s), tr(conv_s), lru_s[None])
```

```python
import functools
import math

import jax
import jax.numpy as jnp
from jax import lax
from jax.experimental import pallas as pl
from jax.experimental.pallas import tpu as pltpu

D = 1024
D_POOL = 512
D_LRU = 512
D_IN = D_POOL + 2 * D_LRU
POOL_WINDOWS = (2, 4, 8, 16)
POOL_GROUP = 128
POOL_BUF = 15
CONV_W = 4
LRU_C = 8.0
N_GROUPS = 4
EPG = 4
D_EXPERT = 256
DEPTH = 1
ALPHA = (2.0 * DEPTH) ** 0.25
LN_EPS = 1e-5
PAST_LEN = 16384

LANES = 128
SUBLANES = 8
N_SLABS = D // LANES
ROUTE_ROWS = 48
MOE_ROWS = 1024
CHUNK = 288
CHUNK_PAD = 320
ROW_LO = 4
ROW_LO2 = 8
ROW_GSEL = 12
ROW_RANK = 13
N_RB = 4
GPS = 2
FIN_ROWS = 256
MIX_BLOCK_ROWS = 512
VMEM_LIMIT = 60 * 1024 * 1024

f32 = jnp.float32
bf16 = jnp.bfloat16
i32 = jnp.int32


def _dot(a, b):
    return jnp.dot(a, b, preferred_element_type=f32)


LOG2E = 1.0 / math.log(2.0)


def _sigmoid(x):
    return 1.0 / (1.0 + jnp.exp2(x * (-LOG2E)))


def _gelu_tanh(x):
    c = math.sqrt(2.0 / math.pi)
    hx = 0.5 * x
    return hx + hx * jnp.tanh(x * (c + (c * 0.044715) * (x * x)))


def _layer_norm(v, g, b):
    mu = jnp.mean(v, axis=-1, keepdims=True)
    c = v - mu
    var = jnp.mean(c * c, axis=-1, keepdims=True)
    return c * lax.rsqrt(var + LN_EPS) * g + b


def _slab_pitch(tb):
    return tb + SUBLANES if (tb // SUBLANES) % 2 == 0 else tb


def _ada_body(c_ref, w_ref, b_ref, o_ref):
    o_ref[...] = _dot(c_ref[...].astype(bf16), w_ref[...].astype(bf16)) + b_ref[...]


def _ada_mod(c_all, w_ada, b_ada):
    n = c_all.shape[0]
    bn = 1024
    return pl.pallas_call(
        _ada_body,
        grid=(6 * D // bn,),
        in_specs=[
            pl.BlockSpec((n, D), lambda j: (0, 0)),
            pl.BlockSpec((D, bn), lambda j: (0, j)),
            pl.BlockSpec((1, bn), lambda j: (0, j)),
        ],
        out_specs=pl.BlockSpec((n, bn), lambda j: (0, j)),
        out_shape=jax.ShapeDtypeStruct((n, 6 * D), f32),
        name="ada_mod",
    )(c_all, w_ada, b_ada.reshape(1, 6 * D))


def _mix_body(nb, tb, pitch, n_prev, n_tt,
              x_ref, mod_ref, pool0_ref, conv0_ref, h0_ref,
              w_in_ref, w_pool_ref, pscale_ref, w_conv_ref, b_conv_ref,
              w_ax_ref, b_ax_ref, lam_ref, w_out_ref, ln1g_ref, ln1b_ref,
              x1_ref, npool_ref, nconv_ref, nh_ref,
              slab, xt, ycat, zpool, zconv, a_s, b_s, gl_s, h_s):
    tm = nb * tb
    n_blk = max(1, tm // MIX_BLOCK_ROWS)
    tbb = tb // n_blk
    rb = tbb * nb
    tq = tbb // N_RB
    rq = tq * nb
    ti = pl.program_id(0) % n_tt

    def mod_part(k):
        return mod_ref[:, pl.ds(k * D, D)]

    @pl.when(ti == 0)
    def _init_state():
        zpool[pl.ds(0, POOL_BUF * nb), :] = pool0_ref[...].reshape(POOL_BUF * nb, D_POOL)
        zconv[pl.ds(0, (CONV_W - 1) * nb), :] = conv0_ref[...].reshape((CONV_W - 1) * nb, D_LRU)
        h_s[...] = h0_ref[...]

    for b in range(nb):
        for j in range(N_SLABS):
            slab[j, pl.ds(b * pitch, tb), :] = x_ref[b, :, pl.ds(j * LANES, LANES)]

    sh1 = mod_part(0)
    sc1 = mod_part(1)
    g1 = mod_part(2)
    nl = -lam_ref[...]
    softplus = jnp.maximum(nl, 0.0) + jnp.log(1.0 + jnp.exp(-jnp.abs(nl)))
    log2a_unit = (-LRU_C * LOG2E) * softplus
    row = lax.broadcasted_iota(i32, (rb, LANES), 0)
    t_loc = lax.shift_right_logical(row, int(math.log2(nb)))
    h = h_s[...]

    for blk in range(n_blk):
        t0 = blk * tbb
        r0 = t0 * nb
        brows = pl.ds(r0, rb)
        for t in range(t0, t0 + tbb):
            for j in range(N_SLABS):
                xt[t, :, pl.ds(j * LANES, LANES)] = slab[j, pl.ds(t, nb, stride=pitch), :]

        for r in range(N_RB):
            u = xt[pl.ds(t0 + r * tq, tq)] * (1.0 + sc1)[None] + sh1[None]
            ycat[pl.ds(r0 + r * rq, rq), :] = u.reshape(rq, D).astype(bf16)
        ub = ycat[brows, :]
        zpool[pl.ds(POOL_BUF * nb + r0, rb), :] = _dot(ub, w_in_ref[:, pl.ds(0, D_POOL)])
        zconv[pl.ds((CONV_W - 1) * nb + r0, rb), :] = _dot(ub, w_in_ref[:, pl.ds(D_POOL, D_LRU)])
        half = D_LRU // 2
        for hpart in range(2):
            cols = pl.ds(D_POOL + D_LRU + hpart * half, half)
            gl_s[brows, pl.ds(hpart * half, half)] = _gelu_tanh(_dot(ub, w_in_ref[:, cols]))

        t_glob = (ti * tb + t0 + t_loc + (1 + n_prev)).astype(f32)
        for c, w in enumerate(POOL_WINDOWS):
            lanes = pl.ds(c * LANES, LANES)
            s = zpool[pl.ds((POOL_BUF + 1 - w) * nb + r0, rb + (w - 1) * nb), lanes]
            step = 1
            while step < w:
                s = s[step * nb:] + s[:-step * nb]
                step *= 2
            cnt = jnp.minimum(f32(w), t_glob)
            dlt = s / cnt - zpool[pl.ds(POOL_BUF * nb + r0, rb), lanes]
            yp = _dot(dlt.astype(bf16), w_pool_ref[c]) * pscale_ref[:, lanes]
            ycat[brows, lanes] = yp.astype(bf16)

        for j in range(D_LRU // LANES):
            lanes = pl.ds(j * LANES, LANES)
            xc = b_conv_ref[:, lanes] + zconv[pl.ds(r0, rb), lanes] * w_conv_ref[0:1, lanes]
            for k in range(1, CONV_W):
                xc = xc + zconv[pl.ds(k * nb + r0, rb), lanes] * w_conv_ref[k:k + 1, lanes]
            xcb = xc.astype(bf16)
            r = _sigmoid(_dot(xcb, w_ax_ref[0, j]) + b_ax_ref[:, lanes])
            ig = _sigmoid(_dot(xcb, w_ax_ref[1, j]) + b_ax_ref[:, pl.ds(D_LRU + j * LANES, LANES)])
            a = jnp.exp2(r * log2a_unit[:, j * LANES:(j + 1) * LANES])
            a_s[brows, lanes] = a
            om = jnp.maximum(1.0 - a * a, 0.0)
            mult = jnp.where(om > 0.0, om * lax.rsqrt(om), 0.0)
            b_s[brows, lanes] = mult * ig * xc

        for t in range(t0, t0 + tbb):
            rows = pl.ds(t * nb, nb)
            h = a_s[rows, :] * h + b_s[rows, :]
            b_s[rows, :] = h
        ycat[brows, pl.ds(D_POOL, D_LRU)] = (b_s[brows, :] * gl_s[brows, :]).astype(bf16)

        yc = ycat[brows, :]
        cb = D // N_RB
        for j in range(N_RB):
            cols = pl.ds(j * cb, cb)
            mix = _dot(yc, w_out_ref[:, cols]).reshape(tbb, nb, cb)
            xt[pl.ds(t0, tbb), :, cols] = (ALPHA * xt[pl.ds(t0, tbb), :, cols]
                                           + (1.0 + g1[:, j * cb:(j + 1) * cb])[None] * mix)
        for r in range(N_RB):
            xn = _layer_norm(xt[pl.ds(t0 + r * tq, tq)], ln1g_ref[...][None], ln1b_ref[...][None])
            x1_ref[pl.ds(r0 + r * rq, rq), :] = xn.reshape(rq, D)

    h_s[...] = h
    zpool[pl.ds(0, POOL_BUF * nb), :] = zpool[pl.ds(tm, POOL_BUF * nb), :]
    zconv[pl.ds(0, (CONV_W - 1) * nb), :] = zconv[pl.ds(tm, (CONV_W - 1) * nb), :]

    @pl.when(ti == n_tt - 1)
    def _emit_state():
        npool_ref[...] = zpool[pl.ds(0, POOL_BUF * nb), :].reshape(POOL_BUF, nb, D_POOL)
        nconv_ref[...] = zconv[pl.ds(0, (CONV_W - 1) * nb), :].reshape(CONV_W - 1, nb, D_LRU)
        nh_ref[...] = h_s[...]


def _run_mix(x, mod, mod_blk0, pool0, conv0, h0, n_prev, nb, tb, wts):
    bsz, t_len, _ = x.shape
    n_sb = bsz // nb
    n_tt = t_len // tb
    tm = nb * tb
    pitch = _slab_pitch(tb)
    st = lambda q: (q // n_tt, q % n_tt)
    const2 = lambda q: (0, 0)
    const3 = lambda q: (0, 0, 0)
    const4 = lambda q: (0, 0, 0, 0)
    single = pl.Buffered(1)
    in_specs = [
        pl.BlockSpec((nb, tb, D), lambda q: (*st(q), 0)),
        pl.BlockSpec((nb, 6 * D), lambda q: (mod_blk0 + st(q)[0], 0)),
        pl.BlockSpec((POOL_BUF, nb, D_POOL), lambda q: (0, st(q)[0], 0)),
        pl.BlockSpec((CONV_W - 1, nb, D_LRU), lambda q: (0, st(q)[0], 0)),
        pl.BlockSpec((nb, D_LRU), lambda q: (st(q)[0], 0)),
        pl.BlockSpec((D, D_IN), const2, pipeline_mode=single),
        pl.BlockSpec((4, POOL_GROUP, POOL_GROUP), const3),
        pl.BlockSpec((1, D_POOL), const2),
        pl.BlockSpec((CONV_W, D_LRU), const2),
        pl.BlockSpec((1, D_LRU), const2),
        pl.BlockSpec((2, D_LRU // LANES, LANES, LANES), const4),
        pl.BlockSpec((1, 2 * D_LRU), const2),
        pl.BlockSpec((1, D_LRU), const2),
        pl.BlockSpec((D, D), const2, pipeline_mode=single),
        pl.BlockSpec((1, D), const2),
        pl.BlockSpec((1, D), const2),
    ]
    out_specs = [
        pl.BlockSpec((tm, D), lambda q: (q, 0)),
        pl.BlockSpec((POOL_BUF, nb, D_POOL), lambda q: (0, st(q)[0], 0)),
        pl.BlockSpec((CONV_W - 1, nb, D_LRU), lambda q: (0, st(q)[0], 0)),
        pl.BlockSpec((nb, D_LRU), lambda q: (st(q)[0], 0)),
    ]
    out_shape = [
        jax.ShapeDtypeStruct((bsz * t_len, D), f32),
        jax.ShapeDtypeStruct((POOL_BUF, bsz, D_POOL), f32),
        jax.ShapeDtypeStruct((CONV_W - 1, bsz, D_LRU), f32),
        jax.ShapeDtypeStruct((bsz, D_LRU), f32),
    ]
    scratch = [
        pltpu.VMEM((N_SLABS, nb * pitch, LANES), f32),
        pltpu.VMEM((tb, nb, D), f32),
        pltpu.VMEM((tm, D), bf16),
        pltpu.VMEM(((tb + POOL_BUF) * nb, D_POOL), f32),
        pltpu.VMEM(((tb + CONV_W - 1) * nb, D_LRU), f32),
        pltpu.VMEM((tm, D_LRU), f32),
        pltpu.VMEM((tm, D_LRU), f32),
        pltpu.VMEM((tm, D_LRU), f32),
        pltpu.VMEM((nb, D_LRU), f32),
    ]
    body = functools.partial(_mix_body, nb, tb, pitch, n_prev, n_tt)
    return pl.pallas_call(
        body,
        grid=(n_sb * n_tt,),
        in_specs=in_specs,
        out_specs=out_specs,
        out_shape=out_shape,
        scratch_shapes=scratch,
        compiler_params=pltpu.CompilerParams(
            dimension_semantics=("arbitrary",),
            vmem_limit_bytes=VMEM_LIMIT),
        name=f"mix_nb{nb}_tb{tb}",
    )(x, mod, pool0, conv0, h0, *wts)


def _moe_body(n_half, nb, tb, pitch,
              x1_ref, mod_ref, w_rt_ref, b_rt_ref, tri_ref, wg_ref, wu_ref, wd_ref, ln2g_ref, ln2b_ref,
              y_ref,
              u2, tok_t, tok3, rows_s, cnt_s, ys_all, yacc, slab):
    tm = MOE_ROWS
    hm = nb * tb
    tq = tb // N_RB
    rq = tq * nb
    gs = pl.program_id(1)
    last_gs = N_GROUPS // GPS - 1

    def mod_part(h, k):
        return mod_ref[pl.ds(h * nb, nb), pl.ds(k * D, D)]

    @pl.when(gs == 0)
    def _route():
        for h in range(n_half):
            sh2 = mod_part(h, 3)
            sc2 = mod_part(h, 4)
            for r in range(N_RB):
                rows = pl.ds(h * hm + r * rq, rq)
                v = x1_ref[rows, :].reshape(tq, nb, D) * (1.0 + sc2)[None] + sh2[None]
                u2[rows, :] = v.reshape(rq, D).astype(bf16)
        lt = lax.dot_general(w_rt_ref[...], u2[...], (((1,), (1,)), ((), ())),
                             preferred_element_type=f32) + b_rt_ref[...]
        gl = [lt[k:k + 1, :] for k in range(N_GROUPS)]
        best = gl[0]
        gsel = jnp.zeros_like(best, dtype=i32)
        for k in range(1, N_GROUPS):
            better = gl[k] > best
            best = jnp.where(better, gl[k], best)
            gsel = jnp.where(better, k, gsel)
        denom = jnp.exp(gl[0] - best)
        for k in range(1, N_GROUPS):
            denom = denom + jnp.exp(gl[k] - best)
        p_sel = 1.0 / denom
        es = []
        for j in range(EPG):
            v_j = lt[SUBLANES + j:SUBLANES + j + 1, :]
            for k in range(1, N_GROUPS):
                r0 = SUBLANES * (k + 1) + j
                v_j = jnp.where(gsel == k, lt[r0:r0 + 1, :], v_j)
            es.append(v_j)
        v1 = es[0]
        i1 = jnp.zeros_like(gsel)
        for j in range(1, EPG):
            better = es[j] > v1
            v1 = jnp.where(better, es[j], v1)
            i1 = jnp.where(better, j, i1)
        v2 = jnp.full_like(v1, -jnp.inf)
        i2 = jnp.full_like(gsel, -1)
        for j in range(EPG):
            cand = jnp.logical_and(i1 != j, jnp.logical_or(i2 < 0, es[j] > v2))
            v2 = jnp.where(cand, es[j], v2)
            i2 = jnp.where(cand, j, i2)
        e21 = jnp.exp(v2 - v1)
        w1 = p_sel / (1.0 + e21)
        w2 = p_sel * e21 / (1.0 + e21)
        rid8 = lax.broadcasted_iota(i32, (SUBLANES, tm), 0)
        onehot = jnp.where(rid8 == gsel, 1.0, 0.0)
        blocks = [onehot[:, j * LANES:(j + 1) * LANES] for j in range(tm // LANES)]
        inner = _dot(jnp.concatenate(blocks, axis=0).astype(bf16), tri_ref[...])
        before = jnp.zeros((SUBLANES, 1), f32)
        prefs = []
        for j, blk in enumerate(blocks):
            prefs.append(inner[j * SUBLANES:(j + 1) * SUBLANES, :] + before)
            before = before + jnp.sum(blk, axis=1, keepdims=True)
        pref = jnp.concatenate(prefs, axis=1)
        rank = jnp.sum(onehot * pref, axis=0, keepdims=True)
        n_max = 0
        for k in range(N_GROUPS):
            n_k = jnp.sum(jnp.where(gsel == k, 1.0, 0.0)).astype(i32)
            cnt_s[k] = n_k
            n_max = jnp.maximum(n_max, n_k)
        cnt_s[N_GROUPS] = (n_max > CHUNK).astype(i32)
        rows_s[0:1, :] = gsel
        rows_s[1:2, :] = rank.astype(i32)
        rid = lax.broadcasted_iota(i32, (LANES, tm), 0)
        rec = jnp.zeros((LANES, tm), f32)
        for j in range(EPG):
            wj = jnp.where(i1 == j, w1, jnp.where(i2 == j, w2, 0.0))
            hi = wj.astype(bf16).astype(f32)
            mid = (wj - hi).astype(bf16).astype(f32)
            lo = (wj - hi - mid).astype(bf16).astype(f32)
            rec = jnp.where(rid == j, hi, rec)
            rec = jnp.where(rid == ROW_LO + j, mid, rec)
            rec = jnp.where(rid == ROW_LO2 + j, lo, rec)
        rec = jnp.where(rid == ROW_GSEL, gsel.astype(f32), rec)
        rec = jnp.where(rid == ROW_RANK, rank, rec)
        rt = rec.T
        tok_t[...] = rt
        tok3[...] = rt.astype(bf16)

    def sorted_experts(gi, base):
        g = gs * GPS + gi
        gsel_row = rows_s[0:1, :]
        rank_row = rows_s[1:2, :]
        jj = lax.broadcasted_iota(i32, (CHUNK, tm), 0) + base
        pm = jnp.where(jnp.logical_and(gsel_row == g, rank_row == jj), 1.0, 0.0).astype(bf16)
        xs = _dot(pm, u2[...]).astype(bf16)
        cs3 = _dot(pm, tok3[...])
        cs = (cs3 + pltpu.roll(cs3, LANES - ROW_LO, axis=1)
              + pltpu.roll(cs3, LANES - ROW_LO2, axis=1))
        acts = []
        for e in range(EPG):
            hg = _dot(xs, wg_ref[gi * EPG + e])
            hu = _dot(xs, wu_ref[gi * EPG + e])
            acts.append((hg * _sigmoid(hg) * hu * cs[:, e:e + 1]).astype(bf16))
        hb = jnp.concatenate(acts, axis=1)
        wd_g = wd_ref[pl.ds(gi * EPG, EPG)].reshape(EPG * D_EXPERT, D)
        return _dot(hb, wd_g).astype(bf16)

    for gi in range(GPS):
        first_row = pl.multiple_of((gs * GPS + gi) * CHUNK_PAD, CHUNK_PAD)
        ys_all[pl.ds(first_row, CHUNK), :] = sorted_experts(gi, 0)

    @pl.when(gs == 0)
    def _zero_pad_rows():
        for k in range(N_GROUPS):
            ys_all[pl.ds(k * CHUNK_PAD + CHUNK, CHUNK_PAD - CHUNK), :] = jnp.zeros((CHUNK_PAD - CHUNK, D), bf16)

    has_overflow = cnt_s[N_GROUPS] > 0

    @pl.when(jnp.logical_and(gs == 0, has_overflow))
    def _zero_acc():
        yacc[...] = jnp.zeros((tm, D), f32)

    for gi in range(GPS):
        g = gs * GPS + gi

        def overflow(c, carry, gi=gi, g=g):
            base = c * CHUNK
            ys = sorted_experts(gi, base)
            rec = tok_t[...]
            gsel_col = rec[:, ROW_GSEL:ROW_GSEL + 1]
            rank_col = rec[:, ROW_RANK:ROW_RANK + 1]
            jl = (lax.broadcasted_iota(i32, (tm, CHUNK), 1) + base).astype(f32)
            pt = jnp.where(jnp.logical_and(gsel_col == g.astype(f32), rank_col == jl), 1.0, 0.0).astype(bf16)
            yacc[...] += _dot(pt, ys)
            return carry

        lax.fori_loop(1, pl.cdiv(cnt_s[g], CHUNK), overflow, 0)

    def finish(with_acc):
        tqf = FIN_ROWS // nb
        jl = lax.broadcasted_iota(i32, (FIN_ROWS, N_GROUPS * CHUNK_PAD), 1).astype(f32)
        for h in range(n_half):
            g2 = mod_part(h, 5)
            for r in range(hm // FIN_ROWS):
                rows = pl.ds(h * hm + r * FIN_ROWS, FIN_ROWS)
                rec = tok_t[rows, :]
                gsel_col = rec[:, ROW_GSEL:ROW_GSEL + 1]
                rank_col = rec[:, ROW_RANK:ROW_RANK + 1]
                key = jnp.where(rank_col < f32(CHUNK), gsel_col * f32(CHUNK_PAD) + rank_col, -1.0)
                pt = jnp.where(key == jl, 1.0, 0.0).astype(bf16)
                y = _dot(pt, ys_all[...])
                if with_acc:
                    y = y + yacc[rows, :]
                v = ALPHA * x1_ref[rows, :] + ((1.0 + g2)[None] * y.reshape(tqf, nb, D)).reshape(FIN_ROWS, D)
                yn = _layer_norm(v, ln2g_ref[...], ln2b_ref[...])
                for tl in range(tqf):
                    for j in range(N_SLABS):
                        slab[j, pl.ds(r * tqf + tl, nb, stride=pitch), :] = (
                            yn[tl * nb:(tl + 1) * nb, j * LANES:(j + 1) * LANES])
            for b in range(nb):
                for j in range(N_SLABS):
                    y_ref[h * nb + b, :, pl.ds(j * LANES, LANES)] = slab[j, pl.ds(b * pitch, tb), :]

    @pl.when(jnp.logical_and(gs == last_gs, jnp.logical_not(has_overflow)))
    def _finish():
        finish(False)

    @pl.when(jnp.logical_and(gs == last_gs, has_overflow))
    def _finish_with_overflow():
        finish(True)


def _run_moe(x1, mod, mod_blk0, n_half, nb, tb, out_seqs, out_len, wts):
    n_tiles = x1.shape[0] // MOE_ROWS
    assert n_half * nb * tb == MOE_ROWS
    seq_per_tile = n_half * nb
    n_tt = out_len // tb
    pitch = _slab_pitch(tb)
    st = lambda q: (q // n_tt, q % n_tt)
    const2 = lambda q, g: (0, 0)
    grp3 = lambda q, g: (g, 0, 0)
    single = pl.Buffered(1)
    in_specs = [
        pl.BlockSpec((MOE_ROWS, D), lambda q, g: (q, 0)),
        pl.BlockSpec((seq_per_tile, 6 * D), lambda q, g: (mod_blk0 + st(q)[0], 0)),
        pl.BlockSpec((ROUTE_ROWS, D), const2),
        pl.BlockSpec((ROUTE_ROWS, 1), const2),
        pl.BlockSpec((LANES, LANES), const2),
        pl.BlockSpec((GPS * EPG, D, D_EXPERT), grp3),
        pl.BlockSpec((GPS * EPG, D, D_EXPERT), grp3),
        pl.BlockSpec((GPS * EPG, D_EXPERT, D), grp3),
        pl.BlockSpec((1, D), const2),
        pl.BlockSpec((1, D), const2),
    ]
    scratch = [
        pltpu.VMEM((MOE_ROWS, D), bf16),
        pltpu.VMEM((MOE_ROWS, LANES), f32),
        pltpu.VMEM((MOE_ROWS, LANES), bf16),
        pltpu.VMEM((SUBLANES, MOE_ROWS), i32),
        pltpu.SMEM((N_GROUPS + 1,), i32),
        pltpu.VMEM((N_GROUPS * CHUNK_PAD, D), bf16),
        pltpu.VMEM((MOE_ROWS, D), f32),
        pltpu.VMEM((N_SLABS, nb * pitch, LANES), f32),
    ]
    body = functools.partial(_moe_body, n_half, nb, tb, pitch)
    return pl.pallas_call(
        body,
        grid=(n_tiles, N_GROUPS // GPS),
        in_specs=in_specs,
        out_specs=pl.BlockSpec((seq_per_tile, tb, D), lambda q, g: (*st(q), 0)),
        out_shape=jax.ShapeDtypeStruct((out_seqs, out_len, D), f32),
        scratch_shapes=scratch,
        compiler_params=pltpu.CompilerParams(
            dimension_semantics=("arbitrary", "arbitrary"),
            vmem_limit_bytes=VMEM_LIMIT),
        name=f"moe_nb{nb}_tb{tb}",
    )(x1, mod, *wts)


def _pair_blocks(w):
    n, k, _ = w.shape
    wp = w.reshape(n // 2, 2, k, k)
    eye = jnp.eye(2, dtype=w.dtype)
    return jnp.einsum('phij,hg->phigj', wp, eye).reshape(n // 2, 2 * k, 2 * k)


def kernel(x_prompt, x_sample, c_prompt, c_sample, state_pool, state_conv, state_lru, w_ada, b_ada, w_in, w_pool, pool_scale, w_conv, b_conv, w_a, b_a, w_x, b_x, lru_lambda, w_out, ln1_g, ln1_b, w_group, b_group, w_route, b_route, w_gate, w_up, w_down, ln2_g, ln2_b):
    l = 0
    bp, tp, _ = x_prompt.shape
    bs, ts, _ = x_sample.shape
    mod = _ada_mod(jnp.concatenate([c_sample, c_prompt], axis=0), w_ada[l], b_ada[l])

    w_rt = jnp.zeros((ROUTE_ROWS, D), f32).at[0:N_GROUPS].set(w_group[l].T)
    b_rt = jnp.zeros((ROUTE_ROWS,), f32).at[0:N_GROUPS].set(b_group[l])
    for k in range(N_GROUPS):
        r0 = SUBLANES * (k + 1)
        w_rt = w_rt.at[r0:r0 + EPG].set(w_route[l][:, k * EPG:(k + 1) * EPG].T)
        b_rt = b_rt.at[r0:r0 + EPG].set(b_route[l][k * EPG:(k + 1) * EPG])
    tri = jnp.triu(jnp.ones((LANES, LANES), bf16), 1)

    mix_wts = (
        w_in[l].astype(bf16),
        w_pool[l].astype(bf16),
        pool_scale[l].reshape(1, D_POOL),
        w_conv[l],
        b_conv[l].reshape(1, D_LRU),
        jnp.stack([_pair_blocks(w_a[l]), _pair_blocks(w_x[l])]).astype(bf16),
        jnp.concatenate([b_a[l], b_x[l]]).reshape(1, 2 * D_LRU),
        lru_lambda[l].reshape(1, D_LRU),
        w_out[l].astype(bf16),
        ln1_g[l].reshape(1, D),
        ln1_b[l].reshape(1, D),
    )
    moe_wts = (
        w_rt.astype(bf16),
        b_rt.reshape(ROUTE_ROWS, 1),
        tri,
        w_gate[l].astype(bf16),
        w_up[l].astype(bf16),
        w_down[l].astype(bf16),
        ln2_g[l].reshape(1, D),
        ln2_b[l].reshape(1, D),
    )

    tb_p = MOE_ROWS // bp
    zp = jnp.zeros((POOL_BUF, bp, D_POOL), f32)
    zc = jnp.zeros((CONV_W - 1, bp, D_LRU), f32)
    zh = jnp.zeros((bp, D_LRU), f32)
    x1p, pool_p, conv_p, lru_p = _run_mix(x_prompt, mod, bs // bp, zp, zc, zh, 0, bp, tb_p, mix_wts)
    yp = _run_moe(x1p, mod, bs // bp, 1, bp, tb_p, bp, tp, moe_wts)

    n_prev_s = min(PAST_LEN, POOL_BUF)
    nb_s = bs // 2
    x1s, pool_s, conv_s, lru_s = _run_mix(
        x_sample, mod, 0, state_pool[l].transpose(1, 0, 2), state_conv[l].transpose(1, 0, 2),
        state_lru[l], n_prev_s, nb_s, ts, mix_wts)
    ys = _run_moe(x1s, mod, 0, 2, nb_s, ts, bs, ts, moe_wts)

    tr = lambda a: a.transpose(1, 0, 2)[None]
    return (yp, ys, tr(pool_p), tr(conv_p), lru_p[None], tr(pool_s), tr(conv_s), lru_s[None])
```

```python
import functools
import math

import jax
import jax.numpy as jnp
from jax import lax
from jax.experimental import pallas as pl
from jax.experimental.pallas import tpu as pltpu

D = 1024
D_POOL = 512
D_LRU = 512
D_IN = D_POOL + 2 * D_LRU
POOL_WINDOWS = (2, 4, 8, 16)
POOL_GROUP = 128
POOL_BUF = 15
CONV_W = 4
LRU_C = 8.0
N_GROUPS = 4
EPG = 4
D_EXPERT = 256
DEPTH = 1
ALPHA = (2.0 * DEPTH) ** 0.25
LN_EPS = 1e-5
PAST_LEN = 16384

LANES = 128
SUBLANES = 8
N_SLABS = D // LANES
ROUTE_ROWS = 48
MOE_ROWS = 1024
CHUNK = 288
CHUNK_PAD = 320
ROW_LO = 4
ROW_LO2 = 8
ROW_GSEL = 12
ROW_RANK = 13
N_RB = 4
GPS = 2
FIN_ROWS = 512
MIX_BLOCK_ROWS = 1024
VMEM_LIMIT = 60 * 1024 * 1024

f32 = jnp.float32
bf16 = jnp.bfloat16
i32 = jnp.int32


def _dot(a, b):
    return jnp.dot(a, b, preferred_element_type=f32)


LOG2E = 1.0 / math.log(2.0)


def _sigmoid(x):
    return 1.0 / (1.0 + jnp.exp2(x * (-LOG2E)))


def _gelu_tanh(x):
    c = math.sqrt(2.0 / math.pi)
    hx = 0.5 * x
    return hx + hx * jnp.tanh(x * (c + (c * 0.044715) * (x * x)))


def _layer_norm(v, g, b):
    mu = jnp.mean(v, axis=-1, keepdims=True)
    c = v - mu
    var = jnp.mean(c * c, axis=-1, keepdims=True)
    return c * lax.rsqrt(var + LN_EPS) * g + b


def _slab_pitch(tb):
    return tb + SUBLANES if (tb // SUBLANES) % 2 == 0 else tb


def _ada_body(c_ref, w_ref, b_ref, o_ref):
    o_ref[...] = _dot(c_ref[...].astype(bf16), w_ref[...].astype(bf16)) + b_ref[...]


def _ada_mod(c_all, w_ada, b_ada):
    n = c_all.shape[0]
    bn = 1024
    return pl.pallas_call(
        _ada_body,
        grid=(6 * D // bn,),
        in_specs=[
            pl.BlockSpec((n, D), lambda j: (0, 0)),
            pl.BlockSpec((D, bn), lambda j: (0, j)),
            pl.BlockSpec((1, bn), lambda j: (0, j)),
        ],
        out_specs=pl.BlockSpec((n, bn), lambda j: (0, j)),
        out_shape=jax.ShapeDtypeStruct((n, 6 * D), f32),
        name="ada_mod",
    )(c_all, w_ada, b_ada.reshape(1, 6 * D))


def _mix_body(nb, tb, pitch, n_prev, n_tt,
              x_ref, mod_ref, pool0_ref, conv0_ref, h0_ref,
              w_in_ref, w_pool_ref, pscale_ref, w_conv_ref, b_conv_ref,
              w_ax_ref, b_ax_ref, lam_ref, w_out_ref, ln1g_ref, ln1b_ref,
              x1_ref, npool_ref, nconv_ref, nh_ref,
              slab, xt, ycat, zpool, zconv, a_s, b_s, gl_s, h_s):
    tm = nb * tb
    n_blk = max(1, tm // MIX_BLOCK_ROWS)
    tbb = tb // n_blk
    rb = tbb * nb
    tq = tbb // N_RB
    rq = tq * nb
    ti = pl.program_id(0) % n_tt

    def mod_part(k):
        return mod_ref[:, pl.ds(k * D, D)]

    @pl.when(ti == 0)
    def _init_state():
        zpool[pl.ds(0, POOL_BUF * nb), :] = pool0_ref[...].reshape(POOL_BUF * nb, D_POOL)
        zconv[pl.ds(0, (CONV_W - 1) * nb), :] = conv0_ref[...].reshape((CONV_W - 1) * nb, D_LRU)
        h_s[...] = h0_ref[...]

    for b in range(nb):
        for j in range(N_SLABS):
            slab[j, pl.ds(b * pitch, tb), :] = x_ref[b, :, pl.ds(j * LANES, LANES)]

    sh1 = mod_part(0)
    sc1 = mod_part(1)
    g1 = mod_part(2)
    nl = -lam_ref[...]
    softplus = jnp.maximum(nl, 0.0) + jnp.log(1.0 + jnp.exp(-jnp.abs(nl)))
    log2a_unit = (-LRU_C * LOG2E) * softplus
    row = lax.broadcasted_iota(i32, (rb, LANES), 0)
    t_loc = lax.shift_right_logical(row, int(math.log2(nb)))
    h = h_s[...]

    for blk in range(n_blk):
        t0 = blk * tbb
        r0 = t0 * nb
        brows = pl.ds(r0, rb)
        for t in range(t0, t0 + tbb):
            for j in range(N_SLABS):
                xt[t, :, pl.ds(j * LANES, LANES)] = slab[j, pl.ds(t, nb, stride=pitch), :]

        for r in range(N_RB):
            u = xt[pl.ds(t0 + r * tq, tq)] * (1.0 + sc1)[None] + sh1[None]
            ycat[pl.ds(r0 + r * rq, rq), :] = u.reshape(rq, D).astype(bf16)
        ub = ycat[brows, :]
        zpool[pl.ds(POOL_BUF * nb + r0, rb), :] = _dot(ub, w_in_ref[:, pl.ds(0, D_POOL)])
        zconv[pl.ds((CONV_W - 1) * nb + r0, rb), :] = _dot(ub, w_in_ref[:, pl.ds(D_POOL, D_LRU)])
        half = D_LRU // 2
        for hpart in range(2):
            cols = pl.ds(D_POOL + D_LRU + hpart * half, half)
            gl_s[brows, pl.ds(hpart * half, half)] = _gelu_tanh(_dot(ub, w_in_ref[:, cols]))

        t_glob = (ti * tb + t0 + t_loc + (1 + n_prev)).astype(f32)
        for c, w in enumerate(POOL_WINDOWS):
            lanes = pl.ds(c * LANES, LANES)
            s = zpool[pl.ds((POOL_BUF + 1 - w) * nb + r0, rb + (w - 1) * nb), lanes]
            step = 1
            while step < w:
                s = s[step * nb:] + s[:-step * nb]
                step *= 2
            cnt = jnp.minimum(f32(w), t_glob)
            dlt = s / cnt - zpool[pl.ds(POOL_BUF * nb + r0, rb), lanes]
            yp = _dot(dlt.astype(bf16), w_pool_ref[c]) * pscale_ref[:, lanes]
            ycat[brows, lanes] = yp.astype(bf16)

        for j in range(D_LRU // LANES):
            lanes = pl.ds(j * LANES, LANES)
            xc = b_conv_ref[:, lanes] + zconv[pl.ds(r0, rb), lanes] * w_conv_ref[0:1, lanes]
            for k in range(1, CONV_W):
                xc = xc + zconv[pl.ds(k * nb + r0, rb), lanes] * w_conv_ref[k:k + 1, lanes]
            xcb = xc.astype(bf16)
            r = _sigmoid(_dot(xcb, w_ax_ref[0, j]) + b_ax_ref[:, lanes])
            ig = _sigmoid(_dot(xcb, w_ax_ref[1, j]) + b_ax_ref[:, pl.ds(D_LRU + j * LANES, LANES)])
            a = jnp.exp2(r * log2a_unit[:, j * LANES:(j + 1) * LANES])
            a_s[brows, lanes] = a
            om = jnp.maximum(1.0 - a * a, 0.0)
            mult = jnp.where(om > 0.0, om * lax.rsqrt(om), 0.0)
            b_s[brows, lanes] = mult * ig * xc

        for t in range(t0, t0 + tbb):
            rows = pl.ds(t * nb, nb)
            h = a_s[rows, :] * h + b_s[rows, :]
            b_s[rows, :] = h
        ycat[brows, pl.ds(D_POOL, D_LRU)] = (b_s[brows, :] * gl_s[brows, :]).astype(bf16)

        yc = ycat[brows, :]
        cb = D // N_RB
        for j in range(N_RB):
            cols = pl.ds(j * cb, cb)
            mix = _dot(yc, w_out_ref[:, cols]).reshape(tbb, nb, cb)
            xt[pl.ds(t0, tbb), :, cols] = (ALPHA * xt[pl.ds(t0, tbb), :, cols]
                                           + (1.0 + g1[:, j * cb:(j + 1) * cb])[None] * mix)
        for r in range(N_RB):
            xn = _layer_norm(xt[pl.ds(t0 + r * tq, tq)], ln1g_ref[...][None], ln1b_ref[...][None])
            x1_ref[pl.ds(r0 + r * rq, rq), :] = xn.reshape(rq, D)

    h_s[...] = h
    zpool[pl.ds(0, POOL_BUF * nb), :] = zpool[pl.ds(tm, POOL_BUF * nb), :]
    zconv[pl.ds(0, (CONV_W - 1) * nb), :] = zconv[pl.ds(tm, (CONV_W - 1) * nb), :]

    @pl.when(ti == n_tt - 1)
    def _emit_state():
        npool_ref[...] = zpool[pl.ds(0, POOL_BUF * nb), :].reshape(POOL_BUF, nb, D_POOL)
        nconv_ref[...] = zconv[pl.ds(0, (CONV_W - 1) * nb), :].reshape(CONV_W - 1, nb, D_LRU)
        nh_ref[...] = h_s[...]


def _run_mix(x, mod, mod_blk0, pool0, conv0, h0, n_prev, nb, tb, wts):
    bsz, t_len, _ = x.shape
    n_sb = bsz // nb
    n_tt = t_len // tb
    tm = nb * tb
    pitch = _slab_pitch(tb)
    st = lambda q: (q // n_tt, q % n_tt)
    const2 = lambda q: (0, 0)
    const3 = lambda q: (0, 0, 0)
    const4 = lambda q: (0, 0, 0, 0)
    single = pl.Buffered(1)
    in_specs = [
        pl.BlockSpec((nb, tb, D), lambda q: (*st(q), 0)),
        pl.BlockSpec((nb, 6 * D), lambda q: (mod_blk0 + st(q)[0], 0)),
        pl.BlockSpec((POOL_BUF, nb, D_POOL), lambda q: (0, st(q)[0], 0)),
        pl.BlockSpec((CONV_W - 1, nb, D_LRU), lambda q: (0, st(q)[0], 0)),
        pl.BlockSpec((nb, D_LRU), lambda q: (st(q)[0], 0)),
        pl.BlockSpec((D, D_IN), const2, pipeline_mode=single),
        pl.BlockSpec((4, POOL_GROUP, POOL_GROUP), const3),
        pl.BlockSpec((1, D_POOL), const2),
        pl.BlockSpec((CONV_W, D_LRU), const2),
        pl.BlockSpec((1, D_LRU), const2),
        pl.BlockSpec((2, D_LRU // LANES, LANES, LANES), const4),
        pl.BlockSpec((1, 2 * D_LRU), const2),
        pl.BlockSpec((1, D_LRU), const2),
        pl.BlockSpec((D, D), const2, pipeline_mode=single),
        pl.BlockSpec((1, D), const2),
        pl.BlockSpec((1, D), const2),
    ]
    out_specs = [
        pl.BlockSpec((tm, D), lambda q: (q, 0)),
        pl.BlockSpec((POOL_BUF, nb, D_POOL), lambda q: (0, st(q)[0], 0)),
        pl.BlockSpec((CONV_W - 1, nb, D_LRU), lambda q: (0, st(q)[0], 0)),
        pl.BlockSpec((nb, D_LRU), lambda q: (st(q)[0], 0)),
    ]
    out_shape = [
        jax.ShapeDtypeStruct((bsz * t_len, D), f32),
        jax.ShapeDtypeStruct((POOL_BUF, bsz, D_POOL), f32),
        jax.ShapeDtypeStruct((CONV_W - 1, bsz, D_LRU), f32),
        jax.ShapeDtypeStruct((bsz, D_LRU), f32),
    ]
    scratch = [
        pltpu.VMEM((N_SLABS, nb * pitch, LANES), f32),
        pltpu.VMEM((tb, nb, D), f32),
        pltpu.VMEM((tm, D), bf16),
        pltpu.VMEM(((tb + POOL_BUF) * nb, D_POOL), f32),
        pltpu.VMEM(((tb + CONV_W - 1) * nb, D_LRU), f32),
        pltpu.VMEM((tm, D_LRU), f32),
        pltpu.VMEM((tm, D_LRU), f32),
        pltpu.VMEM((tm, D_LRU), f32),
        pltpu.VMEM((nb, D_LRU), f32),
    ]
    body = functools.partial(_mix_body, nb, tb, pitch, n_prev, n_tt)
    return pl.pallas_call(
        body,
        grid=(n_sb * n_tt,),
        in_specs=in_specs,
        out_specs=out_specs,
        out_shape=out_shape,
        scratch_shapes=scratch,
        compiler_params=pltpu.CompilerParams(
            dimension_semantics=("arbitrary",),
            vmem_limit_bytes=VMEM_LIMIT),
        name=f"mix_nb{nb}_tb{tb}",
    )(x, mod, pool0, conv0, h0, *wts)


def _moe_body(n_half, nb, tb, pitch,
              x1_ref, mod_ref, w_rt_ref, b_rt_ref, tri_ref, wg_ref, wu_ref, wd_ref, ln2g_ref, ln2b_ref,
              y_ref,
              u2, tok_t, tok3, rows_s, cnt_s, ys_all, yacc, slab):
    tm = MOE_ROWS
    hm = nb * tb
    tq = tb // N_RB
    rq = tq * nb
    gs = pl.program_id(1)
    last_gs = N_GROUPS // GPS - 1

    def mod_part(h, k):
        return mod_ref[pl.ds(h * nb, nb), pl.ds(k * D, D)]

    @pl.when(gs == 0)
    def _route():
        for h in range(n_half):
            sh2 = mod_part(h, 3)
            sc2 = mod_part(h, 4)
            for r in range(N_RB):
                rows = pl.ds(h * hm + r * rq, rq)
                v = x1_ref[rows, :].reshape(tq, nb, D) * (1.0 + sc2)[None] + sh2[None]
                u2[rows, :] = v.reshape(rq, D).astype(bf16)
        lt = lax.dot_general(w_rt_ref[...], u2[...], (((1,), (1,)), ((), ())),
                             preferred_element_type=f32) + b_rt_ref[...]
        gl = [lt[k:k + 1, :] for k in range(N_GROUPS)]
        best = gl[0]
        gsel = jnp.zeros_like(best, dtype=i32)
        for k in range(1, N_GROUPS):
            better = gl[k] > best
            best = jnp.where(better, gl[k], best)
            gsel = jnp.where(better, k, gsel)
        denom = jnp.exp(gl[0] - best)
        for k in range(1, N_GROUPS):
            denom = denom + jnp.exp(gl[k] - best)
        p_sel = 1.0 / denom
        es = []
        for j in range(EPG):
            v_j = lt[SUBLANES + j:SUBLANES + j + 1, :]
            for k in range(1, N_GROUPS):
                r0 = SUBLANES * (k + 1) + j
                v_j = jnp.where(gsel == k, lt[r0:r0 + 1, :], v_j)
            es.append(v_j)
        v1 = es[0]
        i1 = jnp.zeros_like(gsel)
        for j in range(1, EPG):
            better = es[j] > v1
            v1 = jnp.where(better, es[j], v1)
            i1 = jnp.where(better, j, i1)
        v2 = jnp.full_like(v1, -jnp.inf)
        i2 = jnp.full_like(gsel, -1)
        for j in range(EPG):
            cand = jnp.logical_and(i1 != j, jnp.logical_or(i2 < 0, es[j] > v2))
            v2 = jnp.where(cand, es[j], v2)
            i2 = jnp.where(cand, j, i2)
        e21 = jnp.exp(v2 - v1)
        w1 = p_sel / (1.0 + e21)
        w2 = p_sel * e21 / (1.0 + e21)
        rid8 = lax.broadcasted_iota(i32, (SUBLANES, tm), 0)
        onehot = jnp.where(rid8 == gsel, 1.0, 0.0)
        blocks = [onehot[:, j * LANES:(j + 1) * LANES] for j in range(tm // LANES)]
        inner = _dot(jnp.concatenate(blocks, axis=0).astype(bf16), tri_ref[...])
        before = jnp.zeros((SUBLANES, 1), f32)
        prefs = []
        for j, blk in enumerate(blocks):
            prefs.append(inner[j * SUBLANES:(j + 1) * SUBLANES, :] + before)
            before = before + jnp.sum(blk, axis=1, keepdims=True)
        pref = jnp.concatenate(prefs, axis=1)
        rank = jnp.sum(onehot * pref, axis=0, keepdims=True)
        n_max = 0
        for k in range(N_GROUPS):
            n_k = jnp.sum(jnp.where(gsel == k, 1.0, 0.0)).astype(i32)
            cnt_s[k] = n_k
            n_max = jnp.maximum(n_max, n_k)
        cnt_s[N_GROUPS] = (n_max > CHUNK).astype(i32)
        rows_s[0:1, :] = gsel
        rows_s[1:2, :] = rank.astype(i32)
        rid = lax.broadcasted_iota(i32, (LANES, tm), 0)
        rec = jnp.zeros((LANES, tm), f32)
        for j in range(EPG):
            wj = jnp.where(i1 == j, w1, jnp.where(i2 == j, w2, 0.0))
            hi = wj.astype(bf16).astype(f32)
            mid = (wj - hi).astype(bf16).astype(f32)
            lo = (wj - hi - mid).astype(bf16).astype(f32)
            rec = jnp.where(rid == j, hi, rec)
            rec = jnp.where(rid == ROW_LO + j, mid, rec)
            rec = jnp.where(rid == ROW_LO2 + j, lo, rec)
        rec = jnp.where(rid == ROW_GSEL, gsel.astype(f32), rec)
        rec = jnp.where(rid == ROW_RANK, rank, rec)
        rt = rec.T
        tok_t[...] = rt
        tok3[...] = rt.astype(bf16)

    def sorted_experts(gi, base):
        g = gs * GPS + gi
        gsel_row = rows_s[0:1, :]
        rank_row = rows_s[1:2, :]
        jj = lax.broadcasted_iota(i32, (CHUNK, tm), 0) + base
        pm = jnp.where(jnp.logical_and(gsel_row == g, rank_row == jj), 1.0, 0.0).astype(bf16)
        xs = _dot(pm, u2[...]).astype(bf16)
        cs3 = _dot(pm, tok3[...])
        cs = (cs3 + pltpu.roll(cs3, LANES - ROW_LO, axis=1)
              + pltpu.roll(cs3, LANES - ROW_LO2, axis=1))
        acts = []
        for e in range(EPG):
            hg = _dot(xs, wg_ref[gi * EPG + e])
            hu = _dot(xs, wu_ref[gi * EPG + e])
            acts.append((hg * _sigmoid(hg) * hu * cs[:, e:e + 1]).astype(bf16))
        hb = jnp.concatenate(acts, axis=1)
        wd_g = wd_ref[pl.ds(gi * EPG, EPG)].reshape(EPG * D_EXPERT, D)
        return _dot(hb, wd_g).astype(bf16)

    for gi in range(GPS):
        first_row = pl.multiple_of((gs * GPS + gi) * CHUNK_PAD, CHUNK_PAD)
        ys_all[pl.ds(first_row, CHUNK), :] = sorted_experts(gi, 0)

    @pl.when(gs == 0)
    def _zero_pad_rows():
        for k in range(N_GROUPS):
            ys_all[pl.ds(k * CHUNK_PAD + CHUNK, CHUNK_PAD - CHUNK), :] = jnp.zeros((CHUNK_PAD - CHUNK, D), bf16)

    has_overflow = cnt_s[N_GROUPS] > 0

    @pl.when(jnp.logical_and(gs == 0, has_overflow))
    def _zero_acc():
        yacc[...] = jnp.zeros((tm, D), f32)

    for gi in range(GPS):
        g = gs * GPS + gi

        def overflow(c, carry, gi=gi, g=g):
            base = c * CHUNK
            ys = sorted_experts(gi, base)
            rec = tok_t[...]
            gsel_col = rec[:, ROW_GSEL:ROW_GSEL + 1]
            rank_col = rec[:, ROW_RANK:ROW_RANK + 1]
            jl = (lax.broadcasted_iota(i32, (tm, CHUNK), 1) + base).astype(f32)
            pt = jnp.where(jnp.logical_and(gsel_col == g.astype(f32), rank_col == jl), 1.0, 0.0).astype(bf16)
            yacc[...] += _dot(pt, ys)
            return carry

        lax.fori_loop(1, pl.cdiv(cnt_s[g], CHUNK), overflow, 0)

    def finish(with_acc):
        tqf = FIN_ROWS // nb
        jl = lax.broadcasted_iota(i32, (FIN_ROWS, N_GROUPS * CHUNK_PAD), 1).astype(f32)
        for h in range(n_half):
            g2 = mod_part(h, 5)
            for r in range(hm // FIN_ROWS):
                rows = pl.ds(h * hm + r * FIN_ROWS, FIN_ROWS)
                rec = tok_t[rows, :]
                gsel_col = rec[:, ROW_GSEL:ROW_GSEL + 1]
                rank_col = rec[:, ROW_RANK:ROW_RANK + 1]
                key = jnp.where(rank_col < f32(CHUNK), gsel_col * f32(CHUNK_PAD) + rank_col, -1.0)
                pt = jnp.where(key == jl, 1.0, 0.0).astype(bf16)
                y = _dot(pt, ys_all[...])
                if with_acc:
                    y = y + yacc[rows, :]
                v = ALPHA * x1_ref[rows, :] + ((1.0 + g2)[None] * y.reshape(tqf, nb, D)).reshape(FIN_ROWS, D)
                yn = _layer_norm(v, ln2g_ref[...], ln2b_ref[...])
                for tl in range(tqf):
                    for j in range(N_SLABS):
                        slab[j, pl.ds(r * tqf + tl, nb, stride=pitch), :] = (
                            yn[tl * nb:(tl + 1) * nb, j * LANES:(j + 1) * LANES])
            for b in range(nb):
                for j in range(N_SLABS):
                    y_ref[h * nb + b, :, pl.ds(j * LANES, LANES)] = slab[j, pl.ds(b * pitch, tb), :]

    @pl.when(jnp.logical_and(gs == last_gs, jnp.logical_not(has_overflow)))
    def _finish():
        finish(False)

    @pl.when(jnp.logical_and(gs == last_gs, has_overflow))
    def _finish_with_overflow():
        finish(True)


def _run_moe(x1, mod, mod_blk0, n_half, nb, tb, out_seqs, out_len, wts):
    n_tiles = x1.shape[0] // MOE_ROWS
    assert n_half * nb * tb == MOE_ROWS
    seq_per_tile = n_half * nb
    n_tt = out_len // tb
    pitch = _slab_pitch(tb)
    st = lambda q: (q // n_tt, q % n_tt)
    const2 = lambda q, g: (0, 0)
    grp3 = lambda q, g: (g, 0, 0)
    single = pl.Buffered(1)
    in_specs = [
        pl.BlockSpec((MOE_ROWS, D), lambda q, g: (q, 0)),
        pl.BlockSpec((seq_per_tile, 6 * D), lambda q, g: (mod_blk0 + st(q)[0], 0)),
        pl.BlockSpec((ROUTE_ROWS, D), const2),
        pl.BlockSpec((ROUTE_ROWS, 1), const2),
        pl.BlockSpec((LANES, LANES), const2),
        pl.BlockSpec((GPS * EPG, D, D_EXPERT), grp3),
        pl.BlockSpec((GPS * EPG, D, D_EXPERT), grp3),
        pl.BlockSpec((GPS * EPG, D_EXPERT, D), grp3),
        pl.BlockSpec((1, D), const2),
        pl.BlockSpec((1, D), const2),
    ]
    scratch = [
        pltpu.VMEM((MOE_ROWS, D), bf16),
        pltpu.VMEM((MOE_ROWS, LANES), f32),
        pltpu.VMEM((MOE_ROWS, LANES), bf16),
        pltpu.VMEM((SUBLANES, MOE_ROWS), i32),
        pltpu.SMEM((N_GROUPS + 1,), i32),
        pltpu.VMEM((N_GROUPS * CHUNK_PAD, D), bf16),
        pltpu.VMEM((MOE_ROWS, D), f32),
        pltpu.VMEM((N_SLABS, nb * pitch, LANES), f32),
    ]
    body = functools.partial(_moe_body, n_half, nb, tb, pitch)
    return pl.pallas_call(
        body,
        grid=(n_tiles, N_GROUPS // GPS),
        in_specs=in_specs,
        out_specs=pl.BlockSpec((seq_per_tile, tb, D), lambda q, g: (*st(q), 0)),
        out_shape=jax.ShapeDtypeStruct((out_seqs, out_len, D), f32),
        scratch_shapes=scratch,
        compiler_params=pltpu.CompilerParams(
            dimension_semantics=("arbitrary", "arbitrary"),
            vmem_limit_bytes=VMEM_LIMIT),
        name=f"moe_nb{nb}_tb{tb}",
    )(x1, mod, *wts)


def _pair_blocks(w):
    n, k, _ = w.shape
    wp = w.reshape(n // 2, 2, k, k)
    eye = jnp.eye(2, dtype=w.dtype)
    return jnp.einsum('phij,hg->phigj', wp, eye).reshape(n // 2, 2 * k, 2 * k)


def kernel(x_prompt, x_sample, c_prompt, c_sample, state_pool, state_conv, state_lru, w_ada, b_ada, w_in, w_pool, pool_scale, w_conv, b_conv, w_a, b_a, w_x, b_x, lru_lambda, w_out, ln1_g, ln1_b, w_group, b_group, w_route, b_route, w_gate, w_up, w_down, ln2_g, ln2_b):
    l = 0
    bp, tp, _ = x_prompt.shape
    bs, ts, _ = x_sample.shape
    mod = _ada_mod(jnp.concatenate([c_sample, c_prompt], axis=0), w_ada[l], b_ada[l])

    w_rt = jnp.zeros((ROUTE_ROWS, D), f32).at[0:N_GROUPS].set(w_group[l].T)
    b_rt = jnp.zeros((ROUTE_ROWS,), f32).at[0:N_GROUPS].set(b_group[l])
    for k in range(N_GROUPS):
        r0 = SUBLANES * (k + 1)
        w_rt = w_rt.at[r0:r0 + EPG].set(w_route[l][:, k * EPG:(k + 1) * EPG].T)
        b_rt = b_rt.at[r0:r0 + EPG].set(b_route[l][k * EPG:(k + 1) * EPG])
    tri = jnp.triu(jnp.ones((LANES, LANES), bf16), 1)

    mix_wts = (
        w_in[l].astype(bf16),
        w_pool[l].astype(bf16),
        pool_scale[l].reshape(1, D_POOL),
        w_conv[l],
        b_conv[l].reshape(1, D_LRU),
        jnp.stack([_pair_blocks(w_a[l]), _pair_blocks(w_x[l])]).astype(bf16),
        jnp.concatenate([b_a[l], b_x[l]]).reshape(1, 2 * D_LRU),
        lru_lambda[l].reshape(1, D_LRU),
        w_out[l].astype(bf16),
        ln1_g[l].reshape(1, D),
        ln1_b[l].reshape(1, D),
    )
    moe_wts = (
        w_rt.astype(bf16),
        b_rt.reshape(ROUTE_ROWS, 1),
        tri,
        w_gate[l].astype(bf16),
        w_up[l].astype(bf16),
        w_down[l].astype(bf16),
        ln2_g[l].reshape(1, D),
        ln2_b[l].reshape(1, D),
    )

    tb_p = MOE_ROWS // bp
    zp = jnp.zeros((POOL_BUF, bp, D_POOL), f32)
    zc = jnp.zeros((CONV_W - 1, bp, D_LRU), f32)
    zh = jnp.zeros((bp, D_LRU), f32)
    x1p, pool_p, conv_p, lru_p = _run_mix(x_prompt, mod, bs // bp, zp, zc, zh, 0, bp, tb_p, mix_wts)
    yp = _run_moe(x1p, mod, bs // bp, 1, bp, tb_p, bp, tp, moe_wts)

    n_prev_s = min(PAST_LEN, POOL_BUF)
    nb_s = bs // 2
    x1s, pool_s, conv_s, lru_s = _run_mix(
        x_sample, mod, 0, state_pool[l].transpose(1, 0, 2), state_conv[l].transpose(1, 0, 2),
        state_lru[l], n_prev_s, nb_s, ts, mix_wts)
    ys = _run_moe(x1s, mod, 0, 2, nb_s, ts, bs, ts, moe_wts)

    tr = lambda a: a.transpose(1, 0, 2)[None]
    return (yp, ys, tr(pool_p), tr(conv_p), lru_p[None], tr(pool_s), tr(conv_s), lru_s[None])
```

```python
import functools
import math

import jax
import jax.numpy as jnp
from jax import lax
from jax.experimental import pallas as pl
from jax.experimental.pallas import tpu as pltpu

D = 1024
D_POOL = 512
D_LRU = 512
D_IN = D_POOL + 2 * D_LRU
POOL_WINDOWS = (2, 4, 8, 16)
POOL_GROUP = 128
POOL_BUF = 15
CONV_W = 4
LRU_C = 8.0
N_GROUPS = 4
EPG = 4
D_EXPERT = 256
DEPTH = 1
ALPHA = (2.0 * DEPTH) ** 0.25
LN_EPS = 1e-5
PAST_LEN = 16384

LANES = 128
SUBLANES = 8
N_SLABS = D // LANES
ROUTE_ROWS = 48
MOE_ROWS = 1024
CHUNK = 288
CHUNK_PAD = 320
ROW_LO = 4
ROW_LO2 = 8
ROW_GSEL = 12
ROW_RANK = 13
N_RB = 4
GPS = 2
FIN_ROWS = 256
MIX_BLOCK_ROWS = 1024
VMEM_LIMIT = 60 * 1024 * 1024

f32 = jnp.float32
bf16 = jnp.bfloat16
i32 = jnp.int32


def _dot(a, b):
    return jnp.dot(a, b, preferred_element_type=f32)


LOG2E = 1.0 / math.log(2.0)
SQRT_GUARD = 1e-30


def _sigmoid(x):
    return 1.0 / (1.0 + jnp.exp2(x * (-LOG2E)))


def _gelu_tanh(x):
    c = math.sqrt(2.0 / math.pi)
    hx = 0.5 * x
    return hx + hx * jnp.tanh(x * (c + (c * 0.044715) * (x * x)))


def _layer_norm(v, g, b):
    mu = jnp.mean(v, axis=-1, keepdims=True)
    c = v - mu
    var = jnp.mean(c * c, axis=-1, keepdims=True)
    return c * lax.rsqrt(var + LN_EPS) * g + b


def _slab_pitch(tb):
    return tb + SUBLANES if (tb // SUBLANES) % 2 == 0 else tb


def _ada_body(c_ref, w_ref, b_ref, o_ref):
    o_ref[...] = _dot(c_ref[...].astype(bf16), w_ref[...].astype(bf16)) + b_ref[...]


def _ada_mod(c_all, w_ada, b_ada):
    n = c_all.shape[0]
    bn = 1024
    return pl.pallas_call(
        _ada_body,
        grid=(6 * D // bn,),
        in_specs=[
            pl.BlockSpec((n, D), lambda j: (0, 0)),
            pl.BlockSpec((D, bn), lambda j: (0, j)),
            pl.BlockSpec((1, bn), lambda j: (0, j)),
        ],
        out_specs=pl.BlockSpec((n, bn), lambda j: (0, j)),
        out_shape=jax.ShapeDtypeStruct((n, 6 * D), f32),
        name="ada_mod",
    )(c_all, w_ada, b_ada.reshape(1, 6 * D))


def _mix_body(nb, tb, pitch, n_prev, n_tt,
              x_ref, mod_ref, pool0_ref, conv0_ref, h0_ref,
              w_in_ref, w_pool_ref, pscale_ref, w_conv_ref, b_conv_ref,
              w_ax_ref, b_ax_ref, lam_ref, w_out_ref, ln1g_ref, ln1b_ref,
              x1_ref, npool_ref, nconv_ref, nh_ref,
              slab, xt, ycat, zpool, zconv, a_s, b_s, gl_s, h_s):
    tm = nb * tb
    n_blk = max(1, tm // MIX_BLOCK_ROWS)
    tbb = tb // n_blk
    rb = tbb * nb
    tq = tbb // N_RB
    rq = tq * nb
    ti = pl.program_id(0) % n_tt

    def mod_part(k):
        return mod_ref[:, pl.ds(k * D, D)]

    @pl.when(ti == 0)
    def _init_state():
        zpool[pl.ds(0, POOL_BUF * nb), :] = pool0_ref[...].reshape(POOL_BUF * nb, D_POOL)
        zconv[pl.ds(0, (CONV_W - 1) * nb), :] = conv0_ref[...].reshape((CONV_W - 1) * nb, D_LRU)
        h_s[...] = h0_ref[...]

    for b in range(nb):
        for j in range(N_SLABS):
            slab[j, pl.ds(b * pitch, tb), :] = x_ref[b, :, pl.ds(j * LANES, LANES)]

    sh1 = mod_part(0)
    sc1 = mod_part(1)
    g1 = mod_part(2)
    nl = -lam_ref[...]
    softplus = jnp.maximum(nl, 0.0) + jnp.log(1.0 + jnp.exp(-jnp.abs(nl)))
    log2a_unit = (-LRU_C * LOG2E) * softplus
    row = lax.broadcasted_iota(i32, (rb, LANES), 0)
    t_loc = lax.shift_right_logical(row, int(math.log2(nb)))
    h = h_s[...]

    for blk in range(n_blk):
        t0 = blk * tbb
        r0 = t0 * nb
        brows = pl.ds(r0, rb)
        for t in range(t0, t0 + tbb):
            for j in range(N_SLABS):
                xt[t, :, pl.ds(j * LANES, LANES)] = slab[j, pl.ds(t, nb, stride=pitch), :]

        for r in range(N_RB):
            u = xt[pl.ds(t0 + r * tq, tq)] * (1.0 + sc1)[None] + sh1[None]
            ycat[pl.ds(r0 + r * rq, rq), :] = u.reshape(rq, D).astype(bf16)
        ub = ycat[brows, :]
        zpool[pl.ds(POOL_BUF * nb + r0, rb), :] = _dot(ub, w_in_ref[:, pl.ds(0, D_POOL)])
        zconv[pl.ds((CONV_W - 1) * nb + r0, rb), :] = _dot(ub, w_in_ref[:, pl.ds(D_POOL, D_LRU)])
        half = D_LRU // 2
        for hpart in range(2):
            cols = pl.ds(D_POOL + D_LRU + hpart * half, half)
            gl_s[brows, pl.ds(hpart * half, half)] = _gelu_tanh(_dot(ub, w_in_ref[:, cols]))

        t_glob = (ti * tb + t0 + t_loc + (1 + n_prev)).astype(f32)
        for c, w in enumerate(POOL_WINDOWS):
            lanes = pl.ds(c * LANES, LANES)
            s = zpool[pl.ds((POOL_BUF + 1 - w) * nb + r0, rb + (w - 1) * nb), lanes]
            step = 1
            while step < w:
                s = s[step * nb:] + s[:-step * nb]
                step *= 2
            cnt = jnp.minimum(f32(w), t_glob)
            dlt = s / cnt - zpool[pl.ds(POOL_BUF * nb + r0, rb), lanes]
            yp = _dot(dlt.astype(bf16), w_pool_ref[c]) * pscale_ref[:, lanes]
            ycat[brows, lanes] = yp.astype(bf16)

        for j in range(D_LRU // LANES):
            lanes = pl.ds(j * LANES, LANES)
            xc = b_conv_ref[:, lanes] + zconv[pl.ds(r0, rb), lanes] * w_conv_ref[0:1, lanes]
            for k in range(1, CONV_W):
                xc = xc + zconv[pl.ds(k * nb + r0, rb), lanes] * w_conv_ref[k:k + 1, lanes]
            xcb = xc.astype(bf16)
            r = _sigmoid(_dot(xcb, w_ax_ref[0, j]) + b_ax_ref[:, lanes])
            ig = _sigmoid(_dot(xcb, w_ax_ref[1, j]) + b_ax_ref[:, pl.ds(D_LRU + j * LANES, LANES)])
            a = jnp.exp2(r * log2a_unit[:, j * LANES:(j + 1) * LANES])
            a_s[brows, lanes] = a
            om = jnp.maximum(1.0 - a * a, 0.0)
            mult = om * lax.rsqrt(jnp.maximum(om, SQRT_GUARD))
            b_s[brows, lanes] = mult * ig * xc

        for t in range(t0, t0 + tbb):
            rows = pl.ds(t * nb, nb)
            h = a_s[rows, :] * h + b_s[rows, :]
            b_s[rows, :] = h
        ycat[brows, pl.ds(D_POOL, D_LRU)] = (b_s[brows, :] * gl_s[brows, :]).astype(bf16)

        yc = ycat[brows, :]
        cb = D // N_RB
        for j in range(N_RB):
            cols = pl.ds(j * cb, cb)
            mix = _dot(yc, w_out_ref[:, cols]).reshape(tbb, nb, cb)
            xt[pl.ds(t0, tbb), :, cols] = (ALPHA * xt[pl.ds(t0, tbb), :, cols]
                                           + (1.0 + g1[:, j * cb:(j + 1) * cb])[None] * mix)
        for r in range(N_RB):
            xn = _layer_norm(xt[pl.ds(t0 + r * tq, tq)], ln1g_ref[...][None], ln1b_ref[...][None])
            x1_ref[pl.ds(r0 + r * rq, rq), :] = xn.reshape(rq, D)

    h_s[...] = h
    zpool[pl.ds(0, POOL_BUF * nb), :] = zpool[pl.ds(tm, POOL_BUF * nb), :]
    zconv[pl.ds(0, (CONV_W - 1) * nb), :] = zconv[pl.ds(tm, (CONV_W - 1) * nb), :]

    @pl.when(ti == n_tt - 1)
    def _emit_state():
        npool_ref[...] = zpool[pl.ds(0, POOL_BUF * nb), :].reshape(POOL_BUF, nb, D_POOL)
        nconv_ref[...] = zconv[pl.ds(0, (CONV_W - 1) * nb), :].reshape(CONV_W - 1, nb, D_LRU)
        nh_ref[...] = h_s[...]


def _run_mix(x, mod, mod_blk0, pool0, conv0, h0, n_prev, nb, tb, wts):
    bsz, t_len, _ = x.shape
    n_sb = bsz // nb
    n_tt = t_len // tb
    tm = nb * tb
    pitch = _slab_pitch(tb)
    st = lambda q: (q // n_tt, q % n_tt)
    const2 = lambda q: (0, 0)
    const3 = lambda q: (0, 0, 0)
    const4 = lambda q: (0, 0, 0, 0)
    single = pl.Buffered(1)
    in_specs = [
        pl.BlockSpec((nb, tb, D), lambda q: (*st(q), 0)),
        pl.BlockSpec((nb, 6 * D), lambda q: (mod_blk0 + st(q)[0], 0)),
        pl.BlockSpec((POOL_BUF, nb, D_POOL), lambda q: (0, st(q)[0], 0)),
        pl.BlockSpec((CONV_W - 1, nb, D_LRU), lambda q: (0, st(q)[0], 0)),
        pl.BlockSpec((nb, D_LRU), lambda q: (st(q)[0], 0)),
        pl.BlockSpec((D, D_IN), const2, pipeline_mode=single),
        pl.BlockSpec((4, POOL_GROUP, POOL_GROUP), const3),
        pl.BlockSpec((1, D_POOL), const2),
        pl.BlockSpec((CONV_W, D_LRU), const2),
        pl.BlockSpec((1, D_LRU), const2),
        pl.BlockSpec((2, D_LRU // LANES, LANES, LANES), const4),
        pl.BlockSpec((1, 2 * D_LRU), const2),
        pl.BlockSpec((1, D_LRU), const2),
        pl.BlockSpec((D, D), const2, pipeline_mode=single),
        pl.BlockSpec((1, D), const2),
        pl.BlockSpec((1, D), const2),
    ]
    out_specs = [
        pl.BlockSpec((tm, D), lambda q: (q, 0)),
        pl.BlockSpec((POOL_BUF, nb, D_POOL), lambda q: (0, st(q)[0], 0)),
        pl.BlockSpec((CONV_W - 1, nb, D_LRU), lambda q: (0, st(q)[0], 0)),
        pl.BlockSpec((nb, D_LRU), lambda q: (st(q)[0], 0)),
    ]
    out_shape = [
        jax.ShapeDtypeStruct((bsz * t_len, D), f32),
        jax.ShapeDtypeStruct((POOL_BUF, bsz, D_POOL), f32),
        jax.ShapeDtypeStruct((CONV_W - 1, bsz, D_LRU), f32),
        jax.ShapeDtypeStruct((bsz, D_LRU), f32),
    ]
    scratch = [
        pltpu.VMEM((N_SLABS, nb * pitch, LANES), f32),
        pltpu.VMEM((tb, nb, D), f32),
        pltpu.VMEM((tm, D), bf16),
        pltpu.VMEM(((tb + POOL_BUF) * nb, D_POOL), f32),
        pltpu.VMEM(((tb + CONV_W - 1) * nb, D_LRU), f32),
        pltpu.VMEM((tm, D_LRU), f32),
        pltpu.VMEM((tm, D_LRU), f32),
        pltpu.VMEM((tm, D_LRU), f32),
        pltpu.VMEM((nb, D_LRU), f32),
    ]
    body = functools.partial(_mix_body, nb, tb, pitch, n_prev, n_tt)
    return pl.pallas_call(
        body,
        grid=(n_sb * n_tt,),
        in_specs=in_specs,
        out_specs=out_specs,
        out_shape=out_shape,
        scratch_shapes=scratch,
        compiler_params=pltpu.CompilerParams(
            dimension_semantics=("arbitrary",),
            vmem_limit_bytes=VMEM_LIMIT),
        name=f"mix_nb{nb}_tb{tb}",
    )(x, mod, pool0, conv0, h0, *wts)


def _moe_body(n_half, nb, tb, pitch,
              x1_ref, mod_ref, w_rt_ref, b_rt_ref, tri_ref, wg_ref, wu_ref, wd_ref, ln2g_ref, ln2b_ref,
              y_ref,
              u2, tok_t, tok3, rows_s, cnt_s, ys_all, yacc, slab):
    tm = MOE_ROWS
    hm = nb * tb
    tq = tb // N_RB
    rq = tq * nb
    gs = pl.program_id(1)
    last_gs = N_GROUPS // GPS - 1

    def mod_part(h, k):
        return mod_ref[pl.ds(h * nb, nb), pl.ds(k * D, D)]

    @pl.when(gs == 0)
    def _route():
        for h in range(n_half):
            sh2 = mod_part(h, 3)
            sc2 = mod_part(h, 4)
            for r in range(N_RB):
                rows = pl.ds(h * hm + r * rq, rq)
                v = x1_ref[rows, :].reshape(tq, nb, D) * (1.0 + sc2)[None] + sh2[None]
                u2[rows, :] = v.reshape(rq, D).astype(bf16)
        lt = lax.dot_general(w_rt_ref[...], u2[...], (((1,), (1,)), ((), ())),
                             preferred_element_type=f32) + b_rt_ref[...]
        gl = [lt[k:k + 1, :] for k in range(N_GROUPS)]
        best = gl[0]
        gsel = jnp.zeros_like(best, dtype=i32)
        for k in range(1, N_GROUPS):
            better = gl[k] > best
            best = jnp.where(better, gl[k], best)
            gsel = jnp.where(better, k, gsel)
        denom = jnp.exp(gl[0] - best)
        for k in range(1, N_GROUPS):
            denom = denom + jnp.exp(gl[k] - best)
        p_sel = 1.0 / denom
        es = []
        for j in range(EPG):
            v_j = lt[SUBLANES + j:SUBLANES + j + 1, :]
            for k in range(1, N_GROUPS):
                r0 = SUBLANES * (k + 1) + j
                v_j = jnp.where(gsel == k, lt[r0:r0 + 1, :], v_j)
            es.append(v_j)
        v1 = es[0]
        i1 = jnp.zeros_like(gsel)
        for j in range(1, EPG):
            better = es[j] > v1
            v1 = jnp.where(better, es[j], v1)
            i1 = jnp.where(better, j, i1)
        v2 = jnp.full_like(v1, -jnp.inf)
        i2 = jnp.full_like(gsel, -1)
        for j in range(EPG):
            cand = jnp.logical_and(i1 != j, jnp.logical_or(i2 < 0, es[j] > v2))
            v2 = jnp.where(cand, es[j], v2)
            i2 = jnp.where(cand, j, i2)
        e21 = jnp.exp(v2 - v1)
        w1 = p_sel / (1.0 + e21)
        w2 = p_sel * e21 / (1.0 + e21)
        rid8 = lax.broadcasted_iota(i32, (SUBLANES, tm), 0)
        onehot = jnp.where(rid8 == gsel, 1.0, 0.0)
        blocks = [onehot[:, j * LANES:(j + 1) * LANES] for j in range(tm // LANES)]
        inner = _dot(jnp.concatenate(blocks, axis=0).astype(bf16), tri_ref[...])
        before = jnp.zeros((SUBLANES, 1), f32)
        prefs = []
        for j, blk in enumerate(blocks):
            prefs.append(inner[j * SUBLANES:(j + 1) * SUBLANES, :] + before)
            before = before + jnp.sum(blk, axis=1, keepdims=True)
        pref = jnp.concatenate(prefs, axis=1)
        rank = jnp.sum(onehot * pref, axis=0, keepdims=True)
        n_max = 0
        for k in range(N_GROUPS):
            n_k = jnp.sum(jnp.where(gsel == k, 1.0, 0.0)).astype(i32)
            cnt_s[k] = n_k
            n_max = jnp.maximum(n_max, n_k)
        cnt_s[N_GROUPS] = (n_max > CHUNK).astype(i32)
        rows_s[0:1, :] = gsel
        rows_s[1:2, :] = rank.astype(i32)
        rid = lax.broadcasted_iota(i32, (LANES, tm), 0)
        rec = jnp.zeros((LANES, tm), f32)
        for j in range(EPG):
            wj = jnp.where(i1 == j, w1, jnp.where(i2 == j, w2, 0.0))
            hi = wj.astype(bf16).astype(f32)
            mid = (wj - hi).astype(bf16).astype(f32)
            lo = (wj - hi - mid).astype(bf16).astype(f32)
            rec = jnp.where(rid == j, hi, rec)
            rec = jnp.where(rid == ROW_LO + j, mid, rec)
            rec = jnp.where(rid == ROW_LO2 + j, lo, rec)
        rec = jnp.where(rid == ROW_GSEL, gsel.astype(f32), rec)
        rec = jnp.where(rid == ROW_RANK, rank, rec)
        rt = rec.T
        tok_t[...] = rt
        tok3[...] = rt.astype(bf16)

    def sorted_experts(gi, base):
        g = gs * GPS + gi
        gsel_row = rows_s[0:1, :]
        rank_row = rows_s[1:2, :]
        jj = lax.broadcasted_iota(i32, (CHUNK, tm), 0) + base
        pm = jnp.where(jnp.logical_and(gsel_row == g, rank_row == jj), 1.0, 0.0).astype(bf16)
        xs = _dot(pm, u2[...]).astype(bf16)
        cs3 = _dot(pm, tok3[...])
        cs = (cs3 + pltpu.roll(cs3, LANES - ROW_LO, axis=1)
              + pltpu.roll(cs3, LANES - ROW_LO2, axis=1))
        acts = []
        for e in range(EPG):
            hg = _dot(xs, wg_ref[gi * EPG + e])
            hu = _dot(xs, wu_ref[gi * EPG + e])
            acts.append((hg * _sigmoid(hg) * hu * cs[:, e:e + 1]).astype(bf16))
        hb = jnp.concatenate(acts, axis=1)
        wd_g = wd_ref[pl.ds(gi * EPG, EPG)].reshape(EPG * D_EXPERT, D)
        return _dot(hb, wd_g).astype(bf16)

    for gi in range(GPS):
        first_row = pl.multiple_of((gs * GPS + gi) * CHUNK_PAD, CHUNK_PAD)
        ys_all[pl.ds(first_row, CHUNK), :] = sorted_experts(gi, 0)

    @pl.when(gs == 0)
    def _zero_pad_rows():
        for k in range(N_GROUPS):
            ys_all[pl.ds(k * CHUNK_PAD + CHUNK, CHUNK_PAD - CHUNK), :] = jnp.zeros((CHUNK_PAD - CHUNK, D), bf16)

    has_overflow = cnt_s[N_GROUPS] > 0

    @pl.when(jnp.logical_and(gs == 0, has_overflow))
    def _zero_acc():
        yacc[...] = jnp.zeros((tm, D), f32)

    for gi in range(GPS):
        g = gs * GPS + gi

        def overflow(c, carry, gi=gi, g=g):
            base = c * CHUNK
            ys = sorted_experts(gi, base)
            rec = tok_t[...]
            gsel_col = rec[:, ROW_GSEL:ROW_GSEL + 1]
            rank_col = rec[:, ROW_RANK:ROW_RANK + 1]
            jl = (lax.broadcasted_iota(i32, (tm, CHUNK), 1) + base).astype(f32)
            pt = jnp.where(jnp.logical_and(gsel_col == g.astype(f32), rank_col == jl), 1.0, 0.0).astype(bf16)
            yacc[...] += _dot(pt, ys)
            return carry

        lax.fori_loop(1, pl.cdiv(cnt_s[g], CHUNK), overflow, 0)

    def finish(with_acc):
        tqf = FIN_ROWS // nb
        jl = lax.broadcasted_iota(i32, (FIN_ROWS, N_GROUPS * CHUNK_PAD), 1).astype(f32)
        for h in range(n_half):
            g2 = mod_part(h, 5)
            for r in range(hm // FIN_ROWS):
                rows = pl.ds(h * hm + r * FIN_ROWS, FIN_ROWS)
                rec = tok_t[rows, :]
                gsel_col = rec[:, ROW_GSEL:ROW_GSEL + 1]
                rank_col = rec[:, ROW_RANK:ROW_RANK + 1]
                key = jnp.where(rank_col < f32(CHUNK), gsel_col * f32(CHUNK_PAD) + rank_col, -1.0)
                pt = jnp.where(key == jl, 1.0, 0.0).astype(bf16)
                y = _dot(pt, ys_all[...])
                if with_acc:
                    y = y + yacc[rows, :]
                v = ALPHA * x1_ref[rows, :] + ((1.0 + g2)[None] * y.reshape(tqf, nb, D)).reshape(FIN_ROWS, D)
                yn = _layer_norm(v, ln2g_ref[...], ln2b_ref[...])
                for tl in range(tqf):
                    for j in range(N_SLABS):
                        slab[j, pl.ds(r * tqf + tl, nb, stride=pitch), :] = (
                            yn[tl * nb:(tl + 1) * nb, j * LANES:(j + 1) * LANES])
            for b in range(nb):
                for j in range(N_SLABS):
                    y_ref[h * nb + b, :, pl.ds(j * LANES, LANES)] = slab[j, pl.ds(b * pitch, tb), :]

    @pl.when(jnp.logical_and(gs == last_gs, jnp.logical_not(has_overflow)))
    def _finish():
        finish(False)

    @pl.when(jnp.logical_and(gs == last_gs, has_overflow))
    def _finish_with_overflow():
        finish(True)


def _run_moe(x1, mod, mod_blk0, n_half, nb, tb, out_seqs, out_len, wts):
    n_tiles = x1.shape[0] // MOE_ROWS
    assert n_half * nb * tb == MOE_ROWS
    seq_per_tile = n_half * nb
    n_tt = out_len // tb
    pitch = _slab_pitch(tb)
    st = lambda q: (q // n_tt, q % n_tt)
    const2 = lambda q, g: (0, 0)
    grp3 = lambda q, g: (g, 0, 0)
    single = pl.Buffered(1)
    in_specs = [
        pl.BlockSpec((MOE_ROWS, D), lambda q, g: (q, 0)),
        pl.BlockSpec((seq_per_tile, 6 * D), lambda q, g: (mod_blk0 + st(q)[0], 0)),
        pl.BlockSpec((ROUTE_ROWS, D), const2),
        pl.BlockSpec((ROUTE_ROWS, 1), const2),
        pl.BlockSpec((LANES, LANES), const2),
        pl.BlockSpec((GPS * EPG, D, D_EXPERT), grp3),
        pl.BlockSpec((GPS * EPG, D, D_EXPERT), grp3),
        pl.BlockSpec((GPS * EPG, D_EXPERT, D), grp3),
        pl.BlockSpec((1, D), const2),
        pl.BlockSpec((1, D), const2),
    ]
    scratch = [
        pltpu.VMEM((MOE_ROWS, D), bf16),
        pltpu.VMEM((MOE_ROWS, LANES), f32),
        pltpu.VMEM((MOE_ROWS, LANES), bf16),
        pltpu.VMEM((SUBLANES, MOE_ROWS), i32),
        pltpu.SMEM((N_GROUPS + 1,), i32),
        pltpu.VMEM((N_GROUPS * CHUNK_PAD, D), bf16),
        pltpu.VMEM((MOE_ROWS, D), f32),
        pltpu.VMEM((N_SLABS, nb * pitch, LANES), f32),
    ]
    body = functools.partial(_moe_body, n_half, nb, tb, pitch)
    return pl.pallas_call(
        body,
        grid=(n_tiles, N_GROUPS // GPS),
        in_specs=in_specs,
        out_specs=pl.BlockSpec((seq_per_tile, tb, D), lambda q, g: (*st(q), 0)),
        out_shape=jax.ShapeDtypeStruct((out_seqs, out_len, D), f32),
        scratch_shapes=scratch,
        compiler_params=pltpu.CompilerParams(
            dimension_semantics=("arbitrary", "arbitrary"),
            vmem_limit_bytes=VMEM_LIMIT),
        name=f"moe_nb{nb}_tb{tb}",
    )(x1, mod, *wts)


def _pair_blocks(w):
    n, k, _ = w.shape
    wp = w.reshape(n // 2, 2, k, k)
    eye = jnp.eye(2, dtype=w.dtype)
    return jnp.einsum('phij,hg->phigj', wp, eye).reshape(n // 2, 2 * k, 2 * k)


def kernel(x_prompt, x_sample, c_prompt, c_sample, state_pool, state_conv, state_lru, w_ada, b_ada, w_in, w_pool, pool_scale, w_conv, b_conv, w_a, b_a, w_x, b_x, lru_lambda, w_out, ln1_g, ln1_b, w_group, b_group, w_route, b_route, w_gate, w_up, w_down, ln2_g, ln2_b):
    l = 0
    bp, tp, _ = x_prompt.shape
    bs, ts, _ = x_sample.shape
    mod = _ada_mod(jnp.concatenate([c_sample, c_prompt], axis=0), w_ada[l], b_ada[l])

    w_rt = jnp.zeros((ROUTE_ROWS, D), f32).at[0:N_GROUPS].set(w_group[l].T)
    b_rt = jnp.zeros((ROUTE_ROWS,), f32).at[0:N_GROUPS].set(b_group[l])
    for k in range(N_GROUPS):
        r0 = SUBLANES * (k + 1)
        w_rt = w_rt.at[r0:r0 + EPG].set(w_route[l][:, k * EPG:(k + 1) * EPG].T)
        b_rt = b_rt.at[r0:r0 + EPG].set(b_route[l][k * EPG:(k + 1) * EPG])
    tri = jnp.triu(jnp.ones((LANES, LANES), bf16), 1)

    mix_wts = (
        w_in[l].astype(bf16),
        w_pool[l].astype(bf16),
        pool_scale[l].reshape(1, D_POOL),
        w_conv[l],
        b_conv[l].reshape(1, D_LRU),
        jnp.stack([_pair_blocks(w_a[l]), _pair_blocks(w_x[l])]).astype(bf16),
        jnp.concatenate([b_a[l], b_x[l]]).reshape(1, 2 * D_LRU),
        lru_lambda[l].reshape(1, D_LRU),
        w_out[l].astype(bf16),
        ln1_g[l].reshape(1, D),
        ln1_b[l].reshape(1, D),
    )
    moe_wts = (
        w_rt.astype(bf16),
        b_rt.reshape(ROUTE_ROWS, 1),
        tri,
        w_gate[l].astype(bf16),
        w_up[l].astype(bf16),
        w_down[l].astype(bf16),
        ln2_g[l].reshape(1, D),
        ln2_b[l].reshape(1, D),
    )

    tb_p = MOE_ROWS // bp
    zp = jnp.zeros((POOL_BUF, bp, D_POOL), f32)
    zc = jnp.zeros((CONV_W - 1, bp, D_LRU), f32)
    zh = jnp.zeros((bp, D_LRU), f32)
    x1p, pool_p, conv_p, lru_p = _run_mix(x_prompt, mod, bs // bp, zp, zc, zh, 0, bp, tb_p, mix_wts)
    yp = _run_moe(x1p, mod, bs // bp, 1, bp, tb_p, bp, tp, moe_wts)

    n_prev_s = min(PAST_LEN, POOL_BUF)
    nb_s = bs // 2
    x1s, pool_s, conv_s, lru_s = _run_mix(
        x_sample, mod, 0, state_pool[l].transpose(1, 0, 2), state_conv[l].transpose(1, 0, 2),
        state_lru[l], n_prev_s, nb_s, ts, mix_wts)
    ys = _run_moe(x1s, mod, 0, 2, nb_s, ts, bs, ts, moe_wts)

    tr = lambda a: a.transpose(1, 0, 2)[None]
    return (yp, ys, tr(pool_p), tr(conv_p), lru_p[None], tr(pool_s), tr(conv_s), lru_s[None])
```

```python
import functools
import math

import jax
import jax.numpy as jnp
from jax import lax
from jax.experimental import pallas as pl
from jax.experimental.pallas import tpu as pltpu

D = 1024
D_POOL = 512
D_LRU = 512
D_IN = D_POOL + 2 * D_LRU
POOL_WINDOWS = (2, 4, 8, 16)
POOL_GROUP = 128
POOL_BUF = 15
CONV_W = 4
LRU_C = 8.0
N_GROUPS = 4
EPG = 4
D_EXPERT = 256
DEPTH = 1
ALPHA = (2.0 * DEPTH) ** 0.25
LN_EPS = 1e-5
PAST_LEN = 16384

LANES = 128
SUBLANES = 8
MXU_DIM = 256
N_SLABS = D // LANES
ROUTE_ROWS = 48
MOE_ROWS = 1024
CHUNK = 288
CHUNK_PAD = 320
assert CHUNK <= CHUNK_PAD and (N_GROUPS * CHUNK_PAD) % MXU_DIM == 0
ROW_LO = 4
ROW_LO2 = 8
ROW_GSEL = 12
ROW_RANK = 13
REC_ROWS = 16
N_RB = 4
GPS = 2
FIN_ROWS = 256
ADA_COLS = 1024
V7X_VMEM_BYTES = 64 * 1024 * 1024
VMEM_RESERVE_BYTES = 4 * 1024 * 1024
VMEM_LIMIT = V7X_VMEM_BYTES - VMEM_RESERVE_BYTES
LOG2E = 1.0 / math.log(2.0)
SQRT_GUARD = 1e-30

f32 = jnp.float32
bf16 = jnp.bfloat16
i32 = jnp.int32


def _dot(a, b):
    return jnp.dot(a, b, preferred_element_type=f32)


def _sigmoid(x):
    return 1.0 / (1.0 + jnp.exp2(x * (-LOG2E)))


def _gelu_tanh(x):
    k1 = -2.0 * math.sqrt(2.0 / math.pi) * LOG2E
    return x / (1.0 + jnp.exp2(x * (k1 + (k1 * 0.044715) * (x * x))))


def _layer_norm(v, g, b):
    mu = jnp.mean(v, axis=-1, keepdims=True)
    c = v - mu
    var = jnp.mean(c * c, axis=-1, keepdims=True)
    return c * lax.rsqrt(var + LN_EPS) * g + b


def _slab_pitch(tb):
    return tb + SUBLANES if (tb // SUBLANES) % 2 == 0 else tb


def _ada_body(c_ref, w_ref, b_ref, o_ref):
    o_ref[...] = _dot(c_ref[...].astype(bf16), w_ref[...].astype(bf16)) + b_ref[...]


def _ada_mod(c_all, w_ada, b_ada):
    n = c_all.shape[0]
    bn = ADA_COLS
    return pl.pallas_call(
        _ada_body,
        grid=(6 * D // bn,),
        in_specs=[
            pl.BlockSpec((n, D), lambda j: (0, 0)),
            pl.BlockSpec((D, bn), lambda j: (0, j)),
            pl.BlockSpec((1, bn), lambda j: (0, j)),
        ],
        out_specs=pl.BlockSpec((n, bn), lambda j: (0, j)),
        out_shape=jax.ShapeDtypeStruct((n, 6 * D), f32),
        name="ada_mod",
    )(c_all, w_ada, b_ada.reshape(1, 6 * D))


def _mix_body(nb, tb, pitch, n_prev, n_tt, n_cast,
              x_ref, mod_ref, pool0_ref, conv0_ref, h0_ref,
              w_in_ref, w_pool_ref, pscale_ref, w_conv_ref, b_conv_ref,
              w_ax_ref, b_ax_ref, lam_ref, w_out_ref, ln1g_ref, ln1b_ref,
              *rest):
    cast_in = rest[:n_cast]
    x1_ref, npool_ref, nconv_ref, nh_ref = rest[n_cast:n_cast + 4]
    cast_out = rest[n_cast + 4:2 * n_cast + 4]
    slab, xt, ycat, zpool, zconv, a_s, b_s, gl_s, h_s = rest[2 * n_cast + 4:]
    for src, dst in zip(cast_in, cast_out):
        dst[...] = src[...].astype(bf16)
    tm = nb * tb
    tq = tb // N_RB
    rq = tq * nb
    ti = pl.program_id(0) % n_tt

    def mod_part(k):
        return mod_ref[:, pl.ds(k * D, D)]

    @pl.when(ti == 0)
    def _init_state():
        zpool[pl.ds(0, POOL_BUF * nb), :] = pool0_ref[...].reshape(POOL_BUF * nb, D_POOL)
        zconv[pl.ds(0, (CONV_W - 1) * nb), :] = conv0_ref[...].reshape((CONV_W - 1) * nb, D_LRU)
        h_s[...] = h0_ref[...]

    for b in range(nb):
        for j in range(N_SLABS):
            slab[j, pl.ds(b * pitch, tb), :] = x_ref[b, :, pl.ds(j * LANES, LANES)]

    sh1 = mod_part(0)
    sc1 = mod_part(1)
    g1 = mod_part(2)
    nl = -lam_ref[...]
    softplus = jnp.maximum(nl, 0.0) + jnp.log(1.0 + jnp.exp(-jnp.abs(nl)))
    log2a_unit = (-LRU_C * LOG2E) * softplus
    row = lax.broadcasted_iota(i32, (tm, LANES), 0)
    t_loc = lax.shift_right_logical(row, int(math.log2(nb)))
    h = h_s[...]

    for t in range(tb):
        for j in range(N_SLABS):
            xt[t, :, pl.ds(j * LANES, LANES)] = slab[j, pl.ds(t, nb, stride=pitch), :]

    for r in range(N_RB):
        u = xt[pl.ds(r * tq, tq)] * (1.0 + sc1)[None] + sh1[None]
        ycat[pl.ds(r * rq, rq), :] = u.reshape(rq, D).astype(bf16)
    ub = ycat[...]
    zpool[pl.ds(POOL_BUF * nb, tm), :] = _dot(ub, w_in_ref[:, pl.ds(0, D_POOL)])
    zconv[pl.ds((CONV_W - 1) * nb, tm), :] = _dot(ub, w_in_ref[:, pl.ds(D_POOL, D_LRU)])
    half = D_LRU // 2
    for hpart in range(2):
        cols = pl.ds(D_POOL + D_LRU + hpart * half, half)
        gl_s[:, pl.ds(hpart * half, half)] = _gelu_tanh(_dot(ub, w_in_ref[:, cols]))

    t_glob = (ti * tb + t_loc + (1 + n_prev)).astype(f32)
    for c, w in enumerate(POOL_WINDOWS):
        lanes = pl.ds(c * LANES, LANES)
        s = zpool[pl.ds((POOL_BUF + 1 - w) * nb, tm + (w - 1) * nb), lanes]
        step = 1
        while step < w:
            s = s[step * nb:] + s[:-step * nb]
            step *= 2
        cnt = jnp.minimum(f32(w), t_glob)
        dlt = s / cnt - zpool[pl.ds(POOL_BUF * nb, tm), lanes]
        yp = _dot(dlt.astype(bf16), w_pool_ref[c]) * pscale_ref[:, lanes]
        ycat[:, lanes] = yp.astype(bf16)

    for j in range(D_LRU // LANES):
        lanes = pl.ds(j * LANES, LANES)
        xc = b_conv_ref[:, lanes] + zconv[pl.ds(0, tm), lanes] * w_conv_ref[0:1, lanes]
        for k in range(1, CONV_W):
            xc = xc + zconv[pl.ds(k * nb, tm), lanes] * w_conv_ref[k:k + 1, lanes]
        xcb = xc.astype(bf16)
        r = _sigmoid(_dot(xcb, w_ax_ref[0, j]) + b_ax_ref[:, lanes])
        ig = _sigmoid(_dot(xcb, w_ax_ref[1, j]) + b_ax_ref[:, pl.ds(D_LRU + j * LANES, LANES)])
        a = jnp.exp2(r * log2a_unit[:, j * LANES:(j + 1) * LANES])
        a_s[:, lanes] = a
        om = jnp.maximum(1.0 - a * a, 0.0)
        mult = om * lax.rsqrt(jnp.maximum(om, SQRT_GUARD))
        b_s[:, lanes] = mult * ig * xc

    for t in range(tb):
        rows = pl.ds(t * nb, nb)
        h = a_s[rows, :] * h + b_s[rows, :]
        b_s[rows, :] = h
    ycat[:, pl.ds(D_POOL, D_LRU)] = (b_s[...] * gl_s[...]).astype(bf16)

    yc = ycat[...]
    cb = D // N_RB
    for j in range(N_RB):
        cols = pl.ds(j * cb, cb)
        mix = _dot(yc, w_out_ref[:, cols]).reshape(tb, nb, cb)
        xt[:, :, cols] = ALPHA * xt[:, :, cols] + (1.0 + g1[:, j * cb:(j + 1) * cb])[None] * mix
    for r in range(N_RB):
        xn = _layer_norm(xt[pl.ds(r * tq, tq)], ln1g_ref[...][None], ln1b_ref[...][None])
        x1_ref[pl.ds(r * rq, rq), :] = xn.reshape(rq, D)

    h_s[...] = h
    zpool[pl.ds(0, POOL_BUF * nb), :] = zpool[pl.ds(tm, POOL_BUF * nb), :]
    zconv[pl.ds(0, (CONV_W - 1) * nb), :] = zconv[pl.ds(tm, (CONV_W - 1) * nb), :]

    @pl.when(ti == n_tt - 1)
    def _emit_state():
        npool_ref[...] = zpool[pl.ds(0, POOL_BUF * nb), :].reshape(POOL_BUF, nb, D_POOL)
        nconv_ref[...] = zconv[pl.ds(0, (CONV_W - 1) * nb), :].reshape(CONV_W - 1, nb, D_LRU)
        nh_ref[...] = h_s[...]


def _run_mix(x, mod, mod_blk0, pool0, conv0, h0, n_prev, nb, tb, wts, to_cast=()):
    bsz, t_len, _ = x.shape
    n_sb = bsz // nb
    n_tt = t_len // tb
    tm = nb * tb
    pitch = _slab_pitch(tb)
    st = lambda q: (q // n_tt, q % n_tt)
    const2 = lambda q: (0, 0)
    const3 = lambda q: (0, 0, 0)
    const4 = lambda q: (0, 0, 0, 0)
    single = pl.Buffered(1)
    in_specs = [
        pl.BlockSpec((nb, tb, D), lambda q: (*st(q), 0)),
        pl.BlockSpec((nb, 6 * D), lambda q: (mod_blk0 + st(q)[0], 0)),
        pl.BlockSpec((POOL_BUF, nb, D_POOL), lambda q: (0, st(q)[0], 0)),
        pl.BlockSpec((CONV_W - 1, nb, D_LRU), lambda q: (0, st(q)[0], 0)),
        pl.BlockSpec((nb, D_LRU), lambda q: (st(q)[0], 0)),
        pl.BlockSpec((D, D_IN), const2, pipeline_mode=single),
        pl.BlockSpec((4, POOL_GROUP, POOL_GROUP), const3),
        pl.BlockSpec((1, D_POOL), const2),
        pl.BlockSpec((CONV_W, D_LRU), const2),
        pl.BlockSpec((1, D_LRU), const2),
        pl.BlockSpec((2, D_LRU // LANES, LANES, LANES), const4),
        pl.BlockSpec((1, 2 * D_LRU), const2),
        pl.BlockSpec((1, D_LRU), const2),
        pl.BlockSpec((D, D), const2, pipeline_mode=single),
        pl.BlockSpec((1, D), const2),
        pl.BlockSpec((1, D), const2),
    ]
    out_specs = [
        pl.BlockSpec((tm, D), lambda q: (q, 0)),
        pl.BlockSpec((POOL_BUF, nb, D_POOL), lambda q: (0, st(q)[0], 0)),
        pl.BlockSpec((CONV_W - 1, nb, D_LRU), lambda q: (0, st(q)[0], 0)),
        pl.BlockSpec((nb, D_LRU), lambda q: (st(q)[0], 0)),
    ]
    out_shape = [
        jax.ShapeDtypeStruct((bsz * t_len, D), f32),
        jax.ShapeDtypeStruct((POOL_BUF, bsz, D_POOL), f32),
        jax.ShapeDtypeStruct((CONV_W - 1, bsz, D_LRU), f32),
        jax.ShapeDtypeStruct((bsz, D_LRU), f32),
    ]
    for w in to_cast:
        assert w.shape[0] == n_sb * n_tt, (w.shape, n_sb * n_tt)
        blk = (1,) + w.shape[1:]
        in_specs.append(pl.BlockSpec(blk, lambda q: (q, 0, 0)))
        out_specs.append(pl.BlockSpec(blk, lambda q: (q, 0, 0)))
        out_shape.append(jax.ShapeDtypeStruct(w.shape, bf16))
    scratch = [
        pltpu.VMEM((N_SLABS, nb * pitch, LANES), f32),
        pltpu.VMEM((tb, nb, D), f32),
        pltpu.VMEM((tm, D), bf16),
        pltpu.VMEM(((tb + POOL_BUF) * nb, D_POOL), f32),
        pltpu.VMEM(((tb + CONV_W - 1) * nb, D_LRU), f32),
        pltpu.VMEM((tm, D_LRU), f32),
        pltpu.VMEM((tm, D_LRU), f32),
        pltpu.VMEM((tm, D_LRU), f32),
        pltpu.VMEM((nb, D_LRU), f32),
    ]
    body = functools.partial(_mix_body, nb, tb, pitch, n_prev, n_tt, len(to_cast))
    return pl.pallas_call(
        body,
        grid=(n_sb * n_tt,),
        in_specs=in_specs,
        out_specs=out_specs,
        out_shape=out_shape,
        scratch_shapes=scratch,
        compiler_params=pltpu.CompilerParams(
            dimension_semantics=("arbitrary",),
            vmem_limit_bytes=VMEM_LIMIT),
        name=f"mix_nb{nb}_tb{tb}",
    )(x, mod, pool0, conv0, h0, *wts, *to_cast)


def _moe_body(n_half, nb, tb, pitch,
              x1_ref, mod_ref, w_rt_ref, b_rt_ref, tri_ref, wg_ref, wu_ref, wd_ref, ln2g_ref, ln2b_ref,
              y_ref,
              u2, tok_t, tok3, rows_s, cnt_s, ys_all, yacc, slab):
    tm = MOE_ROWS
    hm = nb * tb
    tq = tb // N_RB
    rq = tq * nb
    gs = pl.program_id(1)
    last_gs = N_GROUPS // GPS - 1

    def mod_part(h, k):
        return mod_ref[pl.ds(h * nb, nb), pl.ds(k * D, D)]

    @pl.when(gs == 0)
    def _route():
        for h in range(n_half):
            sh2 = mod_part(h, 3)
            sc2 = mod_part(h, 4)
            for r in range(N_RB):
                rows = pl.ds(h * hm + r * rq, rq)
                v = x1_ref[rows, :].reshape(tq, nb, D) * (1.0 + sc2)[None] + sh2[None]
                u2[rows, :] = v.reshape(rq, D).astype(bf16)
        lt = lax.dot_general(w_rt_ref[...], u2[...], (((1,), (1,)), ((), ())),
                             preferred_element_type=f32) + b_rt_ref[...]
        gl = [lt[k:k + 1, :] for k in range(N_GROUPS)]
        best = gl[0]
        gsel = jnp.zeros_like(best, dtype=i32)
        for k in range(1, N_GROUPS):
            better = gl[k] > best
            best = jnp.where(better, gl[k], best)
            gsel = jnp.where(better, k, gsel)
        denom = jnp.exp(gl[0] - best)
        for k in range(1, N_GROUPS):
            denom = denom + jnp.exp(gl[k] - best)
        p_sel = 1.0 / denom
        es = []
        for j in range(EPG):
            v_j = lt[SUBLANES + j:SUBLANES + j + 1, :]
            for k in range(1, N_GROUPS):
                r0 = SUBLANES * (k + 1) + j
                v_j = jnp.where(gsel == k, lt[r0:r0 + 1, :], v_j)
            es.append(v_j)
        v1 = es[0]
        i1 = jnp.zeros_like(gsel)
        for j in range(1, EPG):
            better = es[j] > v1
            v1 = jnp.where(better, es[j], v1)
            i1 = jnp.where(better, j, i1)
        v2 = jnp.full_like(v1, -jnp.inf)
        i2 = jnp.full_like(gsel, -1)
        for j in range(EPG):
            cand = jnp.logical_and(i1 != j, jnp.logical_or(i2 < 0, es[j] > v2))
            v2 = jnp.where(cand, es[j], v2)
            i2 = jnp.where(cand, j, i2)
        e21 = jnp.exp(v2 - v1)
        w1 = p_sel / (1.0 + e21)
        w2 = p_sel * e21 / (1.0 + e21)
        rid8 = lax.broadcasted_iota(i32, (SUBLANES, tm), 0)
        onehot = jnp.where(rid8 == gsel, 1.0, 0.0)
        blocks = [onehot[:, j * LANES:(j + 1) * LANES] for j in range(tm // LANES)]
        inner = _dot(jnp.concatenate(blocks, axis=0).astype(bf16), tri_ref[...])
        before = jnp.zeros((SUBLANES, 1), f32)
        prefs = []
        for j, blk in enumerate(blocks):
            prefs.append(inner[j * SUBLANES:(j + 1) * SUBLANES, :] + before)
            before = before + jnp.sum(blk, axis=1, keepdims=True)
        pref = jnp.concatenate(prefs, axis=1)
        rank = jnp.sum(onehot * pref, axis=0, keepdims=True)
        n_max = 0
        for k in range(N_GROUPS):
            n_k = jnp.sum(jnp.where(gsel == k, 1.0, 0.0)).astype(i32)
            cnt_s[k] = n_k
            n_max = jnp.maximum(n_max, n_k)
        cnt_s[N_GROUPS] = (n_max > CHUNK).astype(i32)
        rows_s[0:1, :] = gsel
        rows_s[1:2, :] = rank.astype(i32)
        rid = lax.broadcasted_iota(i32, (REC_ROWS, tm), 0)
        rec = jnp.zeros((REC_ROWS, tm), f32)
        for j in range(EPG):
            wj = jnp.where(i1 == j, w1, jnp.where(i2 == j, w2, 0.0))
            hi = wj.astype(bf16).astype(f32)
            mid = (wj - hi).astype(bf16).astype(f32)
            lo = (wj - hi - mid).astype(bf16).astype(f32)
            rec = jnp.where(rid == j, hi, rec)
            rec = jnp.where(rid == ROW_LO + j, mid, rec)
            rec = jnp.where(rid == ROW_LO2 + j, lo, rec)
        rec = jnp.where(rid == ROW_GSEL, gsel.astype(f32), rec)
        rec = jnp.where(rid == ROW_RANK, rank, rec)
        rec = jnp.concatenate([rec, jnp.zeros((LANES - REC_ROWS, tm), f32)], axis=0)
        rt = rec.T
        tok_t[...] = rt
        tok3[...] = rt.astype(bf16)

    def sorted_experts(gi, base):
        g = gs * GPS + gi
        gsel_row = rows_s[0:1, :]
        rank_row = rows_s[1:2, :]
        jj = lax.broadcasted_iota(i32, (CHUNK, tm), 0) + base
        pm = jnp.where(jnp.logical_and(gsel_row == g, rank_row == jj), 1.0, 0.0).astype(bf16)
        xs = _dot(pm, u2[...]).astype(bf16)
        cs3 = _dot(pm, tok3[...])
        cs = (cs3 + pltpu.roll(cs3, LANES - ROW_LO, axis=1)
              + pltpu.roll(cs3, LANES - ROW_LO2, axis=1))
        acts = []
        for e in range(EPG):
            hg = _dot(xs, wg_ref[gi * EPG + e])
            hu = _dot(xs, wu_ref[gi * EPG + e])
            acts.append((hg * _sigmoid(hg) * hu * cs[:, e:e + 1]).astype(bf16))
        hb = jnp.concatenate(acts, axis=1)
        wd_g = wd_ref[pl.ds(gi * EPG, EPG)].reshape(EPG * D_EXPERT, D)
        return _dot(hb, wd_g).astype(bf16)

    for gi in range(GPS):
        first_row = pl.multiple_of((gs * GPS + gi) * CHUNK_PAD, CHUNK_PAD)
        ys_all[pl.ds(first_row, CHUNK), :] = sorted_experts(gi, 0)

    @pl.when(gs == 0)
    def _zero_pad_rows():
        for k in range(N_GROUPS):
            ys_all[pl.ds(k * CHUNK_PAD + CHUNK, CHUNK_PAD - CHUNK), :] = jnp.zeros((CHUNK_PAD - CHUNK, D), bf16)

    has_overflow = cnt_s[N_GROUPS] > 0

    @pl.when(jnp.logical_and(gs == 0, has_overflow))
    def _zero_acc():
        yacc[...] = jnp.zeros((tm, D), f32)

    for gi in range(GPS):
        g = gs * GPS + gi

        def overflow(c, carry, gi=gi, g=g):
            base = c * CHUNK
            ys = sorted_experts(gi, base)
            rec = tok_t[...]
            gsel_col = rec[:, ROW_GSEL:ROW_GSEL + 1]
            rank_col = rec[:, ROW_RANK:ROW_RANK + 1]
            jl = (lax.broadcasted_iota(i32, (tm, CHUNK), 1) + base).astype(f32)
            pt = jnp.where(jnp.logical_and(gsel_col == g.astype(f32), rank_col == jl), 1.0, 0.0).astype(bf16)
            yacc[...] += _dot(pt, ys)
            return carry

        lax.fori_loop(1, pl.cdiv(cnt_s[g], CHUNK), overflow, 0)

    def finish(with_acc):
        tqf = FIN_ROWS // nb
        jl = lax.broadcasted_iota(i32, (FIN_ROWS, N_GROUPS * CHUNK_PAD), 1).astype(f32)
        for h in range(n_half):
            g2 = mod_part(h, 5)
            for r in range(hm // FIN_ROWS):
                rows = pl.ds(h * hm + r * FIN_ROWS, FIN_ROWS)
                rec = tok_t[rows, :]
                gsel_col = rec[:, ROW_GSEL:ROW_GSEL + 1]
                rank_col = rec[:, ROW_RANK:ROW_RANK + 1]
                key = jnp.where(rank_col < f32(CHUNK), gsel_col * f32(CHUNK_PAD) + rank_col, -1.0)
                pt = jnp.where(key == jl, 1.0, 0.0).astype(bf16)
                y = _dot(pt, ys_all[...])
                if with_acc:
                    y = y + yacc[rows, :]
                v = ALPHA * x1_ref[rows, :] + ((1.0 + g2)[None] * y.reshape(tqf, nb, D)).reshape(FIN_ROWS, D)
                yn = _layer_norm(v, ln2g_ref[...], ln2b_ref[...])
                for tl in range(tqf):
                    for j in range(N_SLABS):
                        slab[j, pl.ds(r * tqf + tl, nb, stride=pitch), :] = (
                            yn[tl * nb:(tl + 1) * nb, j * LANES:(j + 1) * LANES])
            for b in range(nb):
                for j in range(N_SLABS):
                    y_ref[h * nb + b, :, pl.ds(j * LANES, LANES)] = slab[j, pl.ds(b * pitch, tb), :]

    @pl.when(jnp.logical_and(gs == last_gs, jnp.logical_not(has_overflow)))
    def _finish():
        finish(False)

    @pl.when(jnp.logical_and(gs == last_gs, has_overflow))
    def _finish_with_overflow():
        finish(True)


def _run_moe(x1, mod, mod_blk0, n_half, nb, tb, out_seqs, out_len, wts):
    n_tiles = x1.shape[0] // MOE_ROWS
    assert n_half * nb * tb == MOE_ROWS
    seq_per_tile = n_half * nb
    n_tt = out_len // tb
    pitch = _slab_pitch(tb)
    st = lambda q: (q // n_tt, q % n_tt)
    const2 = lambda q, g: (0, 0)
    grp3 = lambda q, g: (g, 0, 0)
    single = pl.Buffered(1)
    in_specs = [
        pl.BlockSpec((MOE_ROWS, D), lambda q, g: (q, 0)),
        pl.BlockSpec((seq_per_tile, 6 * D), lambda q, g: (mod_blk0 + st(q)[0], 0)),
        pl.BlockSpec((ROUTE_ROWS, D), const2),
        pl.BlockSpec((ROUTE_ROWS, 1), const2),
        pl.BlockSpec((LANES, LANES), const2),
        pl.BlockSpec((GPS * EPG, D, D_EXPERT), grp3),
        pl.BlockSpec((GPS * EPG, D, D_EXPERT), grp3),
        pl.BlockSpec((GPS * EPG, D_EXPERT, D), grp3),
        pl.BlockSpec((1, D), const2),
        pl.BlockSpec((1, D), const2),
    ]
    scratch = [
        pltpu.VMEM((MOE_ROWS, D), bf16),
        pltpu.VMEM((MOE_ROWS, LANES), f32),
        pltpu.VMEM((MOE_ROWS, LANES), bf16),
        pltpu.VMEM((SUBLANES, MOE_ROWS), i32),
        pltpu.SMEM((N_GROUPS + 1,), i32),
        pltpu.VMEM((N_GROUPS * CHUNK_PAD, D), bf16),
        pltpu.VMEM((MOE_ROWS, D), f32),
        pltpu.VMEM((N_SLABS, nb * pitch, LANES), f32),
    ]
    body = functools.partial(_moe_body, n_half, nb, tb, pitch)
    return pl.pallas_call(
        body,
        grid=(n_tiles, N_GROUPS // GPS),
        in_specs=in_specs,
        out_specs=pl.BlockSpec((seq_per_tile, tb, D), lambda q, g: (*st(q), 0)),
        out_shape=jax.ShapeDtypeStruct((out_seqs, out_len, D), f32),
        scratch_shapes=scratch,
        compiler_params=pltpu.CompilerParams(
            dimension_semantics=("arbitrary", "arbitrary"),
            vmem_limit_bytes=VMEM_LIMIT),
        name=f"moe_nb{nb}_tb{tb}",
    )(x1, mod, *wts)


def _pair_blocks(w):
    n, k, _ = w.shape
    wp = w.reshape(n // 2, 2, k, k)
    eye = jnp.eye(2, dtype=w.dtype)
    return jnp.einsum('phij,hg->phigj', wp, eye).reshape(n // 2, 2 * k, 2 * k)


def kernel(x_prompt, x_sample, c_prompt, c_sample, state_pool, state_conv, state_lru, w_ada, b_ada, w_in, w_pool, pool_scale, w_conv, b_conv, w_a, b_a, w_x, b_x, lru_lambda, w_out, ln1_g, ln1_b, w_group, b_group, w_route, b_route, w_gate, w_up, w_down, ln2_g, ln2_b):
    l = 0
    bp, tp, _ = x_prompt.shape
    bs, ts, _ = x_sample.shape
    mod = _ada_mod(jnp.concatenate([c_sample, c_prompt], axis=0), w_ada[l], b_ada[l])

    w_rt = jnp.zeros((ROUTE_ROWS, D), f32).at[0:N_GROUPS].set(w_group[l].T)
    b_rt = jnp.zeros((ROUTE_ROWS,), f32).at[0:N_GROUPS].set(b_group[l])
    for k in range(N_GROUPS):
        r0 = SUBLANES * (k + 1)
        w_rt = w_rt.at[r0:r0 + EPG].set(w_route[l][:, k * EPG:(k + 1) * EPG].T)
        b_rt = b_rt.at[r0:r0 + EPG].set(b_route[l][k * EPG:(k + 1) * EPG])
    tri = jnp.triu(jnp.ones((LANES, LANES), bf16), 1)

    mix_wts = (
        w_in[l].astype(bf16),
        w_pool[l].astype(bf16),
        pool_scale[l].reshape(1, D_POOL),
        w_conv[l],
        b_conv[l].reshape(1, D_LRU),
        jnp.stack([_pair_blocks(w_a[l]), _pair_blocks(w_x[l])]).astype(bf16),
        jnp.concatenate([b_a[l], b_x[l]]).reshape(1, 2 * D_LRU),
        lru_lambda[l].reshape(1, D_LRU),
        w_out[l].astype(bf16),
        ln1_g[l].reshape(1, D),
        ln1_b[l].reshape(1, D),
    )
    tb_p = MOE_ROWS // bp
    zp = jnp.zeros((POOL_BUF, bp, D_POOL), f32)
    zc = jnp.zeros((CONV_W - 1, bp, D_LRU), f32)
    zh = jnp.zeros((bp, D_LRU), f32)
    x1p, pool_p, conv_p, lru_p, wg_b, wu_b, wd_b = _run_mix(
        x_prompt, mod, bs // bp, zp, zc, zh, 0, bp, tb_p, mix_wts, (w_gate[l], w_up[l], w_down[l]))
    moe_wts = (
        w_rt.astype(bf16),
        b_rt.reshape(ROUTE_ROWS, 1),
        tri,
        wg_b,
        wu_b,
        wd_b,
        ln2_g[l].reshape(1, D),
        ln2_b[l].reshape(1, D),
    )
    yp = _run_moe(x1p, mod, bs // bp, 1, bp, tb_p, bp, tp, moe_wts)

    n_prev_s = min(PAST_LEN, POOL_BUF)
    nb_s = bs // 2
    x1s, pool_s, conv_s, lru_s = _run_mix(
        x_sample, mod, 0, state_pool[l].transpose(1, 0, 2), state_conv[l].transpose(1, 0, 2),
        state_lru[l], n_prev_s, nb_s, ts, mix_wts)
    ys = _run_moe(x1s, mod, 0, 2, nb_s, ts, bs, ts, moe_wts)

    tr = lambda a: a.transpose(1, 0, 2)[None]
    return (yp, ys, tr(pool_p), tr(conv_p), lru_p[None], tr(pool_s), tr(conv_s), lru_s[None])
```

```python
import functools
import math

import jax
import jax.numpy as jnp
from jax import lax
from jax.experimental import pallas as pl
from jax.experimental.pallas import tpu as pltpu

D = 1024
D_POOL = 512
D_LRU = 512
D_IN = D_POOL + 2 * D_LRU
POOL_WINDOWS = (2, 4, 8, 16)
POOL_GROUP = 128
POOL_BUF = 15
CONV_W = 4
LRU_C = 8.0
N_GROUPS = 4
EPG = 4
D_EXPERT = 256
DEPTH = 1
ALPHA = (2.0 * DEPTH) ** 0.25
LN_EPS = 1e-5
PAST_LEN = 16384

LANES = 128
SUBLANES = 8
MXU_DIM = 256
N_SLABS = D // LANES
ROUTE_ROWS = 48
MOE_ROWS = 1024
CHUNK = 288
CHUNK_PAD = 320
assert CHUNK <= CHUNK_PAD and (N_GROUPS * CHUNK_PAD) % MXU_DIM == 0
ROW_LO = 4
ROW_LO2 = 8
ROW_GSEL = 12
ROW_RANK = 13
REC_ROWS = 16
N_RB = 4
GPS = 2
FIN_ROWS = 256
ADA_STEPS = 8
V7X_VMEM_BYTES = 64 * 1024 * 1024
VMEM_RESERVE_BYTES = 4 * 1024 * 1024
VMEM_LIMIT = V7X_VMEM_BYTES - VMEM_RESERVE_BYTES
LOG2E = 1.0 / math.log(2.0)
SQRT_GUARD = 1e-30

f32 = jnp.float32
bf16 = jnp.bfloat16
i32 = jnp.int32


def _dot(a, b):
    return jnp.dot(a, b, preferred_element_type=f32)


def _sigmoid(x):
    return 1.0 / (1.0 + jnp.exp2(x * (-LOG2E)))


def _gelu_tanh(x):
    k1 = -2.0 * math.sqrt(2.0 / math.pi) * LOG2E
    return x / (1.0 + jnp.exp2(x * (k1 + (k1 * 0.044715) * (x * x))))


def _layer_norm(v, g, b):
    mu = jnp.mean(v, axis=-1, keepdims=True)
    c = v - mu
    var = jnp.mean(c * c, axis=-1, keepdims=True)
    return c * lax.rsqrt(var + LN_EPS) * g + b


def _slab_pitch(tb):
    return tb + SUBLANES if (tb // SUBLANES) % 2 == 0 else tb


def _ada_body(c_ref, w_ref, b_ref, w_in_ref, w_out_ref, o_ref, w_in_b_ref, w_out_b_ref):
    o_ref[...] = _dot(c_ref[...].astype(bf16), w_ref[...].astype(bf16)) + b_ref[...]
    w_in_b_ref[...] = w_in_ref[...].astype(bf16)
    w_out_b_ref[...] = w_out_ref[...].astype(bf16)


def _ada_mod(c_all, w_ada, b_ada, w_in, w_out):
    n = c_all.shape[0]
    steps = ADA_STEPS
    bn = 6 * D // steps
    br = D // steps
    return pl.pallas_call(
        _ada_body,
        grid=(steps,),
        in_specs=[
            pl.BlockSpec((n, D), lambda j: (0, 0)),
            pl.BlockSpec((D, bn), lambda j: (0, j)),
            pl.BlockSpec((1, bn), lambda j: (0, j)),
            pl.BlockSpec((br, D_IN), lambda j: (j, 0)),
            pl.BlockSpec((br, D), lambda j: (j, 0)),
        ],
        out_specs=[
            pl.BlockSpec((n, bn), lambda j: (0, j)),
            pl.BlockSpec((br, D_IN), lambda j: (j, 0)),
            pl.BlockSpec((br, D), lambda j: (j, 0)),
        ],
        out_shape=[
            jax.ShapeDtypeStruct((n, 6 * D), f32),
            jax.ShapeDtypeStruct((D, D_IN), bf16),
            jax.ShapeDtypeStruct((D, D), bf16),
        ],
        name="ada_mod",
    )(c_all, w_ada, b_ada.reshape(1, 6 * D), w_in, w_out)


def _mix_body(nb, tb, pitch, n_prev, n_tt, n_cast,
              x_ref, mod_ref, pool0_ref, conv0_ref, h0_ref,
              w_in_ref, w_pool_ref, pscale_ref, w_conv_ref, b_conv_ref,
              w_ax_ref, b_ax_ref, lam_ref, w_out_ref, ln1g_ref, ln1b_ref,
              *rest):
    cast_in = rest[:n_cast]
    x1_ref, npool_ref, nconv_ref, nh_ref = rest[n_cast:n_cast + 4]
    cast_out = rest[n_cast + 4:2 * n_cast + 4]
    slab, xt, ycat, zpool, zconv, a_s, b_s, gl_s, h_s = rest[2 * n_cast + 4:]
    for src, dst in zip(cast_in, cast_out):
        dst[...] = src[...].astype(bf16)
    tm = nb * tb
    tq = tb // N_RB
    rq = tq * nb
    ti = pl.program_id(0) % n_tt

    def mod_part(k):
        return mod_ref[:, pl.ds(k * D, D)]

    @pl.when(ti == 0)
    def _init_state():
        zpool[pl.ds(0, POOL_BUF * nb), :] = pool0_ref[...].reshape(POOL_BUF * nb, D_POOL)
        zconv[pl.ds(0, (CONV_W - 1) * nb), :] = conv0_ref[...].reshape((CONV_W - 1) * nb, D_LRU)
        h_s[...] = h0_ref[...]

    for b in range(nb):
        for j in range(N_SLABS):
            slab[j, pl.ds(b * pitch, tb), :] = x_ref[b, :, pl.ds(j * LANES, LANES)]

    sh1 = mod_part(0)
    sc1 = mod_part(1)
    g1 = mod_part(2)
    nl = -lam_ref[...]
    softplus = jnp.maximum(nl, 0.0) + jnp.log(1.0 + jnp.exp(-jnp.abs(nl)))
    log2a_unit = (-LRU_C * LOG2E) * softplus
    row = lax.broadcasted_iota(i32, (tm, LANES), 0)
    t_loc = lax.shift_right_logical(row, int(math.log2(nb)))
    h = h_s[...]

    for t in range(tb):
        for j in range(N_SLABS):
            xt[t, :, pl.ds(j * LANES, LANES)] = slab[j, pl.ds(t, nb, stride=pitch), :]

    for r in range(N_RB):
        u = xt[pl.ds(r * tq, tq)] * (1.0 + sc1)[None] + sh1[None]
        ycat[pl.ds(r * rq, rq), :] = u.reshape(rq, D).astype(bf16)
    ub = ycat[...]
    zpool[pl.ds(POOL_BUF * nb, tm), :] = _dot(ub, w_in_ref[:, pl.ds(0, D_POOL)])
    zconv[pl.ds((CONV_W - 1) * nb, tm), :] = _dot(ub, w_in_ref[:, pl.ds(D_POOL, D_LRU)])
    half = D_LRU // 2
    for hpart in range(2):
        cols = pl.ds(D_POOL + D_LRU + hpart * half, half)
        gl_s[:, pl.ds(hpart * half, half)] = _gelu_tanh(_dot(ub, w_in_ref[:, cols]))

    t_glob = (ti * tb + t_loc + (1 + n_prev)).astype(f32)
    for c, w in enumerate(POOL_WINDOWS):
        lanes = pl.ds(c * LANES, LANES)
        s = zpool[pl.ds((POOL_BUF + 1 - w) * nb, tm + (w - 1) * nb), lanes]
        step = 1
        while step < w:
            s = s[step * nb:] + s[:-step * nb]
            step *= 2
        cnt = jnp.minimum(f32(w), t_glob)
        dlt = s / cnt - zpool[pl.ds(POOL_BUF * nb, tm), lanes]
        yp = _dot(dlt.astype(bf16), w_pool_ref[c]) * pscale_ref[:, lanes]
        ycat[:, lanes] = yp.astype(bf16)

    for j in range(D_LRU // LANES):
        lanes = pl.ds(j * LANES, LANES)
        xc = b_conv_ref[:, lanes] + zconv[pl.ds(0, tm), lanes] * w_conv_ref[0:1, lanes]
        for k in range(1, CONV_W):
            xc = xc + zconv[pl.ds(k * nb, tm), lanes] * w_conv_ref[k:k + 1, lanes]
        xcb = xc.astype(bf16)
        r = _sigmoid(_dot(xcb, w_ax_ref[0, j]) + b_ax_ref[:, lanes])
        ig = _sigmoid(_dot(xcb, w_ax_ref[1, j]) + b_ax_ref[:, pl.ds(D_LRU + j * LANES, LANES)])
        a = jnp.exp2(r * log2a_unit[:, j * LANES:(j + 1) * LANES])
        a_s[:, lanes] = a
        om = jnp.maximum(1.0 - a * a, 0.0)
        mult = om * lax.rsqrt(jnp.maximum(om, SQRT_GUARD))
        b_s[:, lanes] = mult * ig * xc

    for t in range(tb):
        rows = pl.ds(t * nb, nb)
        h = a_s[rows, :] * h + b_s[rows, :]
        b_s[rows, :] = h
    ycat[:, pl.ds(D_POOL, D_LRU)] = (b_s[...] * gl_s[...]).astype(bf16)

    yc = ycat[...]
    cb = D // N_RB
    for j in range(N_RB):
        cols = pl.ds(j * cb, cb)
        mix = _dot(yc, w_out_ref[:, cols]).reshape(tb, nb, cb)
        xt[:, :, cols] = ALPHA * xt[:, :, cols] + (1.0 + g1[:, j * cb:(j + 1) * cb])[None] * mix
    for r in range(N_RB):
        xn = _layer_norm(xt[pl.ds(r * tq, tq)], ln1g_ref[...][None], ln1b_ref[...][None])
        x1_ref[pl.ds(r * rq, rq), :] = xn.reshape(rq, D)

    h_s[...] = h
    zpool[pl.ds(0, POOL_BUF * nb), :] = zpool[pl.ds(tm, POOL_BUF * nb), :]
    zconv[pl.ds(0, (CONV_W - 1) * nb), :] = zconv[pl.ds(tm, (CONV_W - 1) * nb), :]

    @pl.when(ti == n_tt - 1)
    def _emit_state():
        npool_ref[...] = zpool[pl.ds(0, POOL_BUF * nb), :].reshape(POOL_BUF, nb, D_POOL)
        nconv_ref[...] = zconv[pl.ds(0, (CONV_W - 1) * nb), :].reshape(CONV_W - 1, nb, D_LRU)
        nh_ref[...] = h_s[...]


def _run_mix(x, mod, mod_blk0, pool0, conv0, h0, n_prev, nb, tb, wts, to_cast=()):
    bsz, t_len, _ = x.shape
    n_sb = bsz // nb
    n_tt = t_len // tb
    tm = nb * tb
    pitch = _slab_pitch(tb)
    st = lambda q: (q // n_tt, q % n_tt)
    const2 = lambda q: (0, 0)
    const3 = lambda q: (0, 0, 0)
    const4 = lambda q: (0, 0, 0, 0)
    single = pl.Buffered(1)
    in_specs = [
        pl.BlockSpec((nb, tb, D), lambda q: (*st(q), 0)),
        pl.BlockSpec((nb, 6 * D), lambda q: (mod_blk0 + st(q)[0], 0)),
        pl.BlockSpec((POOL_BUF, nb, D_POOL), lambda q: (0, st(q)[0], 0)),
        pl.BlockSpec((CONV_W - 1, nb, D_LRU), lambda q: (0, st(q)[0], 0)),
        pl.BlockSpec((nb, D_LRU), lambda q: (st(q)[0], 0)),
        pl.BlockSpec((D, D_IN), const2, pipeline_mode=single),
        pl.BlockSpec((4, POOL_GROUP, POOL_GROUP), const3),
        pl.BlockSpec((1, D_POOL), const2),
        pl.BlockSpec((CONV_W, D_LRU), const2),
        pl.BlockSpec((1, D_LRU), const2),
        pl.BlockSpec((2, D_LRU // LANES, LANES, LANES), const4),
        pl.BlockSpec((1, 2 * D_LRU), const2),
        pl.BlockSpec((1, D_LRU), const2),
        pl.BlockSpec((D, D), const2, pipeline_mode=single),
        pl.BlockSpec((1, D), const2),
        pl.BlockSpec((1, D), const2),
    ]
    out_specs = [
        pl.BlockSpec((tm, D), lambda q: (q, 0)),
        pl.BlockSpec((POOL_BUF, nb, D_POOL), lambda q: (0, st(q)[0], 0)),
        pl.BlockSpec((CONV_W - 1, nb, D_LRU), lambda q: (0, st(q)[0], 0)),
        pl.BlockSpec((nb, D_LRU), lambda q: (st(q)[0], 0)),
    ]
    out_shape = [
        jax.ShapeDtypeStruct((bsz * t_len, D), f32),
        jax.ShapeDtypeStruct((POOL_BUF, bsz, D_POOL), f32),
        jax.ShapeDtypeStruct((CONV_W - 1, bsz, D_LRU), f32),
        jax.ShapeDtypeStruct((bsz, D_LRU), f32),
    ]
    for w in to_cast:
        assert w.shape[0] == n_sb * n_tt, (w.shape, n_sb * n_tt)
        blk = (1,) + w.shape[1:]
        in_specs.append(pl.BlockSpec(blk, lambda q: (q, 0, 0)))
        out_specs.append(pl.BlockSpec(blk, lambda q: (q, 0, 0)))
        out_shape.append(jax.ShapeDtypeStruct(w.shape, bf16))
    scratch = [
        pltpu.VMEM((N_SLABS, nb * pitch, LANES), f32),
        pltpu.VMEM((tb, nb, D), f32),
        pltpu.VMEM((tm, D), bf16),
        pltpu.VMEM(((tb + POOL_BUF) * nb, D_POOL), f32),
        pltpu.VMEM(((tb + CONV_W - 1) * nb, D_LRU), f32),
        pltpu.VMEM((tm, D_LRU), f32),
        pltpu.VMEM((tm, D_LRU), f32),
        pltpu.VMEM((tm, D_LRU), f32),
        pltpu.VMEM((nb, D_LRU), f32),
    ]
    body = functools.partial(_mix_body, nb, tb, pitch, n_prev, n_tt, len(to_cast))
    return pl.pallas_call(
        body,
        grid=(n_sb * n_tt,),
        in_specs=in_specs,
        out_specs=out_specs,
        out_shape=out_shape,
        scratch_shapes=scratch,
        compiler_params=pltpu.CompilerParams(
            dimension_semantics=("arbitrary",),
            vmem_limit_bytes=VMEM_LIMIT),
        name=f"mix_nb{nb}_tb{tb}",
    )(x, mod, pool0, conv0, h0, *wts, *to_cast)


def _moe_body(n_half, nb, tb, pitch,
              x1_ref, mod_ref, w_rt_ref, b_rt_ref, tri_ref, wg_ref, wu_ref, wd_ref, ln2g_ref, ln2b_ref,
              y_ref,
              u2, tok_t, tok3, rows_s, cnt_s, ys_all, yacc, slab):
    tm = MOE_ROWS
    hm = nb * tb
    tq = tb // N_RB
    rq = tq * nb
    gs = pl.program_id(1)
    last_gs = N_GROUPS // GPS - 1

    def mod_part(h, k):
        return mod_ref[pl.ds(h * nb, nb), pl.ds(k * D, D)]

    @pl.when(gs == 0)
    def _route():
        for h in range(n_half):
            sh2 = mod_part(h, 3)
            sc2 = mod_part(h, 4)
            for r in range(N_RB):
                rows = pl.ds(h * hm + r * rq, rq)
                v = x1_ref[rows, :].reshape(tq, nb, D) * (1.0 + sc2)[None] + sh2[None]
                u2[rows, :] = v.reshape(rq, D).astype(bf16)
        lt = lax.dot_general(w_rt_ref[...], u2[...], (((1,), (1,)), ((), ())),
                             preferred_element_type=f32) + b_rt_ref[...]
        gl = [lt[k:k + 1, :] for k in range(N_GROUPS)]
        best = gl[0]
        gsel = jnp.zeros_like(best, dtype=i32)
        for k in range(1, N_GROUPS):
            better = gl[k] > best
            best = jnp.where(better, gl[k], best)
            gsel = jnp.where(better, k, gsel)
        denom = jnp.exp(gl[0] - best)
        for k in range(1, N_GROUPS):
            denom = denom + jnp.exp(gl[k] - best)
        p_sel = 1.0 / denom
        es = []
        for j in range(EPG):
            v_j = lt[SUBLANES + j:SUBLANES + j + 1, :]
            for k in range(1, N_GROUPS):
                r0 = SUBLANES * (k + 1) + j
                v_j = jnp.where(gsel == k, lt[r0:r0 + 1, :], v_j)
            es.append(v_j)
        v1 = es[0]
        i1 = jnp.zeros_like(gsel)
        for j in range(1, EPG):
            better = es[j] > v1
            v1 = jnp.where(better, es[j], v1)
            i1 = jnp.where(better, j, i1)
        v2 = jnp.full_like(v1, -jnp.inf)
        i2 = jnp.full_like(gsel, -1)
        for j in range(EPG):
            cand = jnp.logical_and(i1 != j, jnp.logical_or(i2 < 0, es[j] > v2))
            v2 = jnp.where(cand, es[j], v2)
            i2 = jnp.where(cand, j, i2)
        e21 = jnp.exp(v2 - v1)
        w1 = p_sel / (1.0 + e21)
        w2 = p_sel * e21 / (1.0 + e21)
        rid8 = lax.broadcasted_iota(i32, (SUBLANES, tm), 0)
        onehot = jnp.where(rid8 == gsel, 1.0, 0.0)
        blocks = [onehot[:, j * LANES:(j + 1) * LANES] for j in range(tm // LANES)]
        inner = _dot(jnp.concatenate(blocks, axis=0).astype(bf16), tri_ref[...])
        before = jnp.zeros((SUBLANES, 1), f32)
        prefs = []
        for j, blk in enumerate(blocks):
            prefs.append(inner[j * SUBLANES:(j + 1) * SUBLANES, :] + before)
            before = before + jnp.sum(blk, axis=1, keepdims=True)
        pref = jnp.concatenate(prefs, axis=1)
        rank = jnp.sum(onehot * pref, axis=0, keepdims=True)
        n_max = 0
        for k in range(N_GROUPS):
            n_k = jnp.sum(jnp.where(gsel == k, 1.0, 0.0)).astype(i32)
            cnt_s[k] = n_k
            n_max = jnp.maximum(n_max, n_k)
        cnt_s[N_GROUPS] = (n_max > CHUNK).astype(i32)
        rows_s[0:1, :] = gsel
        rows_s[1:2, :] = rank.astype(i32)
        rid = lax.broadcasted_iota(i32, (REC_ROWS, tm), 0)
        rec = jnp.zeros((REC_ROWS, tm), f32)
        for j in range(EPG):
            wj = jnp.where(i1 == j, w1, jnp.where(i2 == j, w2, 0.0))
            hi = wj.astype(bf16).astype(f32)
            mid = (wj - hi).astype(bf16).astype(f32)
            lo = (wj - hi - mid).astype(bf16).astype(f32)
            rec = jnp.where(rid == j, hi, rec)
            rec = jnp.where(rid == ROW_LO + j, mid, rec)
            rec = jnp.where(rid == ROW_LO2 + j, lo, rec)
        rec = jnp.where(rid == ROW_GSEL, gsel.astype(f32), rec)
        rec = jnp.where(rid == ROW_RANK, rank, rec)
        rec = jnp.concatenate([rec, jnp.zeros((LANES - REC_ROWS, tm), f32)], axis=0)
        rt = rec.T
        tok_t[...] = rt
        tok3[...] = rt.astype(bf16)

    def sorted_experts(gi, base):
        g = gs * GPS + gi
        gsel_row = rows_s[0:1, :]
        rank_row = rows_s[1:2, :]
        jj = lax.broadcasted_iota(i32, (CHUNK, tm), 0) + base
        pm = jnp.where(jnp.logical_and(gsel_row == g, rank_row == jj), 1.0, 0.0).astype(bf16)
        xs = _dot(pm, u2[...]).astype(bf16)
        cs3 = _dot(pm, tok3[...])
        cs = (cs3 + pltpu.roll(cs3, LANES - ROW_LO, axis=1)
              + pltpu.roll(cs3, LANES - ROW_LO2, axis=1))
        acts = []
        for e in range(EPG):
            hg = _dot(xs, wg_ref[gi * EPG + e])
            hu = _dot(xs, wu_ref[gi * EPG + e])
            acts.append((hg * _sigmoid(hg) * hu * cs[:, e:e + 1]).astype(bf16))
        hb = jnp.concatenate(acts, axis=1)
        wd_g = wd_ref[pl.ds(gi * EPG, EPG)].reshape(EPG * D_EXPERT, D)
        return _dot(hb, wd_g).astype(bf16)

    for gi in range(GPS):
        first_row = pl.multiple_of((gs * GPS + gi) * CHUNK_PAD, CHUNK_PAD)
        ys_all[pl.ds(first_row, CHUNK), :] = sorted_experts(gi, 0)

    @pl.when(gs == 0)
    def _zero_pad_rows():
        for k in range(N_GROUPS):
            ys_all[pl.ds(k * CHUNK_PAD + CHUNK, CHUNK_PAD - CHUNK), :] = jnp.zeros((CHUNK_PAD - CHUNK, D), bf16)

    has_overflow = cnt_s[N_GROUPS] > 0

    @pl.when(jnp.logical_and(gs == 0, has_overflow))
    def _zero_acc():
        yacc[...] = jnp.zeros((tm, D), f32)

    for gi in range(GPS):
        g = gs * GPS + gi

        def overflow(c, carry, gi=gi, g=g):
            base = c * CHUNK
            ys = sorted_experts(gi, base)
            rec = tok_t[...]
            gsel_col = rec[:, ROW_GSEL:ROW_GSEL + 1]
            rank_col = rec[:, ROW_RANK:ROW_RANK + 1]
            jl = (lax.broadcasted_iota(i32, (tm, CHUNK), 1) + base).astype(f32)
            pt = jnp.where(jnp.logical_and(gsel_col == g.astype(f32), rank_col == jl), 1.0, 0.0).astype(bf16)
            yacc[...] += _dot(pt, ys)
            return carry

        lax.fori_loop(1, pl.cdiv(cnt_s[g], CHUNK), overflow, 0)

    def finish(with_acc):
        tqf = FIN_ROWS // nb
        jl = lax.broadcasted_iota(i32, (FIN_ROWS, N_GROUPS * CHUNK_PAD), 1).astype(f32)
        for h in range(n_half):
            g2 = mod_part(h, 5)
            for r in range(hm // FIN_ROWS):
                rows = pl.ds(h * hm + r * FIN_ROWS, FIN_ROWS)
                rec = tok_t[rows, :]
                gsel_col = rec[:, ROW_GSEL:ROW_GSEL + 1]
                rank_col = rec[:, ROW_RANK:ROW_RANK + 1]
                key = jnp.where(rank_col < f32(CHUNK), gsel_col * f32(CHUNK_PAD) + rank_col, -1.0)
                pt = jnp.where(key == jl, 1.0, 0.0).astype(bf16)
                y = _dot(pt, ys_all[...])
                if with_acc:
                    y = y + yacc[rows, :]
                v = ALPHA * x1_ref[rows, :] + ((1.0 + g2)[None] * y.reshape(tqf, nb, D)).reshape(FIN_ROWS, D)
                yn = _layer_norm(v, ln2g_ref[...], ln2b_ref[...])
                for tl in range(tqf):
                    for j in range(N_SLABS):
                        slab[j, pl.ds(r * tqf + tl, nb, stride=pitch), :] = (
                            yn[tl * nb:(tl + 1) * nb, j * LANES:(j + 1) * LANES])
            for b in range(nb):
                for j in range(N_SLABS):
                    y_ref[h * nb + b, :, pl.ds(j * LANES, LANES)] = slab[j, pl.ds(b * pitch, tb), :]

    @pl.when(jnp.logical_and(gs == last_gs, jnp.logical_not(has_overflow)))
    def _finish():
        finish(False)

    @pl.when(jnp.logical_and(gs == last_gs, has_overflow))
    def _finish_with_overflow():
        finish(True)


def _run_moe(x1, mod, mod_blk0, n_half, nb, tb, out_seqs, out_len, wts):
    n_tiles = x1.shape[0] // MOE_ROWS
    assert n_half * nb * tb == MOE_ROWS
    seq_per_tile = n_half * nb
    n_tt = out_len // tb
    pitch = _slab_pitch(tb)
    st = lambda q: (q // n_tt, q % n_tt)
    const2 = lambda q, g: (0, 0)
    grp3 = lambda q, g: (g, 0, 0)
    single = pl.Buffered(1)
    in_specs = [
        pl.BlockSpec((MOE_ROWS, D), lambda q, g: (q, 0)),
        pl.BlockSpec((seq_per_tile, 6 * D), lambda q, g: (mod_blk0 + st(q)[0], 0)),
        pl.BlockSpec((ROUTE_ROWS, D), const2),
        pl.BlockSpec((ROUTE_ROWS, 1), const2),
        pl.BlockSpec((LANES, LANES), const2),
        pl.BlockSpec((GPS * EPG, D, D_EXPERT), grp3),
        pl.BlockSpec((GPS * EPG, D, D_EXPERT), grp3),
        pl.BlockSpec((GPS * EPG, D_EXPERT, D), grp3),
        pl.BlockSpec((1, D), const2),
        pl.BlockSpec((1, D), const2),
    ]
    scratch = [
        pltpu.VMEM((MOE_ROWS, D), bf16),
        pltpu.VMEM((MOE_ROWS, LANES), f32),
        pltpu.VMEM((MOE_ROWS, LANES), bf16),
        pltpu.VMEM((SUBLANES, MOE_ROWS), i32),
        pltpu.SMEM((N_GROUPS + 1,), i32),
        pltpu.VMEM((N_GROUPS * CHUNK_PAD, D), bf16),
        pltpu.VMEM((MOE_ROWS, D), f32),
        pltpu.VMEM((N_SLABS, nb * pitch, LANES), f32),
    ]
    body = functools.partial(_moe_body, n_half, nb, tb, pitch)
    return pl.pallas_call(
        body,
        grid=(n_tiles, N_GROUPS // GPS),
        in_specs=in_specs,
        out_specs=pl.BlockSpec((seq_per_tile, tb, D), lambda q, g: (*st(q), 0)),
        out_shape=jax.ShapeDtypeStruct((out_seqs, out_len, D), f32),
        scratch_shapes=scratch,
        compiler_params=pltpu.CompilerParams(
            dimension_semantics=("arbitrary", "arbitrary"),
            vmem_limit_bytes=VMEM_LIMIT),
        name=f"moe_nb{nb}_tb{tb}",
    )(x1, mod, *wts)


def _pair_blocks(w):
    n, k, _ = w.shape
    wp = w.reshape(n // 2, 2, k, k)
    eye = jnp.eye(2, dtype=w.dtype)
    return jnp.einsum('phij,hg->phigj', wp, eye).reshape(n // 2, 2 * k, 2 * k)


def kernel(x_prompt, x_sample, c_prompt, c_sample, state_pool, state_conv, state_lru, w_ada, b_ada, w_in, w_pool, pool_scale, w_conv, b_conv, w_a, b_a, w_x, b_x, lru_lambda, w_out, ln1_g, ln1_b, w_group, b_group, w_route, b_route, w_gate, w_up, w_down, ln2_g, ln2_b):
    l = 0
    bp, tp, _ = x_prompt.shape
    bs, ts, _ = x_sample.shape
    mod, w_in_b, w_out_b = _ada_mod(
        jnp.concatenate([c_sample, c_prompt], axis=0), w_ada[l], b_ada[l], w_in[l], w_out[l])

    w_rt = jnp.zeros((ROUTE_ROWS, D), f32).at[0:N_GROUPS].set(w_group[l].T)
    b_rt = jnp.zeros((ROUTE_ROWS,), f32).at[0:N_GROUPS].set(b_group[l])
    for k in range(N_GROUPS):
        r0 = SUBLANES * (k + 1)
        w_rt = w_rt.at[r0:r0 + EPG].set(w_route[l][:, k * EPG:(k + 1) * EPG].T)
        b_rt = b_rt.at[r0:r0 + EPG].set(b_route[l][k * EPG:(k + 1) * EPG])
    tri = jnp.triu(jnp.ones((LANES, LANES), bf16), 1)

    mix_wts = (
        w_in_b,
        w_pool[l].astype(bf16),
        pool_scale[l].reshape(1, D_POOL),
        w_conv[l],
        b_conv[l].reshape(1, D_LRU),
        jnp.stack([_pair_blocks(w_a[l]), _pair_blocks(w_x[l])]).astype(bf16),
        jnp.concatenate([b_a[l], b_x[l]]).reshape(1, 2 * D_LRU),
        lru_lambda[l].reshape(1, D_LRU),
        w_out_b,
        ln1_g[l].reshape(1, D),
        ln1_b[l].reshape(1, D),
    )
    tb_p = MOE_ROWS // bp
    zp = jnp.zeros((POOL_BUF, bp, D_POOL), f32)
    zc = jnp.zeros((CONV_W - 1, bp, D_LRU), f32)
    zh = jnp.zeros((bp, D_LRU), f32)
    x1p, pool_p, conv_p, lru_p, wg_b, wu_b, wd_b = _run_mix(
        x_prompt, mod, bs // bp, zp, zc, zh, 0, bp, tb_p, mix_wts, (w_gate[l], w_up[l], w_down[l]))
    moe_wts = (
        w_rt.astype(bf16),
        b_rt.reshape(ROUTE_ROWS, 1),
        tri,
        wg_b,
        wu_b,
        wd_b,
        ln2_g[l].reshape(1, D),
        ln2_b[l].reshape(1, D),
    )
    yp = _run_moe(x1p, mod, bs // bp, 1, bp, tb_p, bp, tp, moe_wts)

    n_prev_s = min(PAST_LEN, POOL_BUF)
    nb_s = bs // 2
    x1s, pool_s, conv_s, lru_s = _run_mix(
        x_sample, mod, 0, state_pool[l].transpose(1, 0, 2), state_conv[l].transpose(1, 0, 2),
        state_lru[l], n_prev_s, nb_s, ts, mix_wts)
    ys = _run_moe(x1s, mod, 0, 2, nb_s, ts, bs, ts, moe_wts)

    tr = lambda a: a.transpose(1, 0, 2)[None]
    return (yp, ys, tr(pool_p), tr(conv_p), lru_p[None], tr(pool_s), tr(conv_s), lru_s[None])
```

```python
import functools
import math

import jax
import jax.numpy as jnp
from jax import lax
from jax.experimental import pallas as pl
from jax.experimental.pallas import tpu as pltpu

D = 1024
D_POOL = 512
D_LRU = 512
D_IN = D_POOL + 2 * D_LRU
POOL_WINDOWS = (2, 4, 8, 16)
POOL_GROUP = 128
POOL_BUF = 15
CONV_W = 4
LRU_C = 8.0
N_GROUPS = 4
EPG = 4
D_EXPERT = 256
DEPTH = 1
ALPHA = (2.0 * DEPTH) ** 0.25
LN_EPS = 1e-5
PAST_LEN = 16384

LANES = 128
SUBLANES = 8
MXU_DIM = 256
N_SLABS = D // LANES
ROUTE_ROWS = 48
MOE_ROWS = 1024
CHUNK = 288
CHUNK_PAD = 320
assert CHUNK <= CHUNK_PAD and (N_GROUPS * CHUNK_PAD) % MXU_DIM == 0
ROW_LO = 4
ROW_LO2 = 8
ROW_GSEL = 12
ROW_RANK = 13
REC_ROWS = 16
N_RB = 4
GPS = 2
FIN_ROWS = 256
ADA_STEPS = 4
V7X_VMEM_BYTES = 64 * 1024 * 1024
VMEM_RESERVE_BYTES = 4 * 1024 * 1024
VMEM_LIMIT = V7X_VMEM_BYTES - VMEM_RESERVE_BYTES
LOG2E = 1.0 / math.log(2.0)
SQRT_GUARD = 1e-30

f32 = jnp.float32
bf16 = jnp.bfloat16
i32 = jnp.int32


def _dot(a, b):
    return jnp.dot(a, b, preferred_element_type=f32)


def _sigmoid(x):
    return 1.0 / (1.0 + jnp.exp2(x * (-LOG2E)))


def _gelu_tanh(x):
    k1 = -2.0 * math.sqrt(2.0 / math.pi) * LOG2E
    return x / (1.0 + jnp.exp2(x * (k1 + (k1 * 0.044715) * (x * x))))


def _layer_norm(v, g, b):
    mu = jnp.mean(v, axis=-1, keepdims=True)
    c = v - mu
    var = jnp.mean(c * c, axis=-1, keepdims=True)
    return c * lax.rsqrt(var + LN_EPS) * g + b


def _slab_pitch(tb):
    return tb + SUBLANES if (tb // SUBLANES) % 2 == 0 else tb


def _ada_body(c_ref, w_ref, b_ref, w_in_ref, w_out_ref, o_ref, w_in_b_ref, w_out_b_ref):
    o_ref[...] = _dot(c_ref[...].astype(bf16), w_ref[...].astype(bf16)) + b_ref[...]
    w_in_b_ref[...] = w_in_ref[...].astype(bf16)
    w_out_b_ref[...] = w_out_ref[...].astype(bf16)


def _ada_mod(c_all, w_ada, b_ada, w_in, w_out):
    n = c_all.shape[0]
    steps = ADA_STEPS
    bn = 6 * D // steps
    br = D // steps
    return pl.pallas_call(
        _ada_body,
        grid=(steps,),
        in_specs=[
            pl.BlockSpec((n, D), lambda j: (0, 0)),
            pl.BlockSpec((D, bn), lambda j: (0, j)),
            pl.BlockSpec((1, bn), lambda j: (0, j)),
            pl.BlockSpec((br, D_IN), lambda j: (j, 0)),
            pl.BlockSpec((br, D), lambda j: (j, 0)),
        ],
        out_specs=[
            pl.BlockSpec((n, bn), lambda j: (0, j)),
            pl.BlockSpec((br, D_IN), lambda j: (j, 0)),
            pl.BlockSpec((br, D), lambda j: (j, 0)),
        ],
        out_shape=[
            jax.ShapeDtypeStruct((n, 6 * D), f32),
            jax.ShapeDtypeStruct((D, D_IN), bf16),
            jax.ShapeDtypeStruct((D, D), bf16),
        ],
        name="ada_mod",
    )(c_all, w_ada, b_ada.reshape(1, 6 * D), w_in, w_out)


def _mix_body(nb, tb, pitch, n_prev, n_tt, n_cast,
              x_ref, mod_ref, pool0_ref, conv0_ref, h0_ref,
              w_in_ref, w_pool_ref, pscale_ref, w_conv_ref, b_conv_ref,
              w_ax_ref, b_ax_ref, lam_ref, w_out_ref, ln1g_ref, ln1b_ref,
              *rest):
    cast_in = rest[:n_cast]
    x1_ref, npool_ref, nconv_ref, nh_ref = rest[n_cast:n_cast + 4]
    cast_out = rest[n_cast + 4:2 * n_cast + 4]
    slab, xt, ycat, zpool, zconv, a_s, b_s, gl_s, h_s = rest[2 * n_cast + 4:]
    for src, dst in zip(cast_in, cast_out):
        dst[...] = src[...].astype(bf16)
    tm = nb * tb
    tq = tb // N_RB
    rq = tq * nb
    ti = pl.program_id(0) % n_tt

    def mod_part(k):
        return mod_ref[:, pl.ds(k * D, D)]

    @pl.when(ti == 0)
    def _init_state():
        zpool[pl.ds(0, POOL_BUF * nb), :] = pool0_ref[...].reshape(POOL_BUF * nb, D_POOL)
        zconv[pl.ds(0, (CONV_W - 1) * nb), :] = conv0_ref[...].reshape((CONV_W - 1) * nb, D_LRU)
        h_s[...] = h0_ref[...]

    for b in range(nb):
        for j in range(N_SLABS):
            slab[j, pl.ds(b * pitch, tb), :] = x_ref[b, :, pl.ds(j * LANES, LANES)]

    sh1 = mod_part(0)
    sc1 = mod_part(1)
    g1 = mod_part(2)
    nl = -lam_ref[...]
    softplus = jnp.maximum(nl, 0.0) + jnp.log(1.0 + jnp.exp(-jnp.abs(nl)))
    log2a_unit = (-LRU_C * LOG2E) * softplus
    row = lax.broadcasted_iota(i32, (tm, LANES), 0)
    t_loc = lax.shift_right_logical(row, int(math.log2(nb)))
    h = h_s[...]

    for t in range(tb):
        for j in range(N_SLABS):
            xt[t, :, pl.ds(j * LANES, LANES)] = slab[j, pl.ds(t, nb, stride=pitch), :]

    for r in range(N_RB):
        u = xt[pl.ds(r * tq, tq)] * (1.0 + sc1)[None] + sh1[None]
        ycat[pl.ds(r * rq, rq), :] = u.reshape(rq, D).astype(bf16)
    ub = ycat[...]
    zpool[pl.ds(POOL_BUF * nb, tm), :] = _dot(ub, w_in_ref[:, pl.ds(0, D_POOL)])
    zconv[pl.ds((CONV_W - 1) * nb, tm), :] = _dot(ub, w_in_ref[:, pl.ds(D_POOL, D_LRU)])
    half = D_LRU // 2
    for hpart in range(2):
        cols = pl.ds(D_POOL + D_LRU + hpart * half, half)
        gl_s[:, pl.ds(hpart * half, half)] = _gelu_tanh(_dot(ub, w_in_ref[:, cols]))

    t_glob = (ti * tb + t_loc + (1 + n_prev)).astype(f32)
    for c, w in enumerate(POOL_WINDOWS):
        lanes = pl.ds(c * LANES, LANES)
        s = zpool[pl.ds((POOL_BUF + 1 - w) * nb, tm + (w - 1) * nb), lanes]
        step = 1
        while step < w:
            s = s[step * nb:] + s[:-step * nb]
            step *= 2
        cnt = jnp.minimum(f32(w), t_glob)
        dlt = s / cnt - zpool[pl.ds(POOL_BUF * nb, tm), lanes]
        yp = _dot(dlt.astype(bf16), w_pool_ref[c]) * pscale_ref[:, lanes]
        ycat[:, lanes] = yp.astype(bf16)

    for j in range(D_LRU // LANES):
        lanes = pl.ds(j * LANES, LANES)
        xc = b_conv_ref[:, lanes] + zconv[pl.ds(0, tm), lanes] * w_conv_ref[0:1, lanes]
        for k in range(1, CONV_W):
            xc = xc + zconv[pl.ds(k * nb, tm), lanes] * w_conv_ref[k:k + 1, lanes]
        xcb = xc.astype(bf16)
        r = _sigmoid(_dot(xcb, w_ax_ref[0, j]) + b_ax_ref[:, lanes])
        ig = _sigmoid(_dot(xcb, w_ax_ref[1, j]) + b_ax_ref[:, pl.ds(D_LRU + j * LANES, LANES)])
        a = jnp.exp2(r * log2a_unit[:, j * LANES:(j + 1) * LANES])
        a_s[:, lanes] = a
        om = jnp.maximum(1.0 - a * a, 0.0)
        mult = om * lax.rsqrt(jnp.maximum(om, SQRT_GUARD))
        b_s[:, lanes] = mult * ig * xc

    for t in range(tb):
        rows = pl.ds(t * nb, nb)
        h = a_s[rows, :] * h + b_s[rows, :]
        b_s[rows, :] = h
    ycat[:, pl.ds(D_POOL, D_LRU)] = (b_s[...] * gl_s[...]).astype(bf16)

    yc = ycat[...]
    cb = D // N_RB
    for j in range(N_RB):
        cols = pl.ds(j * cb, cb)
        mix = _dot(yc, w_out_ref[:, cols]).reshape(tb, nb, cb)
        xt[:, :, cols] = ALPHA * xt[:, :, cols] + (1.0 + g1[:, j * cb:(j + 1) * cb])[None] * mix
    for r in range(N_RB):
        xn = _layer_norm(xt[pl.ds(r * tq, tq)], ln1g_ref[...][None], ln1b_ref[...][None])
        x1_ref[pl.ds(r * rq, rq), :] = xn.reshape(rq, D)

    h_s[...] = h
    zpool[pl.ds(0, POOL_BUF * nb), :] = zpool[pl.ds(tm, POOL_BUF * nb), :]
    zconv[pl.ds(0, (CONV_W - 1) * nb), :] = zconv[pl.ds(tm, (CONV_W - 1) * nb), :]

    @pl.when(ti == n_tt - 1)
    def _emit_state():
        npool_ref[...] = zpool[pl.ds(0, POOL_BUF * nb), :].reshape(POOL_BUF, nb, D_POOL)
        nconv_ref[...] = zconv[pl.ds(0, (CONV_W - 1) * nb), :].reshape(CONV_W - 1, nb, D_LRU)
        nh_ref[...] = h_s[...]


def _run_mix(x, mod, mod_blk0, pool0, conv0, h0, n_prev, nb, tb, wts, to_cast=()):
    bsz, t_len, _ = x.shape
    n_sb = bsz // nb
    n_tt = t_len // tb
    tm = nb * tb
    pitch = _slab_pitch(tb)
    st = lambda q: (q // n_tt, q % n_tt)
    const2 = lambda q: (0, 0)
    const3 = lambda q: (0, 0, 0)
    const4 = lambda q: (0, 0, 0, 0)
    single = pl.Buffered(1)
    in_specs = [
        pl.BlockSpec((nb, tb, D), lambda q: (*st(q), 0)),
        pl.BlockSpec((nb, 3 * D), lambda q: (mod_blk0 + st(q)[0], 0)),
        pl.BlockSpec((POOL_BUF, nb, D_POOL), lambda q: (0, st(q)[0], 0)),
        pl.BlockSpec((CONV_W - 1, nb, D_LRU), lambda q: (0, st(q)[0], 0)),
        pl.BlockSpec((nb, D_LRU), lambda q: (st(q)[0], 0)),
        pl.BlockSpec((D, D_IN), const2, pipeline_mode=single),
        pl.BlockSpec((4, POOL_GROUP, POOL_GROUP), const3),
        pl.BlockSpec((1, D_POOL), const2),
        pl.BlockSpec((CONV_W, D_LRU), const2),
        pl.BlockSpec((1, D_LRU), const2),
        pl.BlockSpec((2, D_LRU // LANES, LANES, LANES), const4),
        pl.BlockSpec((1, 2 * D_LRU), const2),
        pl.BlockSpec((1, D_LRU), const2),
        pl.BlockSpec((D, D), const2, pipeline_mode=single),
        pl.BlockSpec((1, D), const2),
        pl.BlockSpec((1, D), const2),
    ]
    out_specs = [
        pl.BlockSpec((tm, D), lambda q: (q, 0)),
        pl.BlockSpec((POOL_BUF, nb, D_POOL), lambda q: (0, st(q)[0], 0)),
        pl.BlockSpec((CONV_W - 1, nb, D_LRU), lambda q: (0, st(q)[0], 0)),
        pl.BlockSpec((nb, D_LRU), lambda q: (st(q)[0], 0)),
    ]
    out_shape = [
        jax.ShapeDtypeStruct((bsz * t_len, D), f32),
        jax.ShapeDtypeStruct((POOL_BUF, bsz, D_POOL), f32),
        jax.ShapeDtypeStruct((CONV_W - 1, bsz, D_LRU), f32),
        jax.ShapeDtypeStruct((bsz, D_LRU), f32),
    ]
    for w in to_cast:
        assert w.shape[0] == n_sb * n_tt, (w.shape, n_sb * n_tt)
        blk = (1,) + w.shape[1:]
        in_specs.append(pl.BlockSpec(blk, lambda q: (q, 0, 0)))
        out_specs.append(pl.BlockSpec(blk, lambda q: (q, 0, 0)))
        out_shape.append(jax.ShapeDtypeStruct(w.shape, bf16))
    scratch = [
        pltpu.VMEM((N_SLABS, nb * pitch, LANES), f32),
        pltpu.VMEM((tb, nb, D), f32),
        pltpu.VMEM((tm, D), bf16),
        pltpu.VMEM(((tb + POOL_BUF) * nb, D_POOL), f32),
        pltpu.VMEM(((tb + CONV_W - 1) * nb, D_LRU), f32),
        pltpu.VMEM((tm, D_LRU), f32),
        pltpu.VMEM((tm, D_LRU), f32),
        pltpu.VMEM((tm, D_LRU), f32),
        pltpu.VMEM((nb, D_LRU), f32),
    ]
    body = functools.partial(_mix_body, nb, tb, pitch, n_prev, n_tt, len(to_cast))
    return pl.pallas_call(
        body,
        grid=(n_sb * n_tt,),
        in_specs=in_specs,
        out_specs=out_specs,
        out_shape=out_shape,
        scratch_shapes=scratch,
        compiler_params=pltpu.CompilerParams(
            dimension_semantics=("arbitrary",),
            vmem_limit_bytes=VMEM_LIMIT),
        name=f"mix_nb{nb}_tb{tb}",
    )(x, mod, pool0, conv0, h0, *wts, *to_cast)


def _moe_body(n_half, nb, tb, pitch,
              x1_ref, mod_ref, w_rt_ref, b_rt_ref, tri_ref, wg_ref, wu_ref, wd_ref, ln2g_ref, ln2b_ref,
              y_ref,
              u2, tok_t, tok3, rows_s, cnt_s, ys_all, yacc, slab):
    tm = MOE_ROWS
    hm = nb * tb
    tq = tb // N_RB
    rq = tq * nb
    gs = pl.program_id(1)
    last_gs = N_GROUPS // GPS - 1

    def mod_part(h, k):
        return mod_ref[pl.ds(h * nb, nb), pl.ds((k - 3) * D, D)]

    @pl.when(gs == 0)
    def _route():
        for h in range(n_half):
            sh2 = mod_part(h, 3)
            sc2 = mod_part(h, 4)
            for r in range(N_RB):
                rows = pl.ds(h * hm + r * rq, rq)
                v = x1_ref[rows, :].reshape(tq, nb, D) * (1.0 + sc2)[None] + sh2[None]
                u2[rows, :] = v.reshape(rq, D).astype(bf16)
        lt = lax.dot_general(w_rt_ref[...], u2[...], (((1,), (1,)), ((), ())),
                             preferred_element_type=f32) + b_rt_ref[...]
        gl = [lt[k:k + 1, :] for k in range(N_GROUPS)]
        best = gl[0]
        gsel = jnp.zeros_like(best, dtype=i32)
        for k in range(1, N_GROUPS):
            better = gl[k] > best
            best = jnp.where(better, gl[k], best)
            gsel = jnp.where(better, k, gsel)
        denom = jnp.exp(gl[0] - best)
        for k in range(1, N_GROUPS):
            denom = denom + jnp.exp(gl[k] - best)
        p_sel = 1.0 / denom
        es = []
        for j in range(EPG):
            v_j = lt[SUBLANES + j:SUBLANES + j + 1, :]
            for k in range(1, N_GROUPS):
                r0 = SUBLANES * (k + 1) + j
                v_j = jnp.where(gsel == k, lt[r0:r0 + 1, :], v_j)
            es.append(v_j)
        v1 = es[0]
        i1 = jnp.zeros_like(gsel)
        for j in range(1, EPG):
            better = es[j] > v1
            v1 = jnp.where(better, es[j], v1)
            i1 = jnp.where(better, j, i1)
        v2 = jnp.full_like(v1, -jnp.inf)
        i2 = jnp.full_like(gsel, -1)
        for j in range(EPG):
            cand = jnp.logical_and(i1 != j, jnp.logical_or(i2 < 0, es[j] > v2))
            v2 = jnp.where(cand, es[j], v2)
            i2 = jnp.where(cand, j, i2)
        e21 = jnp.exp(v2 - v1)
        w1 = p_sel / (1.0 + e21)
        w2 = p_sel * e21 / (1.0 + e21)
        rid8 = lax.broadcasted_iota(i32, (SUBLANES, tm), 0)
        onehot = jnp.where(rid8 == gsel, 1.0, 0.0)
        blocks = [onehot[:, j * LANES:(j + 1) * LANES] for j in range(tm // LANES)]
        inner = _dot(jnp.concatenate(blocks, axis=0).astype(bf16), tri_ref[...])
        before = jnp.zeros((SUBLANES, 1), f32)
        prefs = []
        for j, blk in enumerate(blocks):
            prefs.append(inner[j * SUBLANES:(j + 1) * SUBLANES, :] + before)
            before = before + jnp.sum(blk, axis=1, keepdims=True)
        pref = jnp.concatenate(prefs, axis=1)
        rank = jnp.sum(onehot * pref, axis=0, keepdims=True)
        n_max = 0
        for k in range(N_GROUPS):
            n_k = jnp.sum(jnp.where(gsel == k, 1.0, 0.0)).astype(i32)
            cnt_s[k] = n_k
            n_max = jnp.maximum(n_max, n_k)
        cnt_s[N_GROUPS] = (n_max > CHUNK).astype(i32)
        rows_s[0:1, :] = gsel
        rows_s[1:2, :] = rank.astype(i32)
        rid = lax.broadcasted_iota(i32, (REC_ROWS, tm), 0)
        rec = jnp.zeros((REC_ROWS, tm), f32)
        for j in range(EPG):
            wj = jnp.where(i1 == j, w1, jnp.where(i2 == j, w2, 0.0))
            hi = wj.astype(bf16).astype(f32)
            mid = (wj - hi).astype(bf16).astype(f32)
            lo = (wj - hi - mid).astype(bf16).astype(f32)
            rec = jnp.where(rid == j, hi, rec)
            rec = jnp.where(rid == ROW_LO + j, mid, rec)
            rec = jnp.where(rid == ROW_LO2 + j, lo, rec)
        rec = jnp.where(rid == ROW_GSEL, gsel.astype(f32), rec)
        rec = jnp.where(rid == ROW_RANK, rank, rec)
        rec = jnp.concatenate([rec, jnp.zeros((LANES - REC_ROWS, tm), f32)], axis=0)
        rt = rec.T
        tok_t[...] = rt
        tok3[...] = rt.astype(bf16)

    def sorted_experts(gi, base):
        g = gs * GPS + gi
        gsel_row = rows_s[0:1, :]
        rank_row = rows_s[1:2, :]
        jj = lax.broadcasted_iota(i32, (CHUNK, tm), 0) + base
        pm = jnp.where(jnp.logical_and(gsel_row == g, rank_row == jj), 1.0, 0.0).astype(bf16)
        xs = _dot(pm, u2[...]).astype(bf16)
        cs3 = _dot(pm, tok3[...])
        cs = (cs3 + pltpu.roll(cs3, LANES - ROW_LO, axis=1)
              + pltpu.roll(cs3, LANES - ROW_LO2, axis=1))
        acts = []
        for e in range(EPG):
            hg = _dot(xs, wg_ref[gi * EPG + e])
            hu = _dot(xs, wu_ref[gi * EPG + e])
            acts.append((hg * _sigmoid(hg) * hu * cs[:, e:e + 1]).astype(bf16))
        hb = jnp.concatenate(acts, axis=1)
        wd_g = wd_ref[pl.ds(gi * EPG, EPG)].reshape(EPG * D_EXPERT, D)
        return _dot(hb, wd_g).astype(bf16)

    for gi in range(GPS):
        first_row = pl.multiple_of((gs * GPS + gi) * CHUNK_PAD, CHUNK_PAD)
        ys_all[pl.ds(first_row, CHUNK), :] = sorted_experts(gi, 0)

    @pl.when(gs == 0)
    def _zero_pad_rows():
        for k in range(N_GROUPS):
            ys_all[pl.ds(k * CHUNK_PAD + CHUNK, CHUNK_PAD - CHUNK), :] = jnp.zeros((CHUNK_PAD - CHUNK, D), bf16)

    has_overflow = cnt_s[N_GROUPS] > 0

    @pl.when(jnp.logical_and(gs == 0, has_overflow))
    def _zero_acc():
        yacc[...] = jnp.zeros((tm, D), f32)

    for gi in range(GPS):
        g = gs * GPS + gi

        def overflow(c, carry, gi=gi, g=g):
            base = c * CHUNK
            ys = sorted_experts(gi, base)
            rec = tok_t[...]
            gsel_col = rec[:, ROW_GSEL:ROW_GSEL + 1]
            rank_col = rec[:, ROW_RANK:ROW_RANK + 1]
            jl = (lax.broadcasted_iota(i32, (tm, CHUNK), 1) + base).astype(f32)
            pt = jnp.where(jnp.logical_and(gsel_col == g.astype(f32), rank_col == jl), 1.0, 0.0).astype(bf16)
            yacc[...] += _dot(pt, ys)
            return carry

        lax.fori_loop(1, pl.cdiv(cnt_s[g], CHUNK), overflow, 0)

    def finish(with_acc):
        tqf = FIN_ROWS // nb
        jl = lax.broadcasted_iota(i32, (FIN_ROWS, N_GROUPS * CHUNK_PAD), 1).astype(f32)
        for h in range(n_half):
            g2 = mod_part(h, 5)
            for r in range(hm // FIN_ROWS):
                rows = pl.ds(h * hm + r * FIN_ROWS, FIN_ROWS)
                rec = tok_t[rows, :]
                gsel_col = rec[:, ROW_GSEL:ROW_GSEL + 1]
                rank_col = rec[:, ROW_RANK:ROW_RANK + 1]
                key = jnp.where(rank_col < f32(CHUNK), gsel_col * f32(CHUNK_PAD) + rank_col, -1.0)
                pt = jnp.where(key == jl, 1.0, 0.0).astype(bf16)
                y = _dot(pt, ys_all[...])
                if with_acc:
                    y = y + yacc[rows, :]
                v = ALPHA * x1_ref[rows, :] + ((1.0 + g2)[None] * y.reshape(tqf, nb, D)).reshape(FIN_ROWS, D)
                yn = _layer_norm(v, ln2g_ref[...], ln2b_ref[...])
                for tl in range(tqf):
                    for j in range(N_SLABS):
                        slab[j, pl.ds(r * tqf + tl, nb, stride=pitch), :] = (
                            yn[tl * nb:(tl + 1) * nb, j * LANES:(j + 1) * LANES])
            for b in range(nb):
                for j in range(N_SLABS):
                    y_ref[h * nb + b, :, pl.ds(j * LANES, LANES)] = slab[j, pl.ds(b * pitch, tb), :]

    @pl.when(jnp.logical_and(gs == last_gs, jnp.logical_not(has_overflow)))
    def _finish():
        finish(False)

    @pl.when(jnp.logical_and(gs == last_gs, has_overflow))
    def _finish_with_overflow():
        finish(True)


def _run_moe(x1, mod, mod_blk0, n_half, nb, tb, out_seqs, out_len, wts):
    n_tiles = x1.shape[0] // MOE_ROWS
    assert n_half * nb * tb == MOE_ROWS
    seq_per_tile = n_half * nb
    n_tt = out_len // tb
    pitch = _slab_pitch(tb)
    st = lambda q: (q // n_tt, q % n_tt)
    const2 = lambda q, g: (0, 0)
    grp3 = lambda q, g: (g, 0, 0)
    single = pl.Buffered(1)
    in_specs = [
        pl.BlockSpec((MOE_ROWS, D), lambda q, g: (q, 0)),
        pl.BlockSpec((seq_per_tile, 3 * D), lambda q, g: (mod_blk0 + st(q)[0], 1)),
        pl.BlockSpec((ROUTE_ROWS, D), const2),
        pl.BlockSpec((ROUTE_ROWS, 1), const2),
        pl.BlockSpec((LANES, LANES), const2),
        pl.BlockSpec((GPS * EPG, D, D_EXPERT), grp3),
        pl.BlockSpec((GPS * EPG, D, D_EXPERT), grp3),
        pl.BlockSpec((GPS * EPG, D_EXPERT, D), grp3),
        pl.BlockSpec((1, D), const2),
        pl.BlockSpec((1, D), const2),
    ]
    scratch = [
        pltpu.VMEM((MOE_ROWS, D), bf16),
        pltpu.VMEM((MOE_ROWS, LANES), f32),
        pltpu.VMEM((MOE_ROWS, LANES), bf16),
        pltpu.VMEM((SUBLANES, MOE_ROWS), i32),
        pltpu.SMEM((N_GROUPS + 1,), i32),
        pltpu.VMEM((N_GROUPS * CHUNK_PAD, D), bf16),
        pltpu.VMEM((MOE_ROWS, D), f32),
        pltpu.VMEM((N_SLABS, nb * pitch, LANES), f32),
    ]
    body = functools.partial(_moe_body, n_half, nb, tb, pitch)
    return pl.pallas_call(
        body,
        grid=(n_tiles, N_GROUPS // GPS),
        in_specs=in_specs,
        out_specs=pl.BlockSpec((seq_per_tile, tb, D), lambda q, g: (*st(q), 0)),
        out_shape=jax.ShapeDtypeStruct((out_seqs, out_len, D), f32),
        scratch_shapes=scratch,
        compiler_params=pltpu.CompilerParams(
            dimension_semantics=("arbitrary", "arbitrary"),
            vmem_limit_bytes=VMEM_LIMIT),
        name=f"moe_nb{nb}_tb{tb}",
    )(x1, mod, *wts)


def _pair_blocks(w):
    n, k, _ = w.shape
    wp = w.reshape(n // 2, 2, k, k)
    eye = jnp.eye(2, dtype=w.dtype)
    return jnp.einsum('phij,hg->phigj', wp, eye).reshape(n // 2, 2 * k, 2 * k)


def kernel(x_prompt, x_sample, c_prompt, c_sample, state_pool, state_conv, state_lru, w_ada, b_ada, w_in, w_pool, pool_scale, w_conv, b_conv, w_a, b_a, w_x, b_x, lru_lambda, w_out, ln1_g, ln1_b, w_group, b_group, w_route, b_route, w_gate, w_up, w_down, ln2_g, ln2_b):
    l = 0
    bp, tp, _ = x_prompt.shape
    bs, ts, _ = x_sample.shape
    mod, w_in_b, w_out_b = _ada_mod(
        jnp.concatenate([c_sample, c_prompt], axis=0), w_ada[l], b_ada[l], w_in[l], w_out[l])

    w_rt = jnp.zeros((ROUTE_ROWS, D), f32).at[0:N_GROUPS].set(w_group[l].T)
    b_rt = jnp.zeros((ROUTE_ROWS,), f32).at[0:N_GROUPS].set(b_group[l])
    for k in range(N_GROUPS):
        r0 = SUBLANES * (k + 1)
        w_rt = w_rt.at[r0:r0 + EPG].set(w_route[l][:, k * EPG:(k + 1) * EPG].T)
        b_rt = b_rt.at[r0:r0 + EPG].set(b_route[l][k * EPG:(k + 1) * EPG])
    tri = jnp.triu(jnp.ones((LANES, LANES), bf16), 1)

    mix_wts = (
        w_in_b,
        w_pool[l].astype(bf16),
        pool_scale[l].reshape(1, D_POOL),
        w_conv[l],
        b_conv[l].reshape(1, D_LRU),
        jnp.stack([_pair_blocks(w_a[l]), _pair_blocks(w_x[l])]).astype(bf16),
        jnp.concatenate([b_a[l], b_x[l]]).reshape(1, 2 * D_LRU),
        lru_lambda[l].reshape(1, D_LRU),
        w_out_b,
        ln1_g[l].reshape(1, D),
        ln1_b[l].reshape(1, D),
    )
    tb_p = MOE_ROWS // bp
    zp = jnp.zeros((POOL_BUF, bp, D_POOL), f32)
    zc = jnp.zeros((CONV_W - 1, bp, D_LRU), f32)
    zh = jnp.zeros((bp, D_LRU), f32)
    x1p, pool_p, conv_p, lru_p, wg_b, wu_b, wd_b = _run_mix(
        x_prompt, mod, bs // bp, zp, zc, zh, 0, bp, tb_p, mix_wts, (w_gate[l], w_up[l], w_down[l]))
    moe_wts = (
        w_rt.astype(bf16),
        b_rt.reshape(ROUTE_ROWS, 1),
        tri,
        wg_b,
        wu_b,
        wd_b,
        ln2_g[l].reshape(1, D),
        ln2_b[l].reshape(1, D),
    )
    yp = _run_moe(x1p, mod, bs // bp, 1, bp, tb_p, bp, tp, moe_wts)

    n_prev_s = min(PAST_LEN, POOL_BUF)
    nb_s = bs // 2
    x1s, pool_s, conv_s, lru_s = _run_mix(
        x_sample, mod, 0, state_pool[l].transpose(1, 0, 2), state_conv[l].transpose(1, 0, 2),
        state_lru[l], n_prev_s, nb_s, ts, mix_wts)
    ys = _run_moe(x1s, mod, 0, 2, nb_s, ts, bs, ts, moe_wts)

    tr = lambda a: a.transpose(1, 0, 2)[None]
    return (yp, ys, tr(pool_p), tr(conv_p), lru_p[None], tr(pool_s), tr(conv_s), lru_s[None])
```

```python
import functools
import math

import jax
import jax.numpy as jnp
from jax import lax
from jax.experimental import pallas as pl
from jax.experimental.pallas import tpu as pltpu

D = 1024
D_POOL = 512
D_LRU = 512
D_IN = D_POOL + 2 * D_LRU
POOL_WINDOWS = (2, 4, 8, 16)
POOL_GROUP = 128
POOL_BUF = 15
CONV_W = 4
LRU_C = 8.0
N_GROUPS = 4
EPG = 4
D_EXPERT = 256
DEPTH = 1
ALPHA = (2.0 * DEPTH) ** 0.25
LN_EPS = 1e-5
PAST_LEN = 16384

LANES = 128
SUBLANES = 8
MXU_DIM = 256
N_SLABS = D // LANES
ROUTE_ROWS = 48
MOE_ROWS = 1024
CHUNK = 288
CHUNK_PAD = 320
assert CHUNK <= CHUNK_PAD and (N_GROUPS * CHUNK_PAD) % MXU_DIM == 0
ROW_LO = 4
ROW_LO2 = 8
ROW_GSEL = 12
ROW_RANK = 13
REC_ROWS = 16
SUB = 256
WIN = 128
WIN_START = (0, 32, 96, 128)
N_RB = 4
GPS = 2
FIN_ROWS = 256
ADA_STEPS = 8
V7X_VMEM_BYTES = 64 * 1024 * 1024
VMEM_RESERVE_BYTES = 4 * 1024 * 1024
VMEM_LIMIT = V7X_VMEM_BYTES - VMEM_RESERVE_BYTES
LOG2E = 1.0 / math.log(2.0)
SQRT_GUARD = 1e-30

f32 = jnp.float32
bf16 = jnp.bfloat16
i32 = jnp.int32


def _dot(a, b):
    return jnp.dot(a, b, preferred_element_type=f32)


def _sigmoid(x):
    return 1.0 / (1.0 + jnp.exp2(x * (-LOG2E)))


def _gelu_tanh(x):
    k1 = -2.0 * math.sqrt(2.0 / math.pi) * LOG2E
    return x / (1.0 + jnp.exp2(x * (k1 + (k1 * 0.044715) * (x * x))))


def _layer_norm(v, g, b):
    mu = jnp.mean(v, axis=-1, keepdims=True)
    c = v - mu
    var = jnp.mean(c * c, axis=-1, keepdims=True)
    return c * lax.rsqrt(var + LN_EPS) * g + b


def _slab_pitch(tb):
    return tb + SUBLANES if (tb // SUBLANES) % 2 == 0 else tb


def _ada_body(c_ref, w_ref, b_ref, w_in_ref, w_out_ref, o_ref, w_in_b_ref, w_out_b_ref):
    o_ref[...] = _dot(c_ref[...].astype(bf16), w_ref[...].astype(bf16)) + b_ref[...]
    w_in_b_ref[...] = w_in_ref[...].astype(bf16)
    w_out_b_ref[...] = w_out_ref[...].astype(bf16)


def _ada_mod(c_all, w_ada, b_ada, w_in, w_out):
    n = c_all.shape[0]
    steps = ADA_STEPS
    bn = 6 * D // steps
    br = D // steps
    return pl.pallas_call(
        _ada_body,
        grid=(steps,),
        in_specs=[
            pl.BlockSpec((n, D), lambda j: (0, 0)),
            pl.BlockSpec((D, bn), lambda j: (0, j)),
            pl.BlockSpec((1, bn), lambda j: (0, j)),
            pl.BlockSpec((br, D_IN), lambda j: (j, 0)),
            pl.BlockSpec((br, D), lambda j: (j, 0)),
        ],
        out_specs=[
            pl.BlockSpec((n, bn), lambda j: (0, j)),
            pl.BlockSpec((br, D_IN), lambda j: (j, 0)),
            pl.BlockSpec((br, D), lambda j: (j, 0)),
        ],
        out_shape=[
            jax.ShapeDtypeStruct((n, 6 * D), f32),
            jax.ShapeDtypeStruct((D, D_IN), bf16),
            jax.ShapeDtypeStruct((D, D), bf16),
        ],
        name="ada_mod",
    )(c_all, w_ada, b_ada.reshape(1, 6 * D), w_in, w_out)


def _mix_body(nb, tb, pitch, n_prev, n_tt, n_cast,
              x_ref, mod_ref, pool0_ref, conv0_ref, h0_ref,
              w_in_ref, w_pool_ref, pscale_ref, w_conv_ref, b_conv_ref,
              w_ax_ref, b_ax_ref, lam_ref, w_out_ref, ln1g_ref, ln1b_ref,
              *rest):
    cast_in = rest[:n_cast]
    x1_ref, npool_ref, nconv_ref, nh_ref = rest[n_cast:n_cast + 4]
    cast_out = rest[n_cast + 4:2 * n_cast + 4]
    slab, xt, ycat, zpool, zconv, a_s, b_s, gl_s, h_s = rest[2 * n_cast + 4:]
    for src, dst in zip(cast_in, cast_out):
        dst[...] = src[...].astype(bf16)
    tm = nb * tb
    tq = tb // N_RB
    rq = tq * nb
    ti = pl.program_id(0) % n_tt

    def mod_part(k):
        return mod_ref[:, pl.ds(k * D, D)]

    @pl.when(ti == 0)
    def _init_state():
        zpool[pl.ds(0, POOL_BUF * nb), :] = pool0_ref[...].reshape(POOL_BUF * nb, D_POOL)
        zconv[pl.ds(0, (CONV_W - 1) * nb), :] = conv0_ref[...].reshape((CONV_W - 1) * nb, D_LRU)
        h_s[...] = h0_ref[...]

    for b in range(nb):
        for j in range(N_SLABS):
            slab[j, pl.ds(b * pitch, tb), :] = x_ref[b, :, pl.ds(j * LANES, LANES)]

    sh1 = mod_part(0)
    sc1 = mod_part(1)
    g1 = mod_part(2)
    nl = -lam_ref[...]
    softplus = jnp.maximum(nl, 0.0) + jnp.log(1.0 + jnp.exp(-jnp.abs(nl)))
    log2a_unit = (-LRU_C * LOG2E) * softplus
    row = lax.broadcasted_iota(i32, (tm, LANES), 0)
    t_loc = lax.shift_right_logical(row, int(math.log2(nb)))
    h = h_s[...]

    for t in range(tb):
        for j in range(N_SLABS):
            xt[t, :, pl.ds(j * LANES, LANES)] = slab[j, pl.ds(t, nb, stride=pitch), :]

    for r in range(N_RB):
        u = xt[pl.ds(r * tq, tq)] * (1.0 + sc1)[None] + sh1[None]
        ycat[pl.ds(r * rq, rq), :] = u.reshape(rq, D).astype(bf16)
    ub = ycat[...]
    zpool[pl.ds(POOL_BUF * nb, tm), :] = _dot(ub, w_in_ref[:, pl.ds(0, D_POOL)])
    zconv[pl.ds((CONV_W - 1) * nb, tm), :] = _dot(ub, w_in_ref[:, pl.ds(D_POOL, D_LRU)])
    half = D_LRU // 2
    for hpart in range(2):
        cols = pl.ds(D_POOL + D_LRU + hpart * half, half)
        gl_s[:, pl.ds(hpart * half, half)] = _gelu_tanh(_dot(ub, w_in_ref[:, cols]))

    t_glob = (ti * tb + t_loc + (1 + n_prev)).astype(f32)
    for c, w in enumerate(POOL_WINDOWS):
        lanes = pl.ds(c * LANES, LANES)
        s = zpool[pl.ds((POOL_BUF + 1 - w) * nb, tm + (w - 1) * nb), lanes]
        step = 1
        while step < w:
            s = s[step * nb:] + s[:-step * nb]
            step *= 2
        cnt = jnp.minimum(f32(w), t_glob)
        dlt = s / cnt - zpool[pl.ds(POOL_BUF * nb, tm), lanes]
        yp = _dot(dlt.astype(bf16), w_pool_ref[c]) * pscale_ref[:, lanes]
        ycat[:, lanes] = yp.astype(bf16)

    for j in range(D_LRU // LANES):
        lanes = pl.ds(j * LANES, LANES)
        xc = b_conv_ref[:, lanes] + zconv[pl.ds(0, tm), lanes] * w_conv_ref[0:1, lanes]
        for k in range(1, CONV_W):
            xc = xc + zconv[pl.ds(k * nb, tm), lanes] * w_conv_ref[k:k + 1, lanes]
        xcb = xc.astype(bf16)
        r = _sigmoid(_dot(xcb, w_ax_ref[0, j]) + b_ax_ref[:, lanes])
        ig = _sigmoid(_dot(xcb, w_ax_ref[1, j]) + b_ax_ref[:, pl.ds(D_LRU + j * LANES, LANES)])
        a = jnp.exp2(r * log2a_unit[:, j * LANES:(j + 1) * LANES])
        a_s[:, lanes] = a
        om = jnp.maximum(1.0 - a * a, 0.0)
        mult = om * lax.rsqrt(jnp.maximum(om, SQRT_GUARD))
        b_s[:, lanes] = mult * ig * xc

    for t in range(tb):
        rows = pl.ds(t * nb, nb)
        h = a_s[rows, :] * h + b_s[rows, :]
        b_s[rows, :] = h
    ycat[:, pl.ds(D_POOL, D_LRU)] = (b_s[...] * gl_s[...]).astype(bf16)

    yc = ycat[...]
    cb = D // N_RB
    for j in range(N_RB):
        cols = pl.ds(j * cb, cb)
        mix = _dot(yc, w_out_ref[:, cols]).reshape(tb, nb, cb)
        xt[:, :, cols] = ALPHA * xt[:, :, cols] + (1.0 + g1[:, j * cb:(j + 1) * cb])[None] * mix
    for r in range(N_RB):
        xn = _layer_norm(xt[pl.ds(r * tq, tq)], ln1g_ref[...][None], ln1b_ref[...][None])
        x1_ref[pl.ds(r * rq, rq), :] = xn.reshape(rq, D)

    h_s[...] = h
    zpool[pl.ds(0, POOL_BUF * nb), :] = zpool[pl.ds(tm, POOL_BUF * nb), :]
    zconv[pl.ds(0, (CONV_W - 1) * nb), :] = zconv[pl.ds(tm, (CONV_W - 1) * nb), :]

    @pl.when(ti == n_tt - 1)
    def _emit_state():
        npool_ref[...] = zpool[pl.ds(0, POOL_BUF * nb), :].reshape(POOL_BUF, nb, D_POOL)
        nconv_ref[...] = zconv[pl.ds(0, (CONV_W - 1) * nb), :].reshape(CONV_W - 1, nb, D_LRU)
        nh_ref[...] = h_s[...]


def _run_mix(x, mod, mod_blk0, pool0, conv0, h0, n_prev, nb, tb, wts, to_cast=()):
    bsz, t_len, _ = x.shape
    n_sb = bsz // nb
    n_tt = t_len // tb
    tm = nb * tb
    pitch = _slab_pitch(tb)
    st = lambda q: (q // n_tt, q % n_tt)
    const2 = lambda q: (0, 0)
    const3 = lambda q: (0, 0, 0)
    const4 = lambda q: (0, 0, 0, 0)
    single = pl.Buffered(1)
    in_specs = [
        pl.BlockSpec((nb, tb, D), lambda q: (*st(q), 0)),
        pl.BlockSpec((nb, 6 * D), lambda q: (mod_blk0 + st(q)[0], 0)),
        pl.BlockSpec((POOL_BUF, nb, D_POOL), lambda q: (0, st(q)[0], 0)),
        pl.BlockSpec((CONV_W - 1, nb, D_LRU), lambda q: (0, st(q)[0], 0)),
        pl.BlockSpec((nb, D_LRU), lambda q: (st(q)[0], 0)),
        pl.BlockSpec((D, D_IN), const2, pipeline_mode=single),
        pl.BlockSpec((4, POOL_GROUP, POOL_GROUP), const3),
        pl.BlockSpec((1, D_POOL), const2),
        pl.BlockSpec((CONV_W, D_LRU), const2),
        pl.BlockSpec((1, D_LRU), const2),
        pl.BlockSpec((2, D_LRU // LANES, LANES, LANES), const4),
        pl.BlockSpec((1, 2 * D_LRU), const2),
        pl.BlockSpec((1, D_LRU), const2),
        pl.BlockSpec((D, D), const2, pipeline_mode=single),
        pl.BlockSpec((1, D), const2),
        pl.BlockSpec((1, D), const2),
    ]
    out_specs = [
        pl.BlockSpec((tm, D), lambda q: (q, 0)),
        pl.BlockSpec((POOL_BUF, nb, D_POOL), lambda q: (0, st(q)[0], 0)),
        pl.BlockSpec((CONV_W - 1, nb, D_LRU), lambda q: (0, st(q)[0], 0)),
        pl.BlockSpec((nb, D_LRU), lambda q: (st(q)[0], 0)),
    ]
    out_shape = [
        jax.ShapeDtypeStruct((bsz * t_len, D), f32),
        jax.ShapeDtypeStruct((POOL_BUF, bsz, D_POOL), f32),
        jax.ShapeDtypeStruct((CONV_W - 1, bsz, D_LRU), f32),
        jax.ShapeDtypeStruct((bsz, D_LRU), f32),
    ]
    for w in to_cast:
        assert w.shape[0] == n_sb * n_tt, (w.shape, n_sb * n_tt)
        blk = (1,) + w.shape[1:]
        in_specs.append(pl.BlockSpec(blk, lambda q: (q, 0, 0)))
        out_specs.append(pl.BlockSpec(blk, lambda q: (q, 0, 0)))
        out_shape.append(jax.ShapeDtypeStruct(w.shape, bf16))
    scratch = [
        pltpu.VMEM((N_SLABS, nb * pitch, LANES), f32),
        pltpu.VMEM((tb, nb, D), f32),
        pltpu.VMEM((tm, D), bf16),
        pltpu.VMEM(((tb + POOL_BUF) * nb, D_POOL), f32),
        pltpu.VMEM(((tb + CONV_W - 1) * nb, D_LRU), f32),
        pltpu.VMEM((tm, D_LRU), f32),
        pltpu.VMEM((tm, D_LRU), f32),
        pltpu.VMEM((tm, D_LRU), f32),
        pltpu.VMEM((nb, D_LRU), f32),
    ]
    body = functools.partial(_mix_body, nb, tb, pitch, n_prev, n_tt, len(to_cast))
    return pl.pallas_call(
        body,
        grid=(n_sb * n_tt,),
        in_specs=in_specs,
        out_specs=out_specs,
        out_shape=out_shape,
        scratch_shapes=scratch,
        compiler_params=pltpu.CompilerParams(
            dimension_semantics=("arbitrary",),
            vmem_limit_bytes=VMEM_LIMIT),
        name=f"mix_nb{nb}_tb{tb}",
    )(x, mod, pool0, conv0, h0, *wts, *to_cast)


def _moe_body(n_half, nb, tb, pitch,
              x1_ref, mod_ref, w_rt_ref, b_rt_ref, tri_ref, wg_ref, wu_ref, wd_ref, ln2g_ref, ln2b_ref,
              y_ref,
              u2, tok_t, tok3, u2s, tok3s, stat, rows_s, cnt_s, ys_all, yacc, slab):
    tm = MOE_ROWS
    hm = nb * tb
    tq = tb // N_RB
    rq = tq * nb
    gs = pl.program_id(1)
    last_gs = N_GROUPS // GPS - 1

    def mod_part(h, k):
        return mod_ref[pl.ds(h * nb, nb), pl.ds(k * D, D)]

    @pl.when(gs == 0)
    def _route():
        for h in range(n_half):
            sh2 = mod_part(h, 3)
            sc2 = mod_part(h, 4)
            for r in range(N_RB):
                rows = pl.ds(h * hm + r * rq, rq)
                v = x1_ref[rows, :].reshape(tq, nb, D) * (1.0 + sc2)[None] + sh2[None]
                u2[rows, :] = v.reshape(rq, D).astype(bf16)
        lt = lax.dot_general(w_rt_ref[...], u2[...], (((1,), (1,)), ((), ())),
                             preferred_element_type=f32) + b_rt_ref[...]
        gl = [lt[k:k + 1, :] for k in range(N_GROUPS)]
        best = gl[0]
        gsel = jnp.zeros_like(best, dtype=i32)
        for k in range(1, N_GROUPS):
            better = gl[k] > best
            best = jnp.where(better, gl[k], best)
            gsel = jnp.where(better, k, gsel)
        denom = jnp.exp(gl[0] - best)
        for k in range(1, N_GROUPS):
            denom = denom + jnp.exp(gl[k] - best)
        p_sel = 1.0 / denom
        es = []
        for j in range(EPG):
            v_j = lt[SUBLANES + j:SUBLANES + j + 1, :]
            for k in range(1, N_GROUPS):
                r0 = SUBLANES * (k + 1) + j
                v_j = jnp.where(gsel == k, lt[r0:r0 + 1, :], v_j)
            es.append(v_j)
        v1 = es[0]
        i1 = jnp.zeros_like(gsel)
        for j in range(1, EPG):
            better = es[j] > v1
            v1 = jnp.where(better, es[j], v1)
            i1 = jnp.where(better, j, i1)
        v2 = jnp.full_like(v1, -jnp.inf)
        i2 = jnp.full_like(gsel, -1)
        for j in range(EPG):
            cand = jnp.logical_and(i1 != j, jnp.logical_or(i2 < 0, es[j] > v2))
            v2 = jnp.where(cand, es[j], v2)
            i2 = jnp.where(cand, j, i2)
        e21 = jnp.exp(v2 - v1)
        w1 = p_sel / (1.0 + e21)
        w2 = p_sel * e21 / (1.0 + e21)
        rid8 = lax.broadcasted_iota(i32, (SUBLANES, tm), 0)
        onehot = jnp.where(rid8 == gsel, 1.0, 0.0)
        blocks = [onehot[:, j * LANES:(j + 1) * LANES] for j in range(tm // LANES)]
        inner = _dot(jnp.concatenate(blocks, axis=0).astype(bf16), tri_ref[...])
        before = jnp.zeros((SUBLANES, 1), f32)
        prefs, befores, sums = [], [], []
        for j, blk in enumerate(blocks):
            prefs.append(inner[j * SUBLANES:(j + 1) * SUBLANES, :] + before)
            befores.append(before)
            sums.append(jnp.sum(blk, axis=1, keepdims=True))
            before = before + sums[-1]
        pref = jnp.concatenate(prefs, axis=1)
        rank = jnp.sum(onehot * pref, axis=0, keepdims=True)
        n_max = 0
        for k in range(N_GROUPS):
            n_k = jnp.sum(jnp.where(gsel == k, 1.0, 0.0)).astype(i32)
            cnt_s[k] = n_k
            n_max = jnp.maximum(n_max, n_k)
        cnt_s[N_GROUPS] = (n_max > CHUNK).astype(i32)
        rows_s[0:1, :] = gsel
        rows_s[1:2, :] = rank.astype(i32)
        rid = lax.broadcasted_iota(i32, (REC_ROWS, tm), 0)
        rec = jnp.zeros((REC_ROWS, tm), f32)
        for j in range(EPG):
            wj = jnp.where(i1 == j, w1, jnp.where(i2 == j, w2, 0.0))
            hi = wj.astype(bf16).astype(f32)
            mid = (wj - hi).astype(bf16).astype(f32)
            lo = (wj - hi - mid).astype(bf16).astype(f32)
            rec = jnp.where(rid == j, hi, rec)
            rec = jnp.where(rid == ROW_LO + j, mid, rec)
            rec = jnp.where(rid == ROW_LO2 + j, lo, rec)
        rec = jnp.where(rid == ROW_GSEL, gsel.astype(f32), rec)
        rec = jnp.where(rid == ROW_RANK, rank, rec)
        rec = jnp.concatenate([rec, jnp.zeros((LANES - REC_ROWS, tm), f32)], axis=0)
        rt = rec.T
        tok_t[...] = rt
        tok3[...] = rt.astype(bf16)
        rid81 = lax.broadcasted_iota(i32, (SUBLANES, 1), 0)
        win0 = jnp.zeros((SUBLANES, 1), f32)
        for k in range(1, N_GROUPS):
            win0 = jnp.where(rid81 == k, f32(WIN_START[k]), win0)
        misfits = jnp.zeros((SUBLANES, 1), f32)
        jsub = lax.broadcasted_iota(i32, (SUB, SUB), 0).astype(f32)
        per_sub = SUB // LANES
        for s in range(tm // SUB):
            n_s = sums[per_sub * s]
            for jb in range(1, per_sub):
                n_s = n_s + sums[per_sub * s + jb]
            pre_s = befores[per_sub * s]
            o_s = jnp.zeros((SUBLANES, 1), f32)
            for k in range(N_GROUPS - 1):
                o_s = o_s + jnp.where(rid81 > k, n_s[k:k + 1, :], 0.0)
            cols = slice(s * SUB, (s + 1) * SUB)
            pos1 = rank[:, cols] + jnp.sum(onehot[:, cols] * (o_s - pre_s), axis=0, keepdims=True)
            q1 = jnp.where(pos1 == jsub, 1.0, 0.0).astype(bf16)
            rows = pl.ds(s * SUB, SUB)
            u2s[rows, :] = _dot(q1, u2[rows, :]).astype(bf16)
            tok3s[rows, :] = _dot(q1, tok3[rows, :]).astype(bf16)
            stat[3 * s] = jnp.broadcast_to(n_s, (SUBLANES, LANES))
            stat[3 * s + 1] = jnp.broadcast_to(o_s, (SUBLANES, LANES))
            stat[3 * s + 2] = jnp.broadcast_to(pre_s, (SUBLANES, LANES))
            ok = jnp.logical_and(o_s >= win0, o_s + n_s <= win0 + f32(WIN))
            misfits = misfits + jnp.where(jnp.logical_or(ok, rid81 >= N_GROUPS), 0.0, 1.0)
        cnt_s[N_GROUPS + 1] = (jnp.sum(misfits) == 0.0).astype(i32)

    def gather_whole_tile(g, base):
        gsel_row = rows_s[0:1, :]
        rank_row = rows_s[1:2, :]
        jj = lax.broadcasted_iota(i32, (CHUNK, tm), 0) + base
        pm = jnp.where(jnp.logical_and(gsel_row == g, rank_row == jj), 1.0, 0.0).astype(bf16)
        return _dot(pm, u2[...]).astype(bf16), _dot(pm, tok3[...])

    def gather_windows(g):
        w0 = jnp.where(g == 0, WIN_START[0],
                       jnp.where(g == 1, WIN_START[1], jnp.where(g == 2, WIN_START[2], WIN_START[3])))
        pos = lax.broadcasted_iota(i32, (1, WIN), 1).astype(f32) + w0.astype(f32)
        jj = lax.broadcasted_iota(i32, (CHUNK, WIN), 0).astype(f32)
        pieces, wins, wtoks = [], [], []
        for s in range(tm // SUB):
            n = stat[3 * s, pl.ds(g, 1), :]
            o = stat[3 * s + 1, pl.ds(g, 1), :]
            pre = stat[3 * s + 2, pl.ds(g, 1), :]
            mine = jnp.logical_and(pos >= o, pos < o + n)
            pieces.append(jnp.where(jnp.logical_and(mine, pre + pos - o == jj), 1.0, 0.0).astype(bf16))
            wrows = pl.ds(pl.multiple_of(s * SUB + w0, 16), WIN)
            wins.append(u2s[wrows, :])
            wtoks.append(tok3s[wrows, :])
        p2 = jnp.concatenate(pieces, axis=1)
        return (_dot(p2, jnp.concatenate(wins, axis=0)).astype(bf16),
                _dot(p2, jnp.concatenate(wtoks, axis=0)))

    def sorted_experts(gi, base, windowed=False):
        g = gs * GPS + gi
        xs, cs3 = gather_windows(g) if windowed else gather_whole_tile(g, base)
        cs = (cs3 + pltpu.roll(cs3, LANES - ROW_LO, axis=1)
              + pltpu.roll(cs3, LANES - ROW_LO2, axis=1))
        acts = []
        for e in range(EPG):
            hg = _dot(xs, wg_ref[gi * EPG + e])
            hu = _dot(xs, wu_ref[gi * EPG + e])
            acts.append((hg * _sigmoid(hg) * hu * cs[:, e:e + 1]).astype(bf16))
        hb = jnp.concatenate(acts, axis=1)
        wd_g = wd_ref[pl.ds(gi * EPG, EPG)].reshape(EPG * D_EXPERT, D)
        return _dot(hb, wd_g).astype(bf16)

    segments_fit = cnt_s[N_GROUPS + 1] > 0

    def first_chunks(windowed):
        for gi in range(GPS):
            first_row = pl.multiple_of((gs * GPS + gi) * CHUNK_PAD, CHUNK_PAD)
            ys_all[pl.ds(first_row, CHUNK), :] = sorted_experts(gi, 0, windowed)

    @pl.when(segments_fit)
    def _from_windows():
        first_chunks(True)

    @pl.when(jnp.logical_not(segments_fit))
    def _from_whole_tile():
        first_chunks(False)

    @pl.when(gs == 0)
    def _zero_pad_rows():
        for k in range(N_GROUPS):
            ys_all[pl.ds(k * CHUNK_PAD + CHUNK, CHUNK_PAD - CHUNK), :] = jnp.zeros((CHUNK_PAD - CHUNK, D), bf16)

    has_overflow = cnt_s[N_GROUPS] > 0

    @pl.when(jnp.logical_and(gs == 0, has_overflow))
    def _zero_acc():
        yacc[...] = jnp.zeros((tm, D), f32)

    for gi in range(GPS):
        g = gs * GPS + gi

        def overflow(c, carry, gi=gi, g=g):
            base = c * CHUNK
            ys = sorted_experts(gi, base)
            rec = tok_t[...]
            gsel_col = rec[:, ROW_GSEL:ROW_GSEL + 1]
            rank_col = rec[:, ROW_RANK:ROW_RANK + 1]
            jl = (lax.broadcasted_iota(i32, (tm, CHUNK), 1) + base).astype(f32)
            pt = jnp.where(jnp.logical_and(gsel_col == g.astype(f32), rank_col == jl), 1.0, 0.0).astype(bf16)
            yacc[...] += _dot(pt, ys)
            return carry

        lax.fori_loop(1, pl.cdiv(cnt_s[g], CHUNK), overflow, 0)

    def finish(with_acc):
        tqf = FIN_ROWS // nb
        jl = lax.broadcasted_iota(i32, (FIN_ROWS, N_GROUPS * CHUNK_PAD), 1).astype(f32)
        for h in range(n_half):
            g2 = mod_part(h, 5)
            for r in range(hm // FIN_ROWS):
                rows = pl.ds(h * hm + r * FIN_ROWS, FIN_ROWS)
                rec = tok_t[rows, :]
                gsel_col = rec[:, ROW_GSEL:ROW_GSEL + 1]
                rank_col = rec[:, ROW_RANK:ROW_RANK + 1]
                key = jnp.where(rank_col < f32(CHUNK), gsel_col * f32(CHUNK_PAD) + rank_col, -1.0)
                pt = jnp.where(key == jl, 1.0, 0.0).astype(bf16)
                y = _dot(pt, ys_all[...])
                if with_acc:
                    y = y + yacc[rows, :]
                v = ALPHA * x1_ref[rows, :] + ((1.0 + g2)[None] * y.reshape(tqf, nb, D)).reshape(FIN_ROWS, D)
                yn = _layer_norm(v, ln2g_ref[...], ln2b_ref[...])
                for tl in range(tqf):
                    for j in range(N_SLABS):
                        slab[j, pl.ds(r * tqf + tl, nb, stride=pitch), :] = (
                            yn[tl * nb:(tl + 1) * nb, j * LANES:(j + 1) * LANES])
            for b in range(nb):
                for j in range(N_SLABS):
                    y_ref[h * nb + b, :, pl.ds(j * LANES, LANES)] = slab[j, pl.ds(b * pitch, tb), :]

    @pl.when(jnp.logical_and(gs == last_gs, jnp.logical_not(has_overflow)))
    def _finish():
        finish(False)

    @pl.when(jnp.logical_and(gs == last_gs, has_overflow))
    def _finish_with_overflow():
        finish(True)


def _run_moe(x1, mod, mod_blk0, n_half, nb, tb, out_seqs, out_len, wts):
    n_tiles = x1.shape[0] // MOE_ROWS
    assert n_half * nb * tb == MOE_ROWS
    seq_per_tile = n_half * nb
    n_tt = out_len // tb
    pitch = _slab_pitch(tb)
    st = lambda q: (q // n_tt, q % n_tt)
    const2 = lambda q, g: (0, 0)
    grp3 = lambda q, g: (g, 0, 0)
    single = pl.Buffered(1)
    in_specs = [
        pl.BlockSpec((MOE_ROWS, D), lambda q, g: (q, 0)),
        pl.BlockSpec((seq_per_tile, 6 * D), lambda q, g: (mod_blk0 + st(q)[0], 0)),
        pl.BlockSpec((ROUTE_ROWS, D), const2),
        pl.BlockSpec((ROUTE_ROWS, 1), const2),
        pl.BlockSpec((LANES, LANES), const2),
        pl.BlockSpec((GPS * EPG, D, D_EXPERT), grp3),
        pl.BlockSpec((GPS * EPG, D, D_EXPERT), grp3),
        pl.BlockSpec((GPS * EPG, D_EXPERT, D), grp3),
        pl.BlockSpec((1, D), const2),
        pl.BlockSpec((1, D), const2),
    ]
    scratch = [
        pltpu.VMEM((MOE_ROWS, D), bf16),
        pltpu.VMEM((MOE_ROWS, LANES), f32),
        pltpu.VMEM((MOE_ROWS, LANES), bf16),
        pltpu.VMEM((MOE_ROWS, D), bf16),
        pltpu.VMEM((MOE_ROWS, LANES), bf16),
        pltpu.VMEM((3 * (MOE_ROWS // SUB), SUBLANES, LANES), f32),
        pltpu.VMEM((SUBLANES, MOE_ROWS), i32),
        pltpu.SMEM((N_GROUPS + 2,), i32),
        pltpu.VMEM((N_GROUPS * CHUNK_PAD, D), bf16),
        pltpu.VMEM((MOE_ROWS, D), f32),
        pltpu.VMEM((N_SLABS, nb * pitch, LANES), f32),
    ]
    body = functools.partial(_moe_body, n_half, nb, tb, pitch)
    return pl.pallas_call(
        body,
        grid=(n_tiles, N_GROUPS // GPS),
        in_specs=in_specs,
        out_specs=pl.BlockSpec((seq_per_tile, tb, D), lambda q, g: (*st(q), 0)),
        out_shape=jax.ShapeDtypeStruct((out_seqs, out_len, D), f32),
        scratch_shapes=scratch,
        compiler_params=pltpu.CompilerParams(
            dimension_semantics=("arbitrary", "arbitrary"),
            vmem_limit_bytes=VMEM_LIMIT),
        name=f"moe_nb{nb}_tb{tb}",
    )(x1, mod, *wts)


def _pair_blocks(w):
    n, k, _ = w.shape
    wp = w.reshape(n // 2, 2, k, k)
    eye = jnp.eye(2, dtype=w.dtype)
    return jnp.einsum('phij,hg->phigj', wp, eye).reshape(n // 2, 2 * k, 2 * k)


def kernel(x_prompt, x_sample, c_prompt, c_sample, state_pool, state_conv, state_lru, w_ada, b_ada, w_in, w_pool, pool_scale, w_conv, b_conv, w_a, b_a, w_x, b_x, lru_lambda, w_out, ln1_g, ln1_b, w_group, b_group, w_route, b_route, w_gate, w_up, w_down, ln2_g, ln2_b):
    l = 0
    bp, tp, _ = x_prompt.shape
    bs, ts, _ = x_sample.shape
    mod, w_in_b, w_out_b = _ada_mod(
        jnp.concatenate([c_sample, c_prompt], axis=0), w_ada[l], b_ada[l], w_in[l], w_out[l])

    w_rt = jnp.zeros((ROUTE_ROWS, D), f32).at[0:N_GROUPS].set(w_group[l].T)
    b_rt = jnp.zeros((ROUTE_ROWS,), f32).at[0:N_GROUPS].set(b_group[l])
    for k in range(N_GROUPS):
        r0 = SUBLANES * (k + 1)
        w_rt = w_rt.at[r0:r0 + EPG].set(w_route[l][:, k * EPG:(k + 1) * EPG].T)
        b_rt = b_rt.at[r0:r0 + EPG].set(b_route[l][k * EPG:(k + 1) * EPG])
    tri = jnp.triu(jnp.ones((LANES, LANES), bf16), 1)

    mix_wts = (
        w_in_b,
        w_pool[l].astype(bf16),
        pool_scale[l].reshape(1, D_POOL),
        w_conv[l],
        b_conv[l].reshape(1, D_LRU),
        jnp.stack([_pair_blocks(w_a[l]), _pair_blocks(w_x[l])]).astype(bf16),
        jnp.concatenate([b_a[l], b_x[l]]).reshape(1, 2 * D_LRU),
        lru_lambda[l].reshape(1, D_LRU),
        w_out_b,
        ln1_g[l].reshape(1, D),
        ln1_b[l].reshape(1, D),
    )
    tb_p = MOE_ROWS // bp
    zp = jnp.zeros((POOL_BUF, bp, D_POOL), f32)
    zc = jnp.zeros((CONV_W - 1, bp, D_LRU), f32)
    zh = jnp.zeros((bp, D_LRU), f32)
    x1p, pool_p, conv_p, lru_p, wg_b, wu_b, wd_b = _run_mix(
        x_prompt, mod, bs // bp, zp, zc, zh, 0, bp, tb_p, mix_wts, (w_gate[l], w_up[l], w_down[l]))
    moe_wts = (
        w_rt.astype(bf16),
        b_rt.reshape(ROUTE_ROWS, 1),
        tri,
        wg_b,
        wu_b,
        wd_b,
        ln2_g[l].reshape(1, D),
        ln2_b[l].reshape(1, D),
    )
    yp = _run_moe(x1p, mod, bs // bp, 1, bp, tb_p, bp, tp, moe_wts)

    n_prev_s = min(PAST_LEN, POOL_BUF)
    nb_s = bs // 2
    x1s, pool_s, conv_s, lru_s = _run_mix(
        x_sample, mod, 0, state_pool[l].transpose(1, 0, 2), state_conv[l].transpose(1, 0, 2),
        state_lru[l], n_prev_s, nb_s, ts, mix_wts)
    ys = _run_moe(x1s, mod, 0, 2, nb_s, ts, bs, ts, moe_wts)

    tr = lambda a: a.transpose(1, 0, 2)[None]
    return (yp, ys, tr(pool_p), tr(conv_p), lru_p[None], tr(pool_s), tr(conv_s), lru_s[None])
```

```python
import functools
import math

import jax
import jax.numpy as jnp
from jax import lax
from jax.experimental import pallas as pl
from jax.experimental.pallas import tpu as pltpu

D = 1024
D_POOL = 512
D_LRU = 512
D_IN = D_POOL + 2 * D_LRU
POOL_WINDOWS = (2, 4, 8, 16)
POOL_GROUP = 128
POOL_BUF = 15
CONV_W = 4
LRU_C = 8.0
N_GROUPS = 4
EPG = 4
D_EXPERT = 256
DEPTH = 1
ALPHA = (2.0 * DEPTH) ** 0.25
LN_EPS = 1e-5
PAST_LEN = 16384

LANES = 128
SUBLANES = 8
MXU_DIM = 256
N_SLABS = D // LANES
ROUTE_ROWS = 48
MOE_ROWS = 1024
CHUNK = 288
CHUNK_PAD = 320
assert CHUNK <= CHUNK_PAD and (N_GROUPS * CHUNK_PAD) % MXU_DIM == 0
ROW_LO = 4
ROW_LO2 = 8
ROW_GSEL = 12
ROW_RANK = 13
REC_ROWS = 16
SUB = 256
WIN = 128
WIN_START = (0, 32, 96, 128)
N_RB = 4
GPS = 2
FIN_ROWS = 256
ADA_STEPS = 8
V7X_VMEM_BYTES = 64 * 1024 * 1024
VMEM_RESERVE_BYTES = 4 * 1024 * 1024
VMEM_LIMIT = V7X_VMEM_BYTES - VMEM_RESERVE_BYTES
LOG2E = 1.0 / math.log(2.0)
SQRT_GUARD = 1e-30

f32 = jnp.float32
bf16 = jnp.bfloat16
i32 = jnp.int32


def _dot(a, b):
    return jnp.dot(a, b, preferred_element_type=f32)


def _sigmoid(x):
    return 1.0 / (1.0 + jnp.exp2(x * (-LOG2E)))


def _gelu_tanh(x):
    k1 = -2.0 * math.sqrt(2.0 / math.pi) * LOG2E
    return x / (1.0 + jnp.exp2(x * (k1 + (k1 * 0.044715) * (x * x))))


def _layer_norm(v, g, b):
    mu = jnp.mean(v, axis=-1, keepdims=True)
    c = v - mu
    var = jnp.mean(c * c, axis=-1, keepdims=True)
    return c * lax.rsqrt(var + LN_EPS) * g + b


def _slab_pitch(tb):
    return tb + SUBLANES if (tb // SUBLANES) % 2 == 0 else tb


def _ada_body(c_ref, w_ref, b_ref, w_in_ref, w_out_ref, o_ref, w_in_b_ref, w_out_b_ref):
    o_ref[...] = _dot(c_ref[...].astype(bf16), w_ref[...].astype(bf16)) + b_ref[...]
    w_in_b_ref[...] = w_in_ref[...].astype(bf16)
    w_out_b_ref[...] = w_out_ref[...].astype(bf16)


def _ada_mod(c_all, w_ada, b_ada, w_in, w_out):
    n = c_all.shape[0]
    steps = ADA_STEPS
    bn = 6 * D // steps
    br = D // steps
    return pl.pallas_call(
        _ada_body,
        grid=(steps,),
        in_specs=[
            pl.BlockSpec((n, D), lambda j: (0, 0)),
            pl.BlockSpec((D, bn), lambda j: (0, j)),
            pl.BlockSpec((1, bn), lambda j: (0, j)),
            pl.BlockSpec((br, D_IN), lambda j: (j, 0)),
            pl.BlockSpec((br, D), lambda j: (j, 0)),
        ],
        out_specs=[
            pl.BlockSpec((n, bn), lambda j: (0, j)),
            pl.BlockSpec((br, D_IN), lambda j: (j, 0)),
            pl.BlockSpec((br, D), lambda j: (j, 0)),
        ],
        out_shape=[
            jax.ShapeDtypeStruct((n, 6 * D), f32),
            jax.ShapeDtypeStruct((D, D_IN), bf16),
            jax.ShapeDtypeStruct((D, D), bf16),
        ],
        name="ada_mod",
    )(c_all, w_ada, b_ada.reshape(1, 6 * D), w_in, w_out)


def _mix_body(nb, tb, pitch, n_prev, n_tt, n_cast,
              x_ref, mod_ref, pool0_ref, conv0_ref, h0_ref,
              w_in_ref, w_pool_ref, pscale_ref, w_conv_ref, b_conv_ref,
              w_ax_ref, b_ax_ref, lam_ref, w_out_ref, ln1g_ref, ln1b_ref,
              *rest):
    cast_in = rest[:n_cast]
    x1_ref, npool_ref, nconv_ref, nh_ref = rest[n_cast:n_cast + 4]
    cast_out = rest[n_cast + 4:2 * n_cast + 4]
    slab, xt, ycat, zpool, zconv, a_s, b_s, gl_s, h_s = rest[2 * n_cast + 4:]
    for src, dst in zip(cast_in, cast_out):
        dst[...] = src[...].astype(bf16)
    tm = nb * tb
    tq = tb // N_RB
    rq = tq * nb
    ti = pl.program_id(0) % n_tt

    def mod_part(k):
        return mod_ref[:, pl.ds(k * D, D)]

    @pl.when(ti == 0)
    def _init_state():
        zpool[pl.ds(0, POOL_BUF * nb), :] = pool0_ref[...].reshape(POOL_BUF * nb, D_POOL)
        zconv[pl.ds(0, (CONV_W - 1) * nb), :] = conv0_ref[...].reshape((CONV_W - 1) * nb, D_LRU)
        h_s[...] = h0_ref[...]

    for b in range(nb):
        for j in range(N_SLABS):
            slab[j, pl.ds(b * pitch, tb), :] = x_ref[b, :, pl.ds(j * LANES, LANES)]

    sh1 = mod_part(0)
    sc1 = mod_part(1)
    g1 = mod_part(2)
    nl = -lam_ref[...]
    softplus = jnp.maximum(nl, 0.0) + jnp.log(1.0 + jnp.exp(-jnp.abs(nl)))
    log2a_unit = (-LRU_C * LOG2E) * softplus
    row = lax.broadcasted_iota(i32, (tm, LANES), 0)
    t_loc = lax.shift_right_logical(row, int(math.log2(nb)))
    h = h_s[...]

    for t in range(tb):
        for j in range(N_SLABS):
            xt[t, :, pl.ds(j * LANES, LANES)] = slab[j, pl.ds(t, nb, stride=pitch), :]

    for r in range(N_RB):
        u = xt[pl.ds(r * tq, tq)] * (1.0 + sc1)[None] + sh1[None]
        ycat[pl.ds(r * rq, rq), :] = u.reshape(rq, D).astype(bf16)
    ub = ycat[...]
    zpool[pl.ds(POOL_BUF * nb, tm), :] = _dot(ub, w_in_ref[:, pl.ds(0, D_POOL)])
    zconv[pl.ds((CONV_W - 1) * nb, tm), :] = _dot(ub, w_in_ref[:, pl.ds(D_POOL, D_LRU)])
    half = D_LRU // 2
    for hpart in range(2):
        cols = pl.ds(D_POOL + D_LRU + hpart * half, half)
        gl_s[:, pl.ds(hpart * half, half)] = _gelu_tanh(_dot(ub, w_in_ref[:, cols]))

    t_glob = (ti * tb + t_loc + (1 + n_prev)).astype(f32)
    for c, w in enumerate(POOL_WINDOWS):
        lanes = pl.ds(c * LANES, LANES)
        s = zpool[pl.ds((POOL_BUF + 1 - w) * nb, tm + (w - 1) * nb), lanes]
        step = 1
        while step < w:
            s = s[step * nb:] + s[:-step * nb]
            step *= 2
        cnt = jnp.minimum(f32(w), t_glob)
        dlt = s / cnt - zpool[pl.ds(POOL_BUF * nb, tm), lanes]
        yp = _dot(dlt.astype(bf16), w_pool_ref[c]) * pscale_ref[:, lanes]
        ycat[:, lanes] = yp.astype(bf16)

    for j in range(D_LRU // LANES):
        lanes = pl.ds(j * LANES, LANES)
        xc = b_conv_ref[:, lanes] + zconv[pl.ds(0, tm), lanes] * w_conv_ref[0:1, lanes]
        for k in range(1, CONV_W):
            xc = xc + zconv[pl.ds(k * nb, tm), lanes] * w_conv_ref[k:k + 1, lanes]
        xcb = xc.astype(bf16)
        r = _sigmoid(_dot(xcb, w_ax_ref[0, j]) + b_ax_ref[:, lanes])
        ig = _sigmoid(_dot(xcb, w_ax_ref[1, j]) + b_ax_ref[:, pl.ds(D_LRU + j * LANES, LANES)])
        a = jnp.exp2(r * log2a_unit[:, j * LANES:(j + 1) * LANES])
        a_s[:, lanes] = a
        om = jnp.maximum(1.0 - a * a, 0.0)
        mult = om * lax.rsqrt(jnp.maximum(om, SQRT_GUARD))
        b_s[:, lanes] = mult * ig * xc

    for t in range(tb):
        rows = pl.ds(t * nb, nb)
        h = a_s[rows, :] * h + b_s[rows, :]
        b_s[rows, :] = h
    ycat[:, pl.ds(D_POOL, D_LRU)] = (b_s[...] * gl_s[...]).astype(bf16)

    yc = ycat[...]
    cb = D // N_RB
    for j in range(N_RB):
        cols = pl.ds(j * cb, cb)
        mix = _dot(yc, w_out_ref[:, cols]).reshape(tb, nb, cb)
        xt[:, :, cols] = ALPHA * xt[:, :, cols] + (1.0 + g1[:, j * cb:(j + 1) * cb])[None] * mix
    for r in range(N_RB):
        xn = _layer_norm(xt[pl.ds(r * tq, tq)], ln1g_ref[...][None], ln1b_ref[...][None])
        x1_ref[pl.ds(r * rq, rq), :] = xn.reshape(rq, D)

    h_s[...] = h
    zpool[pl.ds(0, POOL_BUF * nb), :] = zpool[pl.ds(tm, POOL_BUF * nb), :]
    zconv[pl.ds(0, (CONV_W - 1) * nb), :] = zconv[pl.ds(tm, (CONV_W - 1) * nb), :]

    @pl.when(ti == n_tt - 1)
    def _emit_state():
        npool_ref[...] = zpool[pl.ds(0, POOL_BUF * nb), :].reshape(POOL_BUF, nb, D_POOL)
        nconv_ref[...] = zconv[pl.ds(0, (CONV_W - 1) * nb), :].reshape(CONV_W - 1, nb, D_LRU)
        nh_ref[...] = h_s[...]


def _run_mix(x, mod, mod_blk0, pool0, conv0, h0, n_prev, nb, tb, wts, to_cast=()):
    bsz, t_len, _ = x.shape
    n_sb = bsz // nb
    n_tt = t_len // tb
    tm = nb * tb
    pitch = _slab_pitch(tb)
    st = lambda q: (q // n_tt, q % n_tt)
    const2 = lambda q: (0, 0)
    const3 = lambda q: (0, 0, 0)
    const4 = lambda q: (0, 0, 0, 0)
    single = pl.Buffered(1)
    in_specs = [
        pl.BlockSpec((nb, tb, D), lambda q: (*st(q), 0)),
        pl.BlockSpec((nb, 6 * D), lambda q: (mod_blk0 + st(q)[0], 0)),
        pl.BlockSpec((POOL_BUF, nb, D_POOL), lambda q: (0, st(q)[0], 0)),
        pl.BlockSpec((CONV_W - 1, nb, D_LRU), lambda q: (0, st(q)[0], 0)),
        pl.BlockSpec((nb, D_LRU), lambda q: (st(q)[0], 0)),
        pl.BlockSpec((D, D_IN), const2, pipeline_mode=single),
        pl.BlockSpec((4, POOL_GROUP, POOL_GROUP), const3),
        pl.BlockSpec((1, D_POOL), const2),
        pl.BlockSpec((CONV_W, D_LRU), const2),
        pl.BlockSpec((1, D_LRU), const2),
        pl.BlockSpec((2, D_LRU // LANES, LANES, LANES), const4),
        pl.BlockSpec((1, 2 * D_LRU), const2),
        pl.BlockSpec((1, D_LRU), const2),
        pl.BlockSpec((D, D), const2, pipeline_mode=single),
        pl.BlockSpec((1, D), const2),
        pl.BlockSpec((1, D), const2),
    ]
    out_specs = [
        pl.BlockSpec((tm, D), lambda q: (q, 0)),
        pl.BlockSpec((POOL_BUF, nb, D_POOL), lambda q: (0, st(q)[0], 0)),
        pl.BlockSpec((CONV_W - 1, nb, D_LRU), lambda q: (0, st(q)[0], 0)),
        pl.BlockSpec((nb, D_LRU), lambda q: (st(q)[0], 0)),
    ]
    out_shape = [
        jax.ShapeDtypeStruct((bsz * t_len, D), f32),
        jax.ShapeDtypeStruct((POOL_BUF, bsz, D_POOL), f32),
        jax.ShapeDtypeStruct((CONV_W - 1, bsz, D_LRU), f32),
        jax.ShapeDtypeStruct((bsz, D_LRU), f32),
    ]
    for w in to_cast:
        assert w.shape[0] == n_sb * n_tt, (w.shape, n_sb * n_tt)
        blk = (1,) + w.shape[1:]
        in_specs.append(pl.BlockSpec(blk, lambda q: (q, 0, 0)))
        out_specs.append(pl.BlockSpec(blk, lambda q: (q, 0, 0)))
        out_shape.append(jax.ShapeDtypeStruct(w.shape, bf16))
    scratch = [
        pltpu.VMEM((N_SLABS, nb * pitch, LANES), f32),
        pltpu.VMEM((tb, nb, D), f32),
        pltpu.VMEM((tm, D), bf16),
        pltpu.VMEM(((tb + POOL_BUF) * nb, D_POOL), f32),
        pltpu.VMEM(((tb + CONV_W - 1) * nb, D_LRU), f32),
        pltpu.VMEM((tm, D_LRU), f32),
        pltpu.VMEM((tm, D_LRU), f32),
        pltpu.VMEM((tm, D_LRU), f32),
        pltpu.VMEM((nb, D_LRU), f32),
    ]
    body = functools.partial(_mix_body, nb, tb, pitch, n_prev, n_tt, len(to_cast))
    return pl.pallas_call(
        body,
        grid=(n_sb * n_tt,),
        in_specs=in_specs,
        out_specs=out_specs,
        out_shape=out_shape,
        scratch_shapes=scratch,
        compiler_params=pltpu.CompilerParams(
            dimension_semantics=("arbitrary",),
            vmem_limit_bytes=VMEM_LIMIT),
        name=f"mix_nb{nb}_tb{tb}",
    )(x, mod, pool0, conv0, h0, *wts, *to_cast)


def _moe_body(n_half, nb, tb, pitch,
              x1_ref, mod_ref, w_rt_ref, b_rt_ref, tri_ref, wg_ref, wu_ref, wd_ref, ln2g_ref, ln2b_ref,
              y_ref,
              u2, tok_t, tok3, u2s, tok3s, stat, rows_s, cnt_s, ys_all, yacc, slab):
    tm = MOE_ROWS
    hm = nb * tb
    tq = tb // N_RB
    rq = tq * nb
    gs = pl.program_id(1)
    last_gs = N_GROUPS // GPS - 1

    def mod_part(h, k):
        return mod_ref[pl.ds(h * nb, nb), pl.ds(k * D, D)]

    @pl.when(gs == 0)
    def _route():
        for h in range(n_half):
            sh2 = mod_part(h, 3)
            sc2 = mod_part(h, 4)
            for r in range(N_RB):
                rows = pl.ds(h * hm + r * rq, rq)
                v = x1_ref[rows, :].reshape(tq, nb, D) * (1.0 + sc2)[None] + sh2[None]
                u2[rows, :] = v.reshape(rq, D).astype(bf16)
        lt = lax.dot_general(w_rt_ref[...], u2[...], (((1,), (1,)), ((), ())),
                             preferred_element_type=f32) + b_rt_ref[...]
        gl = [lt[k:k + 1, :] for k in range(N_GROUPS)]
        best = gl[0]
        gsel = jnp.zeros_like(best, dtype=i32)
        for k in range(1, N_GROUPS):
            better = gl[k] > best
            best = jnp.where(better, gl[k], best)
            gsel = jnp.where(better, k, gsel)
        denom = jnp.exp(gl[0] - best)
        for k in range(1, N_GROUPS):
            denom = denom + jnp.exp(gl[k] - best)
        p_sel = 1.0 / denom
        es = []
        for j in range(EPG):
            v_j = lt[SUBLANES + j:SUBLANES + j + 1, :]
            for k in range(1, N_GROUPS):
                r0 = SUBLANES * (k + 1) + j
                v_j = jnp.where(gsel == k, lt[r0:r0 + 1, :], v_j)
            es.append(v_j)
        v1 = es[0]
        i1 = jnp.zeros_like(gsel)
        for j in range(1, EPG):
            better = es[j] > v1
            v1 = jnp.where(better, es[j], v1)
            i1 = jnp.where(better, j, i1)
        v2 = jnp.full_like(v1, -jnp.inf)
        i2 = jnp.full_like(gsel, -1)
        for j in range(EPG):
            cand = jnp.logical_and(i1 != j, jnp.logical_or(i2 < 0, es[j] > v2))
            v2 = jnp.where(cand, es[j], v2)
            i2 = jnp.where(cand, j, i2)
        e21 = jnp.exp(v2 - v1)
        w1 = p_sel / (1.0 + e21)
        w2 = p_sel * e21 / (1.0 + e21)
        rid8 = lax.broadcasted_iota(i32, (SUBLANES, tm), 0)
        onehot = jnp.where(rid8 == gsel, 1.0, 0.0)
        blocks = [onehot[:, j * LANES:(j + 1) * LANES] for j in range(tm // LANES)]
        inner = _dot(jnp.concatenate(blocks, axis=0).astype(bf16), tri_ref[...])
        before = jnp.zeros((SUBLANES, 1), f32)
        prefs, befores, sums = [], [], []
        for j, blk in enumerate(blocks):
            prefs.append(inner[j * SUBLANES:(j + 1) * SUBLANES, :] + before)
            befores.append(before)
            sums.append(jnp.sum(blk, axis=1, keepdims=True))
            before = before + sums[-1]
        pref = jnp.concatenate(prefs, axis=1)
        rank = jnp.sum(onehot * pref, axis=0, keepdims=True)
        n_max = 0
        for k in range(N_GROUPS):
            n_k = jnp.sum(jnp.where(gsel == k, 1.0, 0.0)).astype(i32)
            cnt_s[k] = n_k
            n_max = jnp.maximum(n_max, n_k)
        cnt_s[N_GROUPS] = (n_max > CHUNK).astype(i32)
        rows_s[0:1, :] = gsel
        rows_s[1:2, :] = rank.astype(i32)
        rid = lax.broadcasted_iota(i32, (REC_ROWS, tm), 0)
        rec = jnp.zeros((REC_ROWS, tm), f32)
        for j in range(EPG):
            wj = jnp.where(i1 == j, w1, jnp.where(i2 == j, w2, 0.0))
            hi = wj.astype(bf16).astype(f32)
            mid = (wj - hi).astype(bf16).astype(f32)
            lo = (wj - hi - mid).astype(bf16).astype(f32)
            rec = jnp.where(rid == j, hi, rec)
            rec = jnp.where(rid == ROW_LO + j, mid, rec)
            rec = jnp.where(rid == ROW_LO2 + j, lo, rec)
        rec = jnp.where(rid == ROW_GSEL, gsel.astype(f32), rec)
        rec = jnp.where(rid == ROW_RANK, rank, rec)
        rec = jnp.concatenate([rec, jnp.zeros((LANES - REC_ROWS, tm), f32)], axis=0)
        rt = rec.T
        tok_t[...] = rt
        tok3[...] = rt.astype(bf16)
        rid81 = lax.broadcasted_iota(i32, (SUBLANES, 1), 0)
        win0 = jnp.zeros((SUBLANES, 1), f32)
        for k in range(1, N_GROUPS):
            win0 = jnp.where(rid81 == k, f32(WIN_START[k]), win0)
        misfits = jnp.zeros((SUBLANES, 1), f32)
        jsub = lax.broadcasted_iota(i32, (SUB, SUB), 0).astype(f32)
        per_sub = SUB // LANES
        for s in range(tm // SUB):
            n_s = sums[per_sub * s]
            for jb in range(1, per_sub):
                n_s = n_s + sums[per_sub * s + jb]
            pre_s = befores[per_sub * s]
            o_s = jnp.zeros((SUBLANES, 1), f32)
            for k in range(N_GROUPS - 1):
                o_s = o_s + jnp.where(rid81 > k, n_s[k:k + 1, :], 0.0)
            cols = slice(s * SUB, (s + 1) * SUB)
            pos1 = rank[:, cols] + jnp.sum(onehot[:, cols] * (o_s - pre_s), axis=0, keepdims=True)
            q1 = jnp.where(pos1 == jsub, 1.0, 0.0).astype(bf16)
            rows = pl.ds(s * SUB, SUB)
            u2s[rows, :] = _dot(q1, u2[rows, :]).astype(bf16)
            tok3s[rows, :] = _dot(q1, tok3[rows, :]).astype(bf16)
            stat[3 * s] = jnp.broadcast_to(n_s, (SUBLANES, LANES))
            stat[3 * s + 1] = jnp.broadcast_to(o_s, (SUBLANES, LANES))
            stat[3 * s + 2] = jnp.broadcast_to(pre_s, (SUBLANES, LANES))
            ok = jnp.logical_and(o_s >= win0, o_s + n_s <= win0 + f32(WIN))
            misfits = misfits + jnp.where(jnp.logical_or(ok, rid81 >= N_GROUPS), 0.0, 1.0)
        cnt_s[N_GROUPS + 1] = (jnp.sum(misfits) == 0.0).astype(i32)

    def gather_whole_tile(g, base):
        gsel_row = rows_s[0:1, :]
        rank_row = rows_s[1:2, :]
        jj = lax.broadcasted_iota(i32, (CHUNK, tm), 0) + base
        pm = jnp.where(jnp.logical_and(gsel_row == g, rank_row == jj), 1.0, 0.0).astype(bf16)
        return _dot(pm, u2[...]).astype(bf16), _dot(pm, tok3[...])

    def gather_windows(g):
        w0 = WIN_START[g]
        pos = lax.broadcasted_iota(i32, (1, WIN), 1).astype(f32) + f32(w0)
        jj = lax.broadcasted_iota(i32, (CHUNK, WIN), 0).astype(f32)
        pieces, wins, wtoks = [], [], []
        for s in range(tm // SUB):
            n = stat[3 * s, g:g + 1, :]
            o = stat[3 * s + 1, g:g + 1, :]
            pre = stat[3 * s + 2, g:g + 1, :]
            mine = jnp.logical_and(pos >= o, pos < o + n)
            pieces.append(jnp.where(jnp.logical_and(mine, pre + pos - o == jj), 1.0, 0.0).astype(bf16))
            wrows = pl.ds(s * SUB + w0, WIN)
            wins.append(u2s[wrows, :])
            wtoks.append(tok3s[wrows, :])
        p2 = jnp.concatenate(pieces, axis=1)
        return (_dot(p2, jnp.concatenate(wins, axis=0)).astype(bf16),
                _dot(p2, jnp.concatenate(wtoks, axis=0)))

    def sorted_experts(gi, base, static_gs=None):
        if static_gs is None:
            xs, cs3 = gather_whole_tile(gs * GPS + gi, base)
        else:
            xs, cs3 = gather_windows(static_gs * GPS + gi)
        cs = (cs3 + pltpu.roll(cs3, LANES - ROW_LO, axis=1)
              + pltpu.roll(cs3, LANES - ROW_LO2, axis=1))
        acts = []
        for e in range(EPG):
            hg = _dot(xs, wg_ref[gi * EPG + e])
            hu = _dot(xs, wu_ref[gi * EPG + e])
            acts.append((hg * _sigmoid(hg) * hu * cs[:, e:e + 1]).astype(bf16))
        hb = jnp.concatenate(acts, axis=1)
        wd_g = wd_ref[pl.ds(gi * EPG, EPG)].reshape(EPG * D_EXPERT, D)
        return _dot(hb, wd_g).astype(bf16)

    segments_fit = cnt_s[N_GROUPS + 1] > 0

    for k_gs in range(N_GROUPS // GPS):
        @pl.when(jnp.logical_and(segments_fit, gs == k_gs))
        def _from_windows(k_gs=k_gs):
            for gi in range(GPS):
                first_row = (k_gs * GPS + gi) * CHUNK_PAD
                ys_all[pl.ds(first_row, CHUNK), :] = sorted_experts(gi, 0, k_gs)

    @pl.when(jnp.logical_not(segments_fit))
    def _from_whole_tile():
        for gi in range(GPS):
            first_row = pl.multiple_of((gs * GPS + gi) * CHUNK_PAD, CHUNK_PAD)
            ys_all[pl.ds(first_row, CHUNK), :] = sorted_experts(gi, 0)

    @pl.when(gs == 0)
    def _zero_pad_rows():
        for k in range(N_GROUPS):
            ys_all[pl.ds(k * CHUNK_PAD + CHUNK, CHUNK_PAD - CHUNK), :] = jnp.zeros((CHUNK_PAD - CHUNK, D), bf16)

    has_overflow = cnt_s[N_GROUPS] > 0

    @pl.when(jnp.logical_and(gs == 0, has_overflow))
    def _zero_acc():
        yacc[...] = jnp.zeros((tm, D), f32)

    for gi in range(GPS):
        g = gs * GPS + gi

        def overflow(c, carry, gi=gi, g=g):
            base = c * CHUNK
            ys = sorted_experts(gi, base)
            rec = tok_t[...]
            gsel_col = rec[:, ROW_GSEL:ROW_GSEL + 1]
            rank_col = rec[:, ROW_RANK:ROW_RANK + 1]
            jl = (lax.broadcasted_iota(i32, (tm, CHUNK), 1) + base).astype(f32)
            pt = jnp.where(jnp.logical_and(gsel_col == g.astype(f32), rank_col == jl), 1.0, 0.0).astype(bf16)
            yacc[...] += _dot(pt, ys)
            return carry

        lax.fori_loop(1, pl.cdiv(cnt_s[g], CHUNK), overflow, 0)

    def finish(with_acc):
        tqf = FIN_ROWS // nb
        jl = lax.broadcasted_iota(i32, (FIN_ROWS, N_GROUPS * CHUNK_PAD), 1).astype(f32)
        for h in range(n_half):
            g2 = mod_part(h, 5)
            for r in range(hm // FIN_ROWS):
                rows = pl.ds(h * hm + r * FIN_ROWS, FIN_ROWS)
                rec = tok_t[rows, :]
                gsel_col = rec[:, ROW_GSEL:ROW_GSEL + 1]
                rank_col = rec[:, ROW_RANK:ROW_RANK + 1]
                key = jnp.where(rank_col < f32(CHUNK), gsel_col * f32(CHUNK_PAD) + rank_col, -1.0)
                pt = jnp.where(key == jl, 1.0, 0.0).astype(bf16)
                y = _dot(pt, ys_all[...])
                if with_acc:
                    y = y + yacc[rows, :]
                v = ALPHA * x1_ref[rows, :] + ((1.0 + g2)[None] * y.reshape(tqf, nb, D)).reshape(FIN_ROWS, D)
                yn = _layer_norm(v, ln2g_ref[...], ln2b_ref[...])
                for tl in range(tqf):
                    for j in range(N_SLABS):
                        slab[j, pl.ds(r * tqf + tl, nb, stride=pitch), :] = (
                            yn[tl * nb:(tl + 1) * nb, j * LANES:(j + 1) * LANES])
            for b in range(nb):
                for j in range(N_SLABS):
                    y_ref[h * nb + b, :, pl.ds(j * LANES, LANES)] = slab[j, pl.ds(b * pitch, tb), :]

    @pl.when(jnp.logical_and(gs == last_gs, jnp.logical_not(has_overflow)))
    def _finish():
        finish(False)

    @pl.when(jnp.logical_and(gs == last_gs, has_overflow))
    def _finish_with_overflow():
        finish(True)


def _run_moe(x1, mod, mod_blk0, n_half, nb, tb, out_seqs, out_len, wts):
    n_tiles = x1.shape[0] // MOE_ROWS
    assert n_half * nb * tb == MOE_ROWS
    seq_per_tile = n_half * nb
    n_tt = out_len // tb
    pitch = _slab_pitch(tb)
    st = lambda q: (q // n_tt, q % n_tt)
    const2 = lambda q, g: (0, 0)
    grp3 = lambda q, g: (g, 0, 0)
    single = pl.Buffered(1)
    in_specs = [
        pl.BlockSpec((MOE_ROWS, D), lambda q, g: (q, 0)),
        pl.BlockSpec((seq_per_tile, 6 * D), lambda q, g: (mod_blk0 + st(q)[0], 0)),
        pl.BlockSpec((ROUTE_ROWS, D), const2),
        pl.BlockSpec((ROUTE_ROWS, 1), const2),
        pl.BlockSpec((LANES, LANES), const2),
        pl.BlockSpec((GPS * EPG, D, D_EXPERT), grp3),
        pl.BlockSpec((GPS * EPG, D, D_EXPERT), grp3),
        pl.BlockSpec((GPS * EPG, D_EXPERT, D), grp3),
        pl.BlockSpec((1, D), const2),
        pl.BlockSpec((1, D), const2),
    ]
    scratch = [
        pltpu.VMEM((MOE_ROWS, D), bf16),
        pltpu.VMEM((MOE_ROWS, LANES), f32),
        pltpu.VMEM((MOE_ROWS, LANES), bf16),
        pltpu.VMEM((MOE_ROWS, D), bf16),
        pltpu.VMEM((MOE_ROWS, LANES), bf16),
        pltpu.VMEM((3 * (MOE_ROWS // SUB), SUBLANES, LANES), f32),
        pltpu.VMEM((SUBLANES, MOE_ROWS), i32),
        pltpu.SMEM((N_GROUPS + 2,), i32),
        pltpu.VMEM((N_GROUPS * CHUNK_PAD, D), bf16),
        pltpu.VMEM((MOE_ROWS, D), f32),
        pltpu.VMEM((N_SLABS, nb * pitch, LANES), f32),
    ]
    body = functools.partial(_moe_body, n_half, nb, tb, pitch)
    return pl.pallas_call(
        body,
        grid=(n_tiles, N_GROUPS // GPS),
        in_specs=in_specs,
        out_specs=pl.BlockSpec((seq_per_tile, tb, D), lambda q, g: (*st(q), 0)),
        out_shape=jax.ShapeDtypeStruct((out_seqs, out_len, D), f32),
        scratch_shapes=scratch,
        compiler_params=pltpu.CompilerParams(
            dimension_semantics=("arbitrary", "arbitrary"),
            vmem_limit_bytes=VMEM_LIMIT),
        name=f"moe_nb{nb}_tb{tb}",
    )(x1, mod, *wts)


def _pair_blocks(w):
    n, k, _ = w.shape
    wp = w.reshape(n // 2, 2, k, k)
    eye = jnp.eye(2, dtype=w.dtype)
    return jnp.einsum('phij,hg->phigj', wp, eye).reshape(n // 2, 2 * k, 2 * k)


def kernel(x_prompt, x_sample, c_prompt, c_sample, state_pool, state_conv, state_lru, w_ada, b_ada, w_in, w_pool, pool_scale, w_conv, b_conv, w_a, b_a, w_x, b_x, lru_lambda, w_out, ln1_g, ln1_b, w_group, b_group, w_route, b_route, w_gate, w_up, w_down, ln2_g, ln2_b):
    l = 0
    bp, tp, _ = x_prompt.shape
    bs, ts, _ = x_sample.shape
    mod, w_in_b, w_out_b = _ada_mod(
        jnp.concatenate([c_sample, c_prompt], axis=0), w_ada[l], b_ada[l], w_in[l], w_out[l])

    w_rt = jnp.zeros((ROUTE_ROWS, D), f32).at[0:N_GROUPS].set(w_group[l].T)
    b_rt = jnp.zeros((ROUTE_ROWS,), f32).at[0:N_GROUPS].set(b_group[l])
    for k in range(N_GROUPS):
        r0 = SUBLANES * (k + 1)
        w_rt = w_rt.at[r0:r0 + EPG].set(w_route[l][:, k * EPG:(k + 1) * EPG].T)
        b_rt = b_rt.at[r0:r0 + EPG].set(b_route[l][k * EPG:(k + 1) * EPG])
    tri = jnp.triu(jnp.ones((LANES, LANES), bf16), 1)

    mix_wts = (
        w_in_b,
        w_pool[l].astype(bf16),
        pool_scale[l].reshape(1, D_POOL),
        w_conv[l],
        b_conv[l].reshape(1, D_LRU),
        jnp.stack([_pair_blocks(w_a[l]), _pair_blocks(w_x[l])]).astype(bf16),
        jnp.concatenate([b_a[l], b_x[l]]).reshape(1, 2 * D_LRU),
        lru_lambda[l].reshape(1, D_LRU),
        w_out_b,
        ln1_g[l].reshape(1, D),
        ln1_b[l].reshape(1, D),
    )
    tb_p = MOE_ROWS // bp
    zp = jnp.zeros((POOL_BUF, bp, D_POOL), f32)
    zc = jnp.zeros((CONV_W - 1, bp, D_LRU), f32)
    zh = jnp.zeros((bp, D_LRU), f32)
    x1p, pool_p, conv_p, lru_p, wg_b, wu_b, wd_b = _run_mix(
        x_prompt, mod, bs // bp, zp, zc, zh, 0, bp, tb_p, mix_wts, (w_gate[l], w_up[l], w_down[l]))
    moe_wts = (
        w_rt.astype(bf16),
        b_rt.reshape(ROUTE_ROWS, 1),
        tri,
        wg_b,
        wu_b,
        wd_b,
        ln2_g[l].reshape(1, D),
        ln2_b[l].reshape(1, D),
    )
    yp = _run_moe(x1p, mod, bs // bp, 1, bp, tb_p, bp, tp, moe_wts)

    n_prev_s = min(PAST_LEN, POOL_BUF)
    nb_s = bs // 2
    x1s, pool_s, conv_s, lru_s = _run_mix(
        x_sample, mod, 0, state_pool[l].transpose(1, 0, 2), state_conv[l].transpose(1, 0, 2),
        state_lru[l], n_prev_s, nb_s, ts, mix_wts)
    ys = _run_moe(x1s, mod, 0, 2, nb_s, ts, bs, ts, moe_wts)

    tr = lambda a: a.transpose(1, 0, 2)[None]
    return (yp, ys, tr(pool_p), tr(conv_p), lru_p[None], tr(pool_s), tr(conv_s), lru_s[None])
```

```python
import functools
import math

import jax
import jax.numpy as jnp
from jax import lax
from jax.experimental import pallas as pl
from jax.experimental.pallas import tpu as pltpu

D = 1024
D_POOL = 512
D_LRU = 512
D_IN = D_POOL + 2 * D_LRU
POOL_WINDOWS = (2, 4, 8, 16)
POOL_GROUP = 128
POOL_BUF = 15
CONV_W = 4
LRU_C = 8.0
N_GROUPS = 4
EPG = 4
D_EXPERT = 256
DEPTH = 1
ALPHA = (2.0 * DEPTH) ** 0.25
LN_EPS = 1e-5
PAST_LEN = 16384

LANES = 128
SUBLANES = 8
MXU_DIM = 256
N_SLABS = D // LANES
ROUTE_ROWS = 48
MOE_ROWS = 1024
CHUNK = 288
CHUNK_PAD = 320
assert CHUNK <= CHUNK_PAD and (N_GROUPS * CHUNK_PAD) % MXU_DIM == 0
ROW_LO = 4
ROW_LO2 = 8
ROW_GSEL = 12
ROW_RANK = 13
REC_ROWS = 16
N_RB = 4
MIX_CHAIN_ROWS = 512
GPS_PROMPT = 4
GPS_SAMPLE = 2
FIN_ROWS = 256
ADA_STEPS = 8
V7X_VMEM_BYTES = 64 * 1024 * 1024
VMEM_RESERVE_BYTES = 4 * 1024 * 1024
VMEM_LIMIT = V7X_VMEM_BYTES - VMEM_RESERVE_BYTES
LOG2E = 1.0 / math.log(2.0)
SQRT_GUARD = 1e-30

f32 = jnp.float32
bf16 = jnp.bfloat16
i32 = jnp.int32


def _dot(a, b):
    return jnp.dot(a, b, preferred_element_type=f32)


def _sigmoid(x):
    return 1.0 / (1.0 + jnp.exp2(x * (-LOG2E)))


def _gelu_tanh(x):
    k1 = -2.0 * math.sqrt(2.0 / math.pi) * LOG2E
    return x / (1.0 + jnp.exp2(x * (k1 + (k1 * 0.044715) * (x * x))))


def _layer_norm(v, g, b):
    mu = jnp.mean(v, axis=-1, keepdims=True)
    c = v - mu
    var = jnp.mean(c * c, axis=-1, keepdims=True)
    return c * lax.rsqrt(var + LN_EPS) * g + b


def _slab_pitch(tb):
    return tb + SUBLANES if (tb // SUBLANES) % 2 == 0 else tb


def _ada_body(c_ref, w_ref, b_ref, w_in_ref, w_out_ref, o_ref, w_in_b_ref, w_out_b_ref):
    o_ref[...] = _dot(c_ref[...].astype(bf16), w_ref[...].astype(bf16)) + b_ref[...]
    w_in_b_ref[...] = w_in_ref[...].astype(bf16)
    w_out_b_ref[...] = w_out_ref[...].astype(bf16)


def _ada_mod(c_all, w_ada, b_ada, w_in, w_out):
    n = c_all.shape[0]
    steps = ADA_STEPS
    bn = 6 * D // steps
    br = D // steps
    return pl.pallas_call(
        _ada_body,
        grid=(steps,),
        in_specs=[
            pl.BlockSpec((n, D), lambda j: (0, 0)),
            pl.BlockSpec((D, bn), lambda j: (0, j)),
            pl.BlockSpec((1, bn), lambda j: (0, j)),
            pl.BlockSpec((br, D_IN), lambda j: (j, 0)),
            pl.BlockSpec((br, D), lambda j: (j, 0)),
        ],
        out_specs=[
            pl.BlockSpec((n, bn), lambda j: (0, j)),
            pl.BlockSpec((br, D_IN), lambda j: (j, 0)),
            pl.BlockSpec((br, D), lambda j: (j, 0)),
        ],
        out_shape=[
            jax.ShapeDtypeStruct((n, 6 * D), f32),
            jax.ShapeDtypeStruct((D, D_IN), bf16),
            jax.ShapeDtypeStruct((D, D), bf16),
        ],
        name="ada_mod",
    )(c_all, w_ada, b_ada.reshape(1, 6 * D), w_in, w_out)


def _mix_body(nb, tb, pitch, n_prev, n_tt, n_cast,
              x_ref, mod_ref, pool0_ref, conv0_ref, h0_ref,
              w_in_ref, w_pool_ref, pscale_ref, w_conv_ref, b_conv_ref,
              w_ax_ref, b_ax_ref, lam_ref, w_out_ref, ln1g_ref, ln1b_ref,
              *rest):
    cast_in = rest[:n_cast]
    x1_ref, npool_ref, nconv_ref, nh_ref = rest[n_cast:n_cast + 4]
    cast_out = rest[n_cast + 4:2 * n_cast + 4]
    slab, xt, ycat, zpool, zconv, a_s, b_s, gl_s, h_s = rest[2 * n_cast + 4:]
    for src, dst in zip(cast_in, cast_out):
        dst[...] = src[...].astype(bf16)
    tm = nb * tb
    tq = tb // N_RB
    rq = tq * nb
    ti = pl.program_id(0) % n_tt

    def mod_part(k):
        return mod_ref[:, pl.ds(k * D, D)]

    @pl.when(ti == 0)
    def _init_state():
        zpool[pl.ds(0, POOL_BUF * nb), :] = pool0_ref[...].reshape(POOL_BUF * nb, D_POOL)
        zconv[pl.ds(0, (CONV_W - 1) * nb), :] = conv0_ref[...].reshape((CONV_W - 1) * nb, D_LRU)
        h_s[...] = h0_ref[...]

    for b in range(nb):
        for j in range(N_SLABS):
            slab[j, pl.ds(b * pitch, tb), :] = x_ref[b, :, pl.ds(j * LANES, LANES)]

    sh1 = mod_part(0)
    sc1 = mod_part(1)
    g1 = mod_part(2)
    nl = -lam_ref[...]
    softplus = jnp.maximum(nl, 0.0) + jnp.log(1.0 + jnp.exp(-jnp.abs(nl)))
    log2a_unit = (-LRU_C * LOG2E) * softplus
    n_blk = 1
    tbb = tb // n_blk
    rb = tbb * nb
    tqb = tbb // N_RB
    rqb = tqb * nb
    row = lax.broadcasted_iota(i32, (rb, LANES), 0)
    t_loc = lax.shift_right_logical(row, int(math.log2(nb)))

    def project(blk):
        t0, r0 = blk * tbb, blk * rb
        brows = pl.ds(r0, rb)
        for t in range(t0, t0 + tbb):
            for j in range(N_SLABS):
                xt[t, :, pl.ds(j * LANES, LANES)] = slab[j, pl.ds(t, nb, stride=pitch), :]
        for r in range(N_RB):
            u = xt[pl.ds(t0 + r * tqb, tqb)] * (1.0 + sc1)[None] + sh1[None]
            ycat[pl.ds(r0 + r * rqb, rqb), :] = u.reshape(rqb, D).astype(bf16)
        ub = ycat[brows, :]
        zpool[pl.ds(POOL_BUF * nb + r0, rb), :] = _dot(ub, w_in_ref[:, pl.ds(0, D_POOL)])
        zconv[pl.ds((CONV_W - 1) * nb + r0, rb), :] = _dot(ub, w_in_ref[:, pl.ds(D_POOL, D_LRU)])
        half = D_LRU // 2
        for hpart in range(2):
            cols = pl.ds(D_POOL + D_LRU + hpart * half, half)
            gl_s[brows, pl.ds(hpart * half, half)] = _gelu_tanh(_dot(ub, w_in_ref[:, cols]))

    def mix_tokens(blk, h):
        t0, r0 = blk * tbb, blk * rb
        brows = pl.ds(r0, rb)
        t_glob = (ti * tb + t0 + t_loc + (1 + n_prev)).astype(f32)
        for c, w in enumerate(POOL_WINDOWS):
            lanes = pl.ds(c * LANES, LANES)
            s = zpool[pl.ds((POOL_BUF + 1 - w) * nb + r0, rb + (w - 1) * nb), lanes]
            step = 1
            while step < w:
                s = s[step * nb:] + s[:-step * nb]
                step *= 2
            cnt = jnp.minimum(f32(w), t_glob)
            dlt = s / cnt - zpool[pl.ds(POOL_BUF * nb + r0, rb), lanes]
            yp = _dot(dlt.astype(bf16), w_pool_ref[c]) * pscale_ref[:, lanes]
            ycat[brows, lanes] = yp.astype(bf16)
        for j in range(D_LRU // LANES):
            lanes = pl.ds(j * LANES, LANES)
            xc = b_conv_ref[:, lanes] + zconv[pl.ds(r0, rb), lanes] * w_conv_ref[0:1, lanes]
            for k in range(1, CONV_W):
                xc = xc + zconv[pl.ds(k * nb + r0, rb), lanes] * w_conv_ref[k:k + 1, lanes]
            xcb = xc.astype(bf16)
            r = _sigmoid(_dot(xcb, w_ax_ref[0, j]) + b_ax_ref[:, lanes])
            ig = _sigmoid(_dot(xcb, w_ax_ref[1, j]) + b_ax_ref[:, pl.ds(D_LRU + j * LANES, LANES)])
            a = jnp.exp2(r * log2a_unit[:, j * LANES:(j + 1) * LANES])
            a_s[brows, lanes] = a
            om = jnp.maximum(1.0 - a * a, 0.0)
            mult = om * lax.rsqrt(jnp.maximum(om, SQRT_GUARD))
            b_s[brows, lanes] = mult * ig * xc
        for t in range(t0, t0 + tbb):
            rows = pl.ds(t * nb, nb)
            h = a_s[rows, :] * h + b_s[rows, :]
            b_s[rows, :] = h
        ycat[brows, pl.ds(D_POOL, D_LRU)] = (b_s[brows, :] * gl_s[brows, :]).astype(bf16)
        yc = ycat[brows, :]
        cb = D // N_RB
        for j in range(N_RB):
            cols = pl.ds(j * cb, cb)
            mix = _dot(yc, w_out_ref[:, cols]).reshape(tbb, nb, cb)
            xt[pl.ds(t0, tbb), :, cols] = (ALPHA * xt[pl.ds(t0, tbb), :, cols]
                                           + (1.0 + g1[:, j * cb:(j + 1) * cb])[None] * mix)
        for r in range(N_RB):
            xn = _layer_norm(xt[pl.ds(t0 + r * tqb, tqb)], ln1g_ref[...][None], ln1b_ref[...][None])
            x1_ref[pl.ds(r0 + r * rqb, rqb), :] = xn.reshape(rqb, D)
        return h

    project(0)
    h = mix_tokens(0, h_s[...])


    h_s[...] = h
    zpool[pl.ds(0, POOL_BUF * nb), :] = zpool[pl.ds(tm, POOL_BUF * nb), :]
    zconv[pl.ds(0, (CONV_W - 1) * nb), :] = zconv[pl.ds(tm, (CONV_W - 1) * nb), :]

    @pl.when(ti == n_tt - 1)
    def _emit_state():
        npool_ref[...] = zpool[pl.ds(0, POOL_BUF * nb), :].reshape(POOL_BUF, nb, D_POOL)
        nconv_ref[...] = zconv[pl.ds(0, (CONV_W - 1) * nb), :].reshape(CONV_W - 1, nb, D_LRU)
        nh_ref[...] = h_s[...]


def _run_mix(x, mod, mod_blk0, pool0, conv0, h0, n_prev, nb, tb, wts, to_cast=()):
    bsz, t_len, _ = x.shape
    n_sb = bsz // nb
    n_tt = t_len // tb
    tm = nb * tb
    pitch = _slab_pitch(tb)
    st = lambda q: (q // n_tt, q % n_tt)
    const2 = lambda q: (0, 0)
    const3 = lambda q: (0, 0, 0)
    const4 = lambda q: (0, 0, 0, 0)
    single = pl.Buffered(1)
    in_specs = [
        pl.BlockSpec((nb, tb, D), lambda q: (*st(q), 0)),
        pl.BlockSpec((nb, 6 * D), lambda q: (mod_blk0 + st(q)[0], 0)),
        pl.BlockSpec((POOL_BUF, nb, D_POOL), lambda q: (0, st(q)[0], 0)),
        pl.BlockSpec((CONV_W - 1, nb, D_LRU), lambda q: (0, st(q)[0], 0)),
        pl.BlockSpec((nb, D_LRU), lambda q: (st(q)[0], 0)),
        pl.BlockSpec((D, D_IN), const2, pipeline_mode=single),
        pl.BlockSpec((4, POOL_GROUP, POOL_GROUP), const3),
        pl.BlockSpec((1, D_POOL), const2),
        pl.BlockSpec((CONV_W, D_LRU), const2),
        pl.BlockSpec((1, D_LRU), const2),
        pl.BlockSpec((2, D_LRU // LANES, LANES, LANES), const4),
        pl.BlockSpec((1, 2 * D_LRU), const2),
        pl.BlockSpec((1, D_LRU), const2),
        pl.BlockSpec((D, D), const2, pipeline_mode=single),
        pl.BlockSpec((1, D), const2),
        pl.BlockSpec((1, D), const2),
    ]
    out_specs = [
        pl.BlockSpec((tm, D), lambda q: (q, 0)),
        pl.BlockSpec((POOL_BUF, nb, D_POOL), lambda q: (0, st(q)[0], 0)),
        pl.BlockSpec((CONV_W - 1, nb, D_LRU), lambda q: (0, st(q)[0], 0)),
        pl.BlockSpec((nb, D_LRU), lambda q: (st(q)[0], 0)),
    ]
    out_shape = [
        jax.ShapeDtypeStruct((bsz * t_len, D), f32),
        jax.ShapeDtypeStruct((POOL_BUF, bsz, D_POOL), f32),
        jax.ShapeDtypeStruct((CONV_W - 1, bsz, D_LRU), f32),
        jax.ShapeDtypeStruct((bsz, D_LRU), f32),
    ]
    for w in to_cast:
        assert w.shape[0] == n_sb * n_tt, (w.shape, n_sb * n_tt)
        blk = (1,) + w.shape[1:]
        in_specs.append(pl.BlockSpec(blk, lambda q: (q, 0, 0)))
        out_specs.append(pl.BlockSpec(blk, lambda q: (q, 0, 0)))
        out_shape.append(jax.ShapeDtypeStruct(w.shape, bf16))
    scratch = [
        pltpu.VMEM((N_SLABS, nb * pitch, LANES), f32),
        pltpu.VMEM((tb, nb, D), f32),
        pltpu.VMEM((tm, D), bf16),
        pltpu.VMEM(((tb + POOL_BUF) * nb, D_POOL), f32),
        pltpu.VMEM(((tb + CONV_W - 1) * nb, D_LRU), f32),
        pltpu.VMEM((tm, D_LRU), f32),
        pltpu.VMEM((tm, D_LRU), f32),
        pltpu.VMEM((tm, D_LRU), f32),
        pltpu.VMEM((nb, D_LRU), f32),
    ]
    body = functools.partial(_mix_body, nb, tb, pitch, n_prev, n_tt, len(to_cast))
    return pl.pallas_call(
        body,
        grid=(n_sb * n_tt,),
        in_specs=in_specs,
        out_specs=out_specs,
        out_shape=out_shape,
        scratch_shapes=scratch,
        compiler_params=pltpu.CompilerParams(
            dimension_semantics=("arbitrary",),
            vmem_limit_bytes=VMEM_LIMIT),
        name=f"mix_nb{nb}_tb{tb}",
    )(x, mod, pool0, conv0, h0, *wts, *to_cast)


def _moe_body(n_half, nb, tb, pitch, GPS,
              x1_ref, mod_ref, w_rt_ref, b_rt_ref, tri_ref, wg_ref, wu_ref, wd_ref, ln2g_ref, ln2b_ref,
              y_ref,
              u2, tok_t, tok3, rows_s, cnt_s, ys_all, yacc, slab):
    tm = MOE_ROWS
    hm = nb * tb
    tq = tb // N_RB
    rq = tq * nb
    gs = pl.program_id(1)
    last_gs = N_GROUPS // GPS - 1

    def mod_part(h, k):
        return mod_ref[pl.ds(h * nb, nb), pl.ds(k * D, D)]

    @pl.when(gs == 0)
    def _route():
        for h in range(n_half):
            sh2 = mod_part(h, 3)
            sc2 = mod_part(h, 4)
            for r in range(N_RB):
                rows = pl.ds(h * hm + r * rq, rq)
                v = x1_ref[rows, :].reshape(tq, nb, D) * (1.0 + sc2)[None] + sh2[None]
                u2[rows, :] = v.reshape(rq, D).astype(bf16)
        lt = lax.dot_general(w_rt_ref[...], u2[...], (((1,), (1,)), ((), ())),
                             preferred_element_type=f32) + b_rt_ref[...]
        gl = [lt[k:k + 1, :] for k in range(N_GROUPS)]
        best = gl[0]
        gsel = jnp.zeros_like(best, dtype=i32)
        for k in range(1, N_GROUPS):
            better = gl[k] > best
            best = jnp.where(better, gl[k], best)
            gsel = jnp.where(better, k, gsel)
        denom = jnp.exp(gl[0] - best)
        for k in range(1, N_GROUPS):
            denom = denom + jnp.exp(gl[k] - best)
        p_sel = 1.0 / denom
        es = []
        for j in range(EPG):
            v_j = lt[SUBLANES + j:SUBLANES + j + 1, :]
            for k in range(1, N_GROUPS):
                r0 = SUBLANES * (k + 1) + j
                v_j = jnp.where(gsel == k, lt[r0:r0 + 1, :], v_j)
            es.append(v_j)
        v1 = es[0]
        i1 = jnp.zeros_like(gsel)
        for j in range(1, EPG):
            better = es[j] > v1
            v1 = jnp.where(better, es[j], v1)
            i1 = jnp.where(better, j, i1)
        v2 = jnp.full_like(v1, -jnp.inf)
        i2 = jnp.full_like(gsel, -1)
        for j in range(EPG):
            cand = jnp.logical_and(i1 != j, jnp.logical_or(i2 < 0, es[j] > v2))
            v2 = jnp.where(cand, es[j], v2)
            i2 = jnp.where(cand, j, i2)
        e21 = jnp.exp(v2 - v1)
        w1 = p_sel / (1.0 + e21)
        w2 = p_sel * e21 / (1.0 + e21)
        rid8 = lax.broadcasted_iota(i32, (SUBLANES, tm), 0)
        onehot = jnp.where(rid8 == gsel, 1.0, 0.0)
        blocks = [onehot[:, j * LANES:(j + 1) * LANES] for j in range(tm // LANES)]
        inner = _dot(jnp.concatenate(blocks, axis=0).astype(bf16), tri_ref[...])
        before = jnp.zeros((SUBLANES, 1), f32)
        prefs = []
        for j, blk in enumerate(blocks):
            prefs.append(inner[j * SUBLANES:(j + 1) * SUBLANES, :] + before)
            before = before + jnp.sum(blk, axis=1, keepdims=True)
        pref = jnp.concatenate(prefs, axis=1)
        rank = jnp.sum(onehot * pref, axis=0, keepdims=True)
        n_max = 0
        for k in range(N_GROUPS):
            n_k = jnp.sum(jnp.where(gsel == k, 1.0, 0.0)).astype(i32)
            cnt_s[k] = n_k
            n_max = jnp.maximum(n_max, n_k)
        cnt_s[N_GROUPS] = (n_max > CHUNK).astype(i32)
        rows_s[0:1, :] = gsel
        rows_s[1:2, :] = rank.astype(i32)
        rid = lax.broadcasted_iota(i32, (REC_ROWS, tm), 0)
        rec = jnp.zeros((REC_ROWS, tm), f32)
        for j in range(EPG):
            wj = jnp.where(i1 == j, w1, jnp.where(i2 == j, w2, 0.0))
            hi = wj.astype(bf16).astype(f32)
            mid = (wj - hi).astype(bf16).astype(f32)
            lo = (wj - hi - mid).astype(bf16).astype(f32)
            rec = jnp.where(rid == j, hi, rec)
            rec = jnp.where(rid == ROW_LO + j, mid, rec)
            rec = jnp.where(rid == ROW_LO2 + j, lo, rec)
        rec = jnp.where(rid == ROW_GSEL, gsel.astype(f32), rec)
        rec = jnp.where(rid == ROW_RANK, rank, rec)
        rec = jnp.concatenate([rec, jnp.zeros((LANES - REC_ROWS, tm), f32)], axis=0)
        rt = rec.T
        tok_t[...] = rt
        tok3[...] = rt.astype(bf16)

    def sorted_experts(gi, base):
        g = gs * GPS + gi
        gsel_row = rows_s[0:1, :]
        rank_row = rows_s[1:2, :]
        jj = lax.broadcasted_iota(i32, (CHUNK, tm), 0) + base
        pm = jnp.where(jnp.logical_and(gsel_row == g, rank_row == jj), 1.0, 0.0).astype(bf16)
        xs = _dot(pm, u2[...]).astype(bf16)
        cs3 = _dot(pm, tok3[...])
        cs = (cs3 + pltpu.roll(cs3, LANES - ROW_LO, axis=1)
              + pltpu.roll(cs3, LANES - ROW_LO2, axis=1))
        acts = []
        for e in range(EPG):
            hg = _dot(xs, wg_ref[gi * EPG + e])
            hu = _dot(xs, wu_ref[gi * EPG + e])
            acts.append((hg * _sigmoid(hg) * hu * cs[:, e:e + 1]).astype(bf16))
        hb = jnp.concatenate(acts, axis=1)
        wd_g = wd_ref[pl.ds(gi * EPG, EPG)].reshape(EPG * D_EXPERT, D)
        return _dot(hb, wd_g).astype(bf16)

    for gi in range(GPS):
        first_row = pl.multiple_of((gs * GPS + gi) * CHUNK_PAD, CHUNK_PAD)
        ys_all[pl.ds(first_row, CHUNK), :] = sorted_experts(gi, 0)

    @pl.when(gs == 0)
    def _zero_pad_rows():
        for k in range(N_GROUPS):
            ys_all[pl.ds(k * CHUNK_PAD + CHUNK, CHUNK_PAD - CHUNK), :] = jnp.zeros((CHUNK_PAD - CHUNK, D), bf16)

    has_overflow = cnt_s[N_GROUPS] > 0

    @pl.when(jnp.logical_and(gs == 0, has_overflow))
    def _zero_acc():
        yacc[...] = jnp.zeros((tm, D), f32)

    for gi in range(GPS):
        g = gs * GPS + gi

        def overflow(c, carry, gi=gi, g=g):
            base = c * CHUNK
            ys = sorted_experts(gi, base)
            rec = tok_t[...]
            gsel_col = rec[:, ROW_GSEL:ROW_GSEL + 1]
            rank_col = rec[:, ROW_RANK:ROW_RANK + 1]
            jl = (lax.broadcasted_iota(i32, (tm, CHUNK), 1) + base).astype(f32)
            pt = jnp.where(jnp.logical_and(gsel_col == g.astype(f32), rank_col == jl), 1.0, 0.0).astype(bf16)
            yacc[...] += _dot(pt, ys)
            return carry

        lax.fori_loop(1, pl.cdiv(cnt_s[g], CHUNK), overflow, 0)

    def finish(with_acc):
        tqf = FIN_ROWS // nb
        jl = lax.broadcasted_iota(i32, (FIN_ROWS, N_GROUPS * CHUNK_PAD), 1).astype(f32)
        for h in range(n_half):
            g2 = mod_part(h, 5)
            for r in range(hm // FIN_ROWS):
                rows = pl.ds(h * hm + r * FIN_ROWS, FIN_ROWS)
                rec = tok_t[rows, :]
                gsel_col = rec[:, ROW_GSEL:ROW_GSEL + 1]
                rank_col = rec[:, ROW_RANK:ROW_RANK + 1]
                key = jnp.where(rank_col < f32(CHUNK), gsel_col * f32(CHUNK_PAD) + rank_col, -1.0)
                pt = jnp.where(key == jl, 1.0, 0.0).astype(bf16)
                y = _dot(pt, ys_all[...])
                if with_acc:
                    y = y + yacc[rows, :]
                v = ALPHA * x1_ref[rows, :] + ((1.0 + g2)[None] * y.reshape(tqf, nb, D)).reshape(FIN_ROWS, D)
                yn = _layer_norm(v, ln2g_ref[...], ln2b_ref[...])
                for tl in range(tqf):
                    for j in range(N_SLABS):
                        slab[j, pl.ds(r * tqf + tl, nb, stride=pitch), :] = (
                            yn[tl * nb:(tl + 1) * nb, j * LANES:(j + 1) * LANES])
            for b in range(nb):
                for j in range(N_SLABS):
                    y_ref[h * nb + b, :, pl.ds(j * LANES, LANES)] = slab[j, pl.ds(b * pitch, tb), :]

    @pl.when(jnp.logical_and(gs == last_gs, jnp.logical_not(has_overflow)))
    def _finish():
        finish(False)

    @pl.when(jnp.logical_and(gs == last_gs, has_overflow))
    def _finish_with_overflow():
        finish(True)


def _run_moe(x1, mod, mod_blk0, n_half, nb, tb, out_seqs, out_len, GPS, wts):
    n_tiles = x1.shape[0] // MOE_ROWS
    assert n_half * nb * tb == MOE_ROWS
    seq_per_tile = n_half * nb
    n_tt = out_len // tb
    pitch = _slab_pitch(tb)
    st = lambda q: (q // n_tt, q % n_tt)
    const2 = lambda q, g: (0, 0)
    grp3 = lambda q, g: (g, 0, 0)
    wmode = pl.Buffered(1) if GPS == N_GROUPS else None
    in_specs = [
        pl.BlockSpec((MOE_ROWS, D), lambda q, g: (q, 0)),
        pl.BlockSpec((seq_per_tile, 6 * D), lambda q, g: (mod_blk0 + st(q)[0], 0)),
        pl.BlockSpec((ROUTE_ROWS, D), const2),
        pl.BlockSpec((ROUTE_ROWS, 1), const2),
        pl.BlockSpec((LANES, LANES), const2),
        pl.BlockSpec((GPS * EPG, D, D_EXPERT), grp3, pipeline_mode=wmode),
        pl.BlockSpec((GPS * EPG, D, D_EXPERT), grp3, pipeline_mode=wmode),
        pl.BlockSpec((GPS * EPG, D_EXPERT, D), grp3, pipeline_mode=wmode),
        pl.BlockSpec((1, D), const2),
        pl.BlockSpec((1, D), const2),
    ]
    scratch = [
        pltpu.VMEM((MOE_ROWS, D), bf16),
        pltpu.VMEM((MOE_ROWS, LANES), f32),
        pltpu.VMEM((MOE_ROWS, LANES), bf16),
        pltpu.VMEM((SUBLANES, MOE_ROWS), i32),
        pltpu.SMEM((N_GROUPS + 1,), i32),
        pltpu.VMEM((N_GROUPS * CHUNK_PAD, D), bf16),
        pltpu.VMEM((MOE_ROWS, D), f32),
        pltpu.VMEM((N_SLABS, nb * pitch, LANES), f32),
    ]
    body = functools.partial(_moe_body, n_half, nb, tb, pitch, GPS)
    return pl.pallas_call(
        body,
        grid=(n_tiles, N_GROUPS // GPS),
        in_specs=in_specs,
        out_specs=pl.BlockSpec((seq_per_tile, tb, D), lambda q, g: (*st(q), 0)),
        out_shape=jax.ShapeDtypeStruct((out_seqs, out_len, D), f32),
        scratch_shapes=scratch,
        compiler_params=pltpu.CompilerParams(
            dimension_semantics=("arbitrary", "arbitrary"),
            vmem_limit_bytes=VMEM_LIMIT),
        name=f"moe_nb{nb}_tb{tb}",
    )(x1, mod, *wts)


def _pair_blocks(w):
    n, k, _ = w.shape
    wp = w.reshape(n // 2, 2, k, k)
    eye = jnp.eye(2, dtype=w.dtype)
    return jnp.einsum('phij,hg->phigj', wp, eye).reshape(n // 2, 2 * k, 2 * k)


def kernel(x_prompt, x_sample, c_prompt, c_sample, state_pool, state_conv, state_lru, w_ada, b_ada, w_in, w_pool, pool_scale, w_conv, b_conv, w_a, b_a, w_x, b_x, lru_lambda, w_out, ln1_g, ln1_b, w_group, b_group, w_route, b_route, w_gate, w_up, w_down, ln2_g, ln2_b):
    l = 0
    bp, tp, _ = x_prompt.shape
    bs, ts, _ = x_sample.shape
    mod, w_in_b, w_out_b = _ada_mod(
        jnp.concatenate([c_sample, c_prompt], axis=0), w_ada[l], b_ada[l], w_in[l], w_out[l])

    w_rt = jnp.zeros((ROUTE_ROWS, D), f32).at[0:N_GROUPS].set(w_group[l].T)
    b_rt = jnp.zeros((ROUTE_ROWS,), f32).at[0:N_GROUPS].set(b_group[l])
    for k in range(N_GROUPS):
        r0 = SUBLANES * (k + 1)
        w_rt = w_rt.at[r0:r0 + EPG].set(w_route[l][:, k * EPG:(k + 1) * EPG].T)
        b_rt = b_rt.at[r0:r0 + EPG].set(b_route[l][k * EPG:(k + 1) * EPG])
    tri = jnp.triu(jnp.ones((LANES, LANES), bf16), 1)

    mix_wts = (
        w_in_b,
        w_pool[l].astype(bf16),
        pool_scale[l].reshape(1, D_POOL),
        w_conv[l],
        b_conv[l].reshape(1, D_LRU),
        jnp.stack([_pair_blocks(w_a[l]), _pair_blocks(w_x[l])]).astype(bf16),
        jnp.concatenate([b_a[l], b_x[l]]).reshape(1, 2 * D_LRU),
        lru_lambda[l].reshape(1, D_LRU),
        w_out_b,
        ln1_g[l].reshape(1, D),
        ln1_b[l].reshape(1, D),
    )
    tb_p = MOE_ROWS // bp
    zp = jnp.zeros((POOL_BUF, bp, D_POOL), f32)
    zc = jnp.zeros((CONV_W - 1, bp, D_LRU), f32)
    zh = jnp.zeros((bp, D_LRU), f32)
    x1p, pool_p, conv_p, lru_p, wg_b, wu_b, wd_b = _run_mix(
        x_prompt, mod, bs // bp, zp, zc, zh, 0, bp, tb_p, mix_wts, (w_gate[l], w_up[l], w_down[l]))
    moe_wts = (
        w_rt.astype(bf16),
        b_rt.reshape(ROUTE_ROWS, 1),
        tri,
        wg_b,
        wu_b,
        wd_b,
        ln2_g[l].reshape(1, D),
        ln2_b[l].reshape(1, D),
    )
    yp = _run_moe(x1p, mod, bs // bp, 1, bp, tb_p, bp, tp, GPS_PROMPT, moe_wts)

    n_prev_s = min(PAST_LEN, POOL_BUF)
    nb_s = bs // 2
    x1s, pool_s, conv_s, lru_s = _run_mix(
        x_sample, mod, 0, state_pool[l].transpose(1, 0, 2), state_conv[l].transpose(1, 0, 2),
        state_lru[l], n_prev_s, nb_s, ts, mix_wts)
    ys = _run_moe(x1s, mod, 0, 2, nb_s, ts, bs, ts, GPS_SAMPLE, moe_wts)

    tr = lambda a: a.transpose(1, 0, 2)[None]
    return (yp, ys, tr(pool_p), tr(conv_p), lru_p[None], tr(pool_s), tr(conv_s), lru_s[None])
```

```python
import functools
import math

import jax
import jax.numpy as jnp
from jax import lax
from jax.experimental import pallas as pl
from jax.experimental.pallas import tpu as pltpu

D = 1024
D_POOL = 512
D_LRU = 512
D_IN = D_POOL + 2 * D_LRU
POOL_WINDOWS = (2, 4, 8, 16)
POOL_GROUP = 128
POOL_BUF = 15
CONV_W = 4
LRU_C = 8.0
N_GROUPS = 4
EPG = 4
D_EXPERT = 256
DEPTH = 1
ALPHA = (2.0 * DEPTH) ** 0.25
LN_EPS = 1e-5
PAST_LEN = 16384

LANES = 128
SUBLANES = 8
MXU_DIM = 256
N_SLABS = D // LANES
ROUTE_ROWS = 48
MOE_ROWS = 1024
CHUNK = 288
CHUNK_PAD = 320
assert CHUNK <= CHUNK_PAD and (N_GROUPS * CHUNK_PAD) % MXU_DIM == 0
ROW_LO = 4
ROW_LO2 = 8
ROW_GSEL = 12
ROW_RANK = 13
REC_ROWS = 16
N_RB = 4
GPS = 2
FIN_ROWS = 256
ADA_STEPS = 8
V7X_VMEM_BYTES = 64 * 1024 * 1024
VMEM_RESERVE_BYTES = 4 * 1024 * 1024
VMEM_LIMIT = V7X_VMEM_BYTES - VMEM_RESERVE_BYTES
LOG2E = 1.0 / math.log(2.0)
SQRT_GUARD = 1e-30

f32 = jnp.float32
bf16 = jnp.bfloat16
i32 = jnp.int32


def _dot(a, b):
    return jnp.dot(a, b, preferred_element_type=f32)


def _sigmoid(x):
    return 1.0 / (1.0 + jnp.exp2(x * (-LOG2E)))


def _gelu_tanh(x):
    k1 = -2.0 * math.sqrt(2.0 / math.pi) * LOG2E
    return x / (1.0 + jnp.exp2(x * (k1 + (k1 * 0.044715) * (x * x))))


def _layer_norm(v, g, b):
    mu = jnp.mean(v, axis=-1, keepdims=True)
    c = v - mu
    var = jnp.mean(c * c, axis=-1, keepdims=True)
    return c * lax.rsqrt(var + LN_EPS) * g + b


def _slab_pitch(tb):
    return tb + SUBLANES if (tb // SUBLANES) % 2 == 0 else tb


def _ada_body(c_ref, w_ref, b_ref, w_in_ref, w_out_ref, o_ref, w_in_b_ref, w_out_b_ref):
    o_ref[...] = _dot(c_ref[...].astype(bf16), w_ref[...].astype(bf16)) + b_ref[...]
    w_in_b_ref[...] = w_in_ref[...].astype(bf16)
    w_out_b_ref[...] = w_out_ref[...].astype(bf16)


def _ada_mod(c_all, w_ada, b_ada, w_in, w_out):
    n = c_all.shape[0]
    steps = ADA_STEPS
    bn = 6 * D // steps
    br = D // steps
    return pl.pallas_call(
        _ada_body,
        grid=(steps,),
        in_specs=[
            pl.BlockSpec((n, D), lambda j: (0, 0)),
            pl.BlockSpec((D, bn), lambda j: (0, j)),
            pl.BlockSpec((1, bn), lambda j: (0, j)),
            pl.BlockSpec((br, D_IN), lambda j: (j, 0)),
            pl.BlockSpec((br, D), lambda j: (j, 0)),
        ],
        out_specs=[
            pl.BlockSpec((n, bn), lambda j: (0, j)),
            pl.BlockSpec((br, D_IN), lambda j: (j, 0)),
            pl.BlockSpec((br, D), lambda j: (j, 0)),
        ],
        out_shape=[
            jax.ShapeDtypeStruct((n, 6 * D), f32),
            jax.ShapeDtypeStruct((D, D_IN), bf16),
            jax.ShapeDtypeStruct((D, D), bf16),
        ],
        name="ada_mod",
    )(c_all, w_ada, b_ada.reshape(1, 6 * D), w_in, w_out)


def _mix_body(nb, tb, pitch, n_prev, n_tt, n_cast,
              x_ref, mod_ref, pool0_ref, conv0_ref, h0_ref,
              w_in_ref, w_pool_ref, pscale_ref, w_conv_ref, b_conv_ref,
              w_ax_ref, b_ax_ref, lam_ref, w_out_ref, ln1g_ref, ln1b_ref,
              *rest):
    cast_in = rest[:n_cast]
    x1_ref, npool_ref, nconv_ref, nh_ref = rest[n_cast:n_cast + 4]
    cast_out = rest[n_cast + 4:2 * n_cast + 4]
    slab, xt, ycat, zpool, zconv, a_s, b_s, gl_s, h_s = rest[2 * n_cast + 4:]
    for src, dst in zip(cast_in, cast_out):
        dst[...] = src[...].astype(bf16)
    tm = nb * tb
    tq = tb // N_RB
    rq = tq * nb
    ti = pl.program_id(0) % n_tt

    def mod_part(k):
        return mod_ref[:, pl.ds(k * D, D)]

    @pl.when(ti == 0)
    def _init_state():
        zpool[pl.ds(0, POOL_BUF * nb), :] = pool0_ref[...].reshape(POOL_BUF * nb, D_POOL)
        zconv[pl.ds(0, (CONV_W - 1) * nb), :] = conv0_ref[...].reshape((CONV_W - 1) * nb, D_LRU)
        h_s[...] = h0_ref[...]

    for b in range(nb):
        for j in range(N_SLABS):
            slab[j, pl.ds(b * pitch, tb), :] = x_ref[b, :, pl.ds(j * LANES, LANES)]

    sh1 = mod_part(0)
    sc1 = mod_part(1)
    g1 = mod_part(2)
    nl = -lam_ref[...]
    softplus = jnp.maximum(nl, 0.0) + jnp.log(1.0 + jnp.exp(-jnp.abs(nl)))
    log2a_unit = (-LRU_C * LOG2E) * softplus
    row = lax.broadcasted_iota(i32, (tm, LANES), 0)
    t_loc = lax.shift_right_logical(row, int(math.log2(nb)))
    h = h_s[...]

    for t in range(tb):
        for j in range(N_SLABS):
            xt[t, :, pl.ds(j * LANES, LANES)] = slab[j, pl.ds(t, nb, stride=pitch), :]

    for r in range(N_RB):
        u = xt[pl.ds(r * tq, tq)] * (1.0 + sc1)[None] + sh1[None]
        ycat[pl.ds(r * rq, rq), :] = u.reshape(rq, D).astype(bf16)
    ub = ycat[...]
    zpool[pl.ds(POOL_BUF * nb, tm), :] = _dot(ub, w_in_ref[:, pl.ds(0, D_POOL)])
    zconv[pl.ds((CONV_W - 1) * nb, tm), :] = _dot(ub, w_in_ref[:, pl.ds(D_POOL, D_LRU)])
    half = D_LRU // 2
    for hpart in range(2):
        cols = pl.ds(D_POOL + D_LRU + hpart * half, half)
        gl_s[:, pl.ds(hpart * half, half)] = _gelu_tanh(_dot(ub, w_in_ref[:, cols]))

    t_glob = (ti * tb + t_loc + (1 + n_prev)).astype(f32)
    for c, w in enumerate(POOL_WINDOWS):
        lanes = pl.ds(c * LANES, LANES)
        s = zpool[pl.ds((POOL_BUF + 1 - w) * nb, tm + (w - 1) * nb), lanes]
        step = 1
        while step < w:
            s = s[step * nb:] + s[:-step * nb]
            step *= 2
        cnt = jnp.minimum(f32(w), t_glob)
        dlt = s / cnt - zpool[pl.ds(POOL_BUF * nb, tm), lanes]
        yp = _dot(dlt.astype(bf16), w_pool_ref[c]) * pscale_ref[:, lanes]
        ycat[:, lanes] = yp.astype(bf16)

    for j in range(D_LRU // LANES):
        lanes = pl.ds(j * LANES, LANES)
        xc = b_conv_ref[:, lanes] + zconv[pl.ds(0, tm), lanes] * w_conv_ref[0:1, lanes]
        for k in range(1, CONV_W):
            xc = xc + zconv[pl.ds(k * nb, tm), lanes] * w_conv_ref[k:k + 1, lanes]
        xcb = xc.astype(bf16)
        r = _sigmoid(_dot(xcb, w_ax_ref[0, j]) + b_ax_ref[:, lanes])
        ig = _sigmoid(_dot(xcb, w_ax_ref[1, j]) + b_ax_ref[:, pl.ds(D_LRU + j * LANES, LANES)])
        a = jnp.exp2(r * log2a_unit[:, j * LANES:(j + 1) * LANES])
        a_s[:, lanes] = a
        om = jnp.maximum(1.0 - a * a, 0.0)
        mult = om * lax.rsqrt(jnp.maximum(om, SQRT_GUARD))
        b_s[:, lanes] = mult * ig * xc

    for t in range(tb):
        rows = pl.ds(t * nb, nb)
        h = a_s[rows, :] * h + b_s[rows, :]
        b_s[rows, :] = h
    ycat[:, pl.ds(D_POOL, D_LRU)] = (b_s[...] * gl_s[...]).astype(bf16)

    yc = ycat[...]
    cb = D // N_RB
    for j in range(N_RB):
        cols = pl.ds(j * cb, cb)
        mix = _dot(yc, w_out_ref[:, cols]).reshape(tb, nb, cb)
        xt[:, :, cols] = ALPHA * xt[:, :, cols] + (1.0 + g1[:, j * cb:(j + 1) * cb])[None] * mix
    for r in range(N_RB):
        xn = _layer_norm(xt[pl.ds(r * tq, tq)], ln1g_ref[...][None], ln1b_ref[...][None])
        x1_ref[pl.ds(r * rq, rq), :] = xn.reshape(rq, D)

    h_s[...] = h
    zpool[pl.ds(0, POOL_BUF * nb), :] = zpool[pl.ds(tm, POOL_BUF * nb), :]
    zconv[pl.ds(0, (CONV_W - 1) * nb), :] = zconv[pl.ds(tm, (CONV_W - 1) * nb), :]

    @pl.when(ti == n_tt - 1)
    def _emit_state():
        npool_ref[...] = zpool[pl.ds(0, POOL_BUF * nb), :].reshape(POOL_BUF, nb, D_POOL)
        nconv_ref[...] = zconv[pl.ds(0, (CONV_W - 1) * nb), :].reshape(CONV_W - 1, nb, D_LRU)
        nh_ref[...] = h_s[...]


def _run_mix(x, mod, mod_blk0, pool0, conv0, h0, n_prev, nb, tb, wts, to_cast=()):
    bsz, t_len, _ = x.shape
    n_sb = bsz // nb
    n_tt = t_len // tb
    tm = nb * tb
    pitch = _slab_pitch(tb)
    st = lambda q: (q // n_tt, q % n_tt)
    const2 = lambda q: (0, 0)
    const3 = lambda q: (0, 0, 0)
    const4 = lambda q: (0, 0, 0, 0)
    single = pl.Buffered(1)
    in_specs = [
        pl.BlockSpec((nb, tb, D), lambda q: (*st(q), 0)),
        pl.BlockSpec((nb, 6 * D), lambda q: (mod_blk0 + st(q)[0], 0)),
        pl.BlockSpec((POOL_BUF, nb, D_POOL), lambda q: (0, st(q)[0], 0)),
        pl.BlockSpec((CONV_W - 1, nb, D_LRU), lambda q: (0, st(q)[0], 0)),
        pl.BlockSpec((nb, D_LRU), lambda q: (st(q)[0], 0)),
        pl.BlockSpec((D, D_IN), const2, pipeline_mode=single),
        pl.BlockSpec((4, POOL_GROUP, POOL_GROUP), const3),
        pl.BlockSpec((1, D_POOL), const2),
        pl.BlockSpec((CONV_W, D_LRU), const2),
        pl.BlockSpec((1, D_LRU), const2),
        pl.BlockSpec((2, D_LRU // LANES, LANES, LANES), const4),
        pl.BlockSpec((1, 2 * D_LRU), const2),
        pl.BlockSpec((1, D_LRU), const2),
        pl.BlockSpec((D, D), const2, pipeline_mode=single),
        pl.BlockSpec((1, D), const2),
        pl.BlockSpec((1, D), const2),
    ]
    out_specs = [
        pl.BlockSpec((tm, D), lambda q: (q, 0)),
        pl.BlockSpec((POOL_BUF, nb, D_POOL), lambda q: (0, st(q)[0], 0)),
        pl.BlockSpec((CONV_W - 1, nb, D_LRU), lambda q: (0, st(q)[0], 0)),
        pl.BlockSpec((nb, D_LRU), lambda q: (st(q)[0], 0)),
    ]
    out_shape = [
        jax.ShapeDtypeStruct((bsz * t_len, D), f32),
        jax.ShapeDtypeStruct((POOL_BUF, bsz, D_POOL), f32),
        jax.ShapeDtypeStruct((CONV_W - 1, bsz, D_LRU), f32),
        jax.ShapeDtypeStruct((bsz, D_LRU), f32),
    ]
    for w in to_cast:
        assert w.shape[0] == n_sb * n_tt, (w.shape, n_sb * n_tt)
        blk = (1,) + w.shape[1:]
        in_specs.append(pl.BlockSpec(blk, lambda q: (q, 0, 0)))
        out_specs.append(pl.BlockSpec(blk, lambda q: (q, 0, 0)))
        out_shape.append(jax.ShapeDtypeStruct(w.shape, bf16))
    scratch = [
        pltpu.VMEM((N_SLABS, nb * pitch, LANES), f32),
        pltpu.VMEM((tb, nb, D), f32),
        pltpu.VMEM((tm, D), bf16),
        pltpu.VMEM(((tb + POOL_BUF) * nb, D_POOL), f32),
        pltpu.VMEM(((tb + CONV_W - 1) * nb, D_LRU), f32),
        pltpu.VMEM((tm, D_LRU), f32),
        pltpu.VMEM((tm, D_LRU), f32),
        pltpu.VMEM((tm, D_LRU), f32),
        pltpu.VMEM((nb, D_LRU), f32),
    ]
    body = functools.partial(_mix_body, nb, tb, pitch, n_prev, n_tt, len(to_cast))
    return pl.pallas_call(
        body,
        grid=(n_sb * n_tt,),
        in_specs=in_specs,
        out_specs=out_specs,
        out_shape=out_shape,
        scratch_shapes=scratch,
        compiler_params=pltpu.CompilerParams(
            dimension_semantics=("arbitrary",),
            allow_input_fusion=[i in (2, 3, 4) for i in range(len(in_specs))],
            vmem_limit_bytes=VMEM_LIMIT),
        name=f"mix_nb{nb}_tb{tb}",
    )(x, mod, pool0, conv0, h0, *wts, *to_cast)


def _moe_body(n_half, nb, tb, pitch,
              x1_ref, mod_ref, w_rt_ref, b_rt_ref, tri_ref, wg_ref, wu_ref, wd_ref, ln2g_ref, ln2b_ref,
              y_ref,
              u2, tok_t, tok3, rows_s, cnt_s, ys_all, yacc, slab):
    tm = MOE_ROWS
    hm = nb * tb
    tq = tb // N_RB
    rq = tq * nb
    gs = pl.program_id(1)
    last_gs = N_GROUPS // GPS - 1

    def mod_part(h, k):
        return mod_ref[pl.ds(h * nb, nb), pl.ds(k * D, D)]

    @pl.when(gs == 0)
    def _route():
        for h in range(n_half):
            sh2 = mod_part(h, 3)
            sc2 = mod_part(h, 4)
            for r in range(N_RB):
                rows = pl.ds(h * hm + r * rq, rq)
                v = x1_ref[rows, :].reshape(tq, nb, D) * (1.0 + sc2)[None] + sh2[None]
                u2[rows, :] = v.reshape(rq, D).astype(bf16)
        lt = lax.dot_general(w_rt_ref[...], u2[...], (((1,), (1,)), ((), ())),
                             preferred_element_type=f32) + b_rt_ref[...]
        gl = [lt[k:k + 1, :] for k in range(N_GROUPS)]
        best = gl[0]
        gsel = jnp.zeros_like(best, dtype=i32)
        for k in range(1, N_GROUPS):
            better = gl[k] > best
            best = jnp.where(better, gl[k], best)
            gsel = jnp.where(better, k, gsel)
        denom = jnp.exp(gl[0] - best)
        for k in range(1, N_GROUPS):
            denom = denom + jnp.exp(gl[k] - best)
        p_sel = 1.0 / denom
        es = []
        for j in range(EPG):
            v_j = lt[SUBLANES + j:SUBLANES + j + 1, :]
            for k in range(1, N_GROUPS):
                r0 = SUBLANES * (k + 1) + j
                v_j = jnp.where(gsel == k, lt[r0:r0 + 1, :], v_j)
            es.append(v_j)
        v1 = es[0]
        i1 = jnp.zeros_like(gsel)
        for j in range(1, EPG):
            better = es[j] > v1
            v1 = jnp.where(better, es[j], v1)
            i1 = jnp.where(better, j, i1)
        v2 = jnp.full_like(v1, -jnp.inf)
        i2 = jnp.full_like(gsel, -1)
        for j in range(EPG):
            cand = jnp.logical_and(i1 != j, jnp.logical_or(i2 < 0, es[j] > v2))
            v2 = jnp.where(cand, es[j], v2)
            i2 = jnp.where(cand, j, i2)
        e21 = jnp.exp(v2 - v1)
        w1 = p_sel / (1.0 + e21)
        w2 = p_sel * e21 / (1.0 + e21)
        rid8 = lax.broadcasted_iota(i32, (SUBLANES, tm), 0)
        onehot = jnp.where(rid8 == gsel, 1.0, 0.0)
        blocks = [onehot[:, j * LANES:(j + 1) * LANES] for j in range(tm // LANES)]
        inner = _dot(jnp.concatenate(blocks, axis=0).astype(bf16), tri_ref[...])
        before = jnp.zeros((SUBLANES, 1), f32)
        prefs = []
        for j, blk in enumerate(blocks):
            prefs.append(inner[j * SUBLANES:(j + 1) * SUBLANES, :] + before)
            before = before + jnp.sum(blk, axis=1, keepdims=True)
        pref = jnp.concatenate(prefs, axis=1)
        rank = jnp.sum(onehot * pref, axis=0, keepdims=True)
        n_max = 0
        for k in range(N_GROUPS):
            n_k = jnp.sum(jnp.where(gsel == k, 1.0, 0.0)).astype(i32)
            cnt_s[k] = n_k
            n_max = jnp.maximum(n_max, n_k)
        cnt_s[N_GROUPS] = (n_max > CHUNK).astype(i32)
        rows_s[0:1, :] = gsel
        rows_s[1:2, :] = rank.astype(i32)
        rid = lax.broadcasted_iota(i32, (REC_ROWS, tm), 0)
        rec = jnp.zeros((REC_ROWS, tm), f32)
        for j in range(EPG):
            wj = jnp.where(i1 == j, w1, jnp.where(i2 == j, w2, 0.0))
            hi = wj.astype(bf16).astype(f32)
            mid = (wj - hi).astype(bf16).astype(f32)
            lo = (wj - hi - mid).astype(bf16).astype(f32)
            rec = jnp.where(rid == j, hi, rec)
            rec = jnp.where(rid == ROW_LO + j, mid, rec)
            rec = jnp.where(rid == ROW_LO2 + j, lo, rec)
        rec = jnp.where(rid == ROW_GSEL, gsel.astype(f32), rec)
        rec = jnp.where(rid == ROW_RANK, rank, rec)
        rec = jnp.concatenate([rec, jnp.zeros((LANES - REC_ROWS, tm), f32)], axis=0)
        rt = rec.T
        tok_t[...] = rt
        tok3[...] = rt.astype(bf16)

    def sorted_experts(gi, base):
        g = gs * GPS + gi
        gsel_row = rows_s[0:1, :]
        rank_row = rows_s[1:2, :]
        jj = lax.broadcasted_iota(i32, (CHUNK, tm), 0) + base
        pm = jnp.where(jnp.logical_and(gsel_row == g, rank_row == jj), 1.0, 0.0).astype(bf16)
        xs = _dot(pm, u2[...]).astype(bf16)
        cs3 = _dot(pm, tok3[...])
        cs = (cs3 + pltpu.roll(cs3, LANES - ROW_LO, axis=1)
              + pltpu.roll(cs3, LANES - ROW_LO2, axis=1))
        acts = []
        for e in range(EPG):
            hg = _dot(xs, wg_ref[gi * EPG + e])
            hu = _dot(xs, wu_ref[gi * EPG + e])
            acts.append((hg * _sigmoid(hg) * hu * cs[:, e:e + 1]).astype(bf16))
        hb = jnp.concatenate(acts, axis=1)
        wd_g = wd_ref[pl.ds(gi * EPG, EPG)].reshape(EPG * D_EXPERT, D)
        return _dot(hb, wd_g).astype(bf16)

    for gi in range(GPS):
        first_row = pl.multiple_of((gs * GPS + gi) * CHUNK_PAD, CHUNK_PAD)
        ys_all[pl.ds(first_row, CHUNK), :] = sorted_experts(gi, 0)

    @pl.when(gs == 0)
    def _zero_pad_rows():
        for k in range(N_GROUPS):
            ys_all[pl.ds(k * CHUNK_PAD + CHUNK, CHUNK_PAD - CHUNK), :] = jnp.zeros((CHUNK_PAD - CHUNK, D), bf16)

    has_overflow = cnt_s[N_GROUPS] > 0

    @pl.when(jnp.logical_and(gs == 0, has_overflow))
    def _zero_acc():
        yacc[...] = jnp.zeros((tm, D), f32)

    for gi in range(GPS):
        g = gs * GPS + gi

        def overflow(c, carry, gi=gi, g=g):
            base = c * CHUNK
            ys = sorted_experts(gi, base)
            rec = tok_t[...]
            gsel_col = rec[:, ROW_GSEL:ROW_GSEL + 1]
            rank_col = rec[:, ROW_RANK:ROW_RANK + 1]
            jl = (lax.broadcasted_iota(i32, (tm, CHUNK), 1) + base).astype(f32)
            pt = jnp.where(jnp.logical_and(gsel_col == g.astype(f32), rank_col == jl), 1.0, 0.0).astype(bf16)
            yacc[...] += _dot(pt, ys)
            return carry

        lax.fori_loop(1, pl.cdiv(cnt_s[g], CHUNK), overflow, 0)

    def finish(with_acc):
        tqf = FIN_ROWS // nb
        jl = lax.broadcasted_iota(i32, (FIN_ROWS, N_GROUPS * CHUNK_PAD), 1).astype(f32)
        for h in range(n_half):
            g2 = mod_part(h, 5)
            for r in range(hm // FIN_ROWS):
                rows = pl.ds(h * hm + r * FIN_ROWS, FIN_ROWS)
                rec = tok_t[rows, :]
                gsel_col = rec[:, ROW_GSEL:ROW_GSEL + 1]
                rank_col = rec[:, ROW_RANK:ROW_RANK + 1]
                key = jnp.where(rank_col < f32(CHUNK), gsel_col * f32(CHUNK_PAD) + rank_col, -1.0)
                pt = jnp.where(key == jl, 1.0, 0.0).astype(bf16)
                y = _dot(pt, ys_all[...])
                if with_acc:
                    y = y + yacc[rows, :]
                v = ALPHA * x1_ref[rows, :] + ((1.0 + g2)[None] * y.reshape(tqf, nb, D)).reshape(FIN_ROWS, D)
                yn = _layer_norm(v, ln2g_ref[...], ln2b_ref[...])
                for tl in range(tqf):
                    for j in range(N_SLABS):
                        slab[j, pl.ds(r * tqf + tl, nb, stride=pitch), :] = (
                            yn[tl * nb:(tl + 1) * nb, j * LANES:(j + 1) * LANES])
            for b in range(nb):
                for j in range(N_SLABS):
                    y_ref[h * nb + b, :, pl.ds(j * LANES, LANES)] = slab[j, pl.ds(b * pitch, tb), :]

    @pl.when(jnp.logical_and(gs == last_gs, jnp.logical_not(has_overflow)))
    def _finish():
        finish(False)

    @pl.when(jnp.logical_and(gs == last_gs, has_overflow))
    def _finish_with_overflow():
        finish(True)


def _run_moe(x1, mod, mod_blk0, n_half, nb, tb, out_seqs, out_len, wts):
    n_tiles = x1.shape[0] // MOE_ROWS
    assert n_half * nb * tb == MOE_ROWS
    seq_per_tile = n_half * nb
    n_tt = out_len // tb
    pitch = _slab_pitch(tb)
    st = lambda q: (q // n_tt, q % n_tt)
    const2 = lambda q, g: (0, 0)
    grp3 = lambda q, g: (g, 0, 0)
    single = pl.Buffered(1)
    in_specs = [
        pl.BlockSpec((MOE_ROWS, D), lambda q, g: (q, 0)),
        pl.BlockSpec((seq_per_tile, 6 * D), lambda q, g: (mod_blk0 + st(q)[0], 0)),
        pl.BlockSpec((ROUTE_ROWS, D), const2),
        pl.BlockSpec((ROUTE_ROWS, 1), const2),
        pl.BlockSpec((LANES, LANES), const2),
        pl.BlockSpec((GPS * EPG, D, D_EXPERT), grp3),
        pl.BlockSpec((GPS * EPG, D, D_EXPERT), grp3),
        pl.BlockSpec((GPS * EPG, D_EXPERT, D), grp3),
        pl.BlockSpec((1, D), const2),
        pl.BlockSpec((1, D), const2),
    ]
    scratch = [
        pltpu.VMEM((MOE_ROWS, D), bf16),
        pltpu.VMEM((MOE_ROWS, LANES), f32),
        pltpu.VMEM((MOE_ROWS, LANES), bf16),
        pltpu.VMEM((SUBLANES, MOE_ROWS), i32),
        pltpu.SMEM((N_GROUPS + 1,), i32),
        pltpu.VMEM((N_GROUPS * CHUNK_PAD, D), bf16),
        pltpu.VMEM((MOE_ROWS, D), f32),
        pltpu.VMEM((N_SLABS, nb * pitch, LANES), f32),
    ]
    body = functools.partial(_moe_body, n_half, nb, tb, pitch)
    return pl.pallas_call(
        body,
        grid=(n_tiles, N_GROUPS // GPS),
        in_specs=in_specs,
        out_specs=pl.BlockSpec((seq_per_tile, tb, D), lambda q, g: (*st(q), 0)),
        out_shape=jax.ShapeDtypeStruct((out_seqs, out_len, D), f32),
        scratch_shapes=scratch,
        compiler_params=pltpu.CompilerParams(
            dimension_semantics=("arbitrary", "arbitrary"),
            vmem_limit_bytes=VMEM_LIMIT),
        name=f"moe_nb{nb}_tb{tb}",
    )(x1, mod, *wts)


def _pair_blocks(w):
    n, k, _ = w.shape
    wp = w.reshape(n // 2, 2, k, k)
    eye = jnp.eye(2, dtype=w.dtype)
    return jnp.einsum('phij,hg->phigj', wp, eye).reshape(n // 2, 2 * k, 2 * k)


def kernel(x_prompt, x_sample, c_prompt, c_sample, state_pool, state_conv, state_lru, w_ada, b_ada, w_in, w_pool, pool_scale, w_conv, b_conv, w_a, b_a, w_x, b_x, lru_lambda, w_out, ln1_g, ln1_b, w_group, b_group, w_route, b_route, w_gate, w_up, w_down, ln2_g, ln2_b):
    l = 0
    bp, tp, _ = x_prompt.shape
    bs, ts, _ = x_sample.shape
    mod, w_in_b, w_out_b = _ada_mod(
        jnp.concatenate([c_sample, c_prompt], axis=0), w_ada[l], b_ada[l], w_in[l], w_out[l])

    w_rt = jnp.zeros((ROUTE_ROWS, D), f32).at[0:N_GROUPS].set(w_group[l].T)
    b_rt = jnp.zeros((ROUTE_ROWS,), f32).at[0:N_GROUPS].set(b_group[l])
    for k in range(N_GROUPS):
        r0 = SUBLANES * (k + 1)
        w_rt = w_rt.at[r0:r0 + EPG].set(w_route[l][:, k * EPG:(k + 1) * EPG].T)
        b_rt = b_rt.at[r0:r0 + EPG].set(b_route[l][k * EPG:(k + 1) * EPG])
    tri = jnp.triu(jnp.ones((LANES, LANES), bf16), 1)

    mix_wts = (
        w_in_b,
        w_pool[l].astype(bf16),
        pool_scale[l].reshape(1, D_POOL),
        w_conv[l],
        b_conv[l].reshape(1, D_LRU),
        jnp.stack([_pair_blocks(w_a[l]), _pair_blocks(w_x[l])]).astype(bf16),
        jnp.concatenate([b_a[l], b_x[l]]).reshape(1, 2 * D_LRU),
        lru_lambda[l].reshape(1, D_LRU),
        w_out_b,
        ln1_g[l].reshape(1, D),
        ln1_b[l].reshape(1, D),
    )
    tb_p = MOE_ROWS // bp
    zp = jnp.zeros((POOL_BUF, bp, D_POOL), f32)
    zc = jnp.zeros((CONV_W - 1, bp, D_LRU), f32)
    zh = jnp.zeros((bp, D_LRU), f32)
    x1p, pool_p, conv_p, lru_p, wg_b, wu_b, wd_b = _run_mix(
        x_prompt, mod, bs // bp, zp, zc, zh, 0, bp, tb_p, mix_wts, (w_gate[l], w_up[l], w_down[l]))
    moe_wts = (
        w_rt.astype(bf16),
        b_rt.reshape(ROUTE_ROWS, 1),
        tri,
        wg_b,
        wu_b,
        wd_b,
        ln2_g[l].reshape(1, D),
        ln2_b[l].reshape(1, D),
    )
    yp = _run_moe(x1p, mod, bs // bp, 1, bp, tb_p, bp, tp, moe_wts)

    n_prev_s = min(PAST_LEN, POOL_BUF)
    nb_s = bs // 2
    x1s, pool_s, conv_s, lru_s = _run_mix(
        x_sample, mod, 0, state_pool[l].transpose(1, 0, 2), state_conv[l].transpose(1, 0, 2),
        state_lru[l], n_prev_s, nb_s, ts, mix_wts)
    ys = _run_moe(x1s, mod, 0, 2, nb_s, ts, bs, ts, moe_wts)

    tr = lambda a: a.transpose(1, 0, 2)[None]
    return (yp, ys, tr(pool_p), tr(conv_p), lru_p[None], tr(pool_s), tr(conv_s), lru_s[None])
```
